```python
import math
import jax
import jax.numpy as jnp
from jax import lax
import numpy as np

D_MODEL = 1024
BATCH = 8
SEQ = 8192
DEPTH = 4

CTX_LEN = 256
GRID_W = 64
EPS = 1e-6
ROPE_BASE = 10000.0
Q_BLOCK = 128

MLA_HEADS = 8
MLA_NOPE = 64
MLA_ROPE = 32
MLA_V = 64
MLA_Q_RANK = 512
MLA_KV_RANK = 256
MLA_SCALE = 1.0 / math.sqrt(MLA_NOPE + MLA_ROPE)

DIFF_HEADS = 4
DIFF_DH = 64
DIFF_V = 2 * DIFF_DH
DIFF_SCALE = 1.0 / math.sqrt(DIFF_DH)

MLA_WIDTH = MLA_HEADS * MLA_V
DIFF_WIDTH = DIFF_HEADS * DIFF_V

OFF_CKV = 0
OFF_KR = OFF_CKV + MLA_KV_RANK
OFF_KD = OFF_KR + MLA_ROPE
OFF_VD = OFF_KD + DIFF_HEADS * 2 * DIFF_DH
KV_COLS = OFF_VD + DIFF_HEADS * DIFF_V
OFF_CQ = KV_COLS
OFF_QD = OFF_CQ + MLA_Q_RANK
OFF_GATE = OFF_QD + DIFF_HEADS * 2 * DIFF_DH
IN_COLS = OFF_GATE + 2 * D_MODEL

N_EXPERTS = 16
N_GROUPS = 4
EXPERTS_PER_GROUP = N_EXPERTS // N_GROUPS
TOP_K = 2
D_EXPERT = 512
D_SHARED = 512

N_MOD = 6

kernel_name = 'hybrid_mla_diffattn_grouped_moe_dit'


def rms_norm(x, g):
    xf = x.astype(jnp.float32)
    y = xf * lax.rsqrt(jnp.mean(xf * xf, axis=-1, keepdims=True) + EPS)
    return (y * g.astype(jnp.float32)).astype(x.dtype)


def modulate(h, shift, scale):
    return h * (1 + scale) + shift


def axial_rope_tables(rows, d_rope):
    d_axis = d_rope // 2
    inv = ROPE_BASE ** (-jnp.arange(0, d_axis, 2, dtype=jnp.float32) / d_axis)
    r = jnp.repeat(jnp.arange(rows, dtype=jnp.float32), GRID_W)
    col = jnp.tile(jnp.arange(GRID_W, dtype=jnp.float32), rows)
    ang = jnp.stack([r[:, None] * inv, col[:, None] * inv], axis=1)
    return jnp.cos(ang), jnp.sin(ang)


def apply_rope(x, cos, sin):
    shp = x.shape
    xa = x.reshape(*shp[:-1], 2, 2, shp[-1] // 4)
    x1, x2 = xa[..., 0, :], xa[..., 1, :]
    cos = cos.astype(x.dtype)
    sin = sin.astype(x.dtype)
    out = jnp.stack([x1 * cos - x2 * sin, x1 * sin + x2 * cos], axis=-2)
    return out.reshape(shp)


def sweep_query_blocks(fn, queries):
    B, n = queries[0].shape[:2]
    nb = n // Q_BLOCK
    blocks = tuple(q.reshape(B, nb, Q_BLOCK, *q.shape[2:]).swapaxes(0, 1) for q in queries)
    out = lax.map(lambda qb: fn(*qb), blocks)
    return out.swapaxes(0, 1).reshape(B, n, *out.shape[3:])


def mla_attend(q_nope, q_rope, k_nope, k_rope, v):
    s = (jnp.einsum('bqhd,bkhd->bhqk', q_nope, k_nope)
         + jnp.einsum('bqhr,bkr->bhqk', q_rope, k_rope))
    p = jax.nn.softmax(s.astype(jnp.float32) * MLA_SCALE, axis=-1).astype(v.dtype)
    return jnp.einsum('bhqk,bkhd->bqhd', p, v)


def diff_attend(q, k, v, lam):
    s = jnp.einsum('bqhmd,bkhmd->mbhqk', q, k).astype(jnp.float32) * DIFF_SCALE
    p = jax.nn.softmax(s, axis=-1)
    a = (p[0] - lam * p[1]).astype(v.dtype)
    return jnp.einsum('bhqk,bkhd->bqhd', a, v)


def mla_keys(p, lp, rope):
    B, n = p.shape[:2]
    kv = rms_norm(p[..., OFF_CKV:OFF_KR], lp['mla_kv_norm_g']) @ lp['mla_w_ukv']
    kv = kv.reshape(B, n, MLA_HEADS, MLA_NOPE + MLA_V)
    k_nope = rms_norm(kv[..., :MLA_NOPE], lp['mla_k_g'][:MLA_NOPE])
    k_rope = rms_norm(p[..., OFF_KR:OFF_KD], lp['mla_k_g'][MLA_NOPE:])
    if rope is not None:
        k_rope = apply_rope(k_rope, *rope)
    return k_nope, k_rope, kv[..., MLA_NOPE:]


def mla_queries(p, lp, rope):
    B, n = p.shape[:2]
    q = rms_norm(p[..., OFF_CQ:OFF_QD], lp['mla_q_norm_g']) @ lp['mla_w_uq']
    q = q.reshape(B, n, MLA_HEADS, MLA_NOPE + MLA_ROPE)
    q_nope = rms_norm(q[..., :MLA_NOPE], lp['mla_q_g'][:MLA_NOPE])
    q_rope = rms_norm(q[..., MLA_NOPE:], lp['mla_q_g'][MLA_NOPE:])
    if rope is not None:
        cos, sin = rope
        q_rope = apply_rope(q_rope, cos[:, None], sin[:, None])
    return q_nope, q_rope


def diff_keys(p, lp, rope):
    B, n = p.shape[:2]
    k = rms_norm(p[..., OFF_KD:OFF_VD].reshape(B, n, DIFF_HEADS, 2, DIFF_DH), lp['diff_k_g'])
    if rope is not None:
        cos, sin = rope
        k = apply_rope(k, cos[:, None, None], sin[:, None, None])
    v = p[..., OFF_VD:KV_COLS].reshape(B, n, DIFF_HEADS, DIFF_V)
    return k, v


def diff_queries(p, lp, rope):
    B, n = p.shape[:2]
    q = rms_norm(p[..., OFF_QD:OFF_GATE].reshape(B, n, DIFF_HEADS, 2, DIFF_DH), lp['diff_q_g'])
    if rope is not None:
        cos, sin = rope
        q = apply_rope(q, cos[:, None, None], sin[:, None, None])
    return q


def lambda_value(lam_p, lam_init):
    lf = lam_p.astype(jnp.float32)
    return jnp.exp(jnp.sum(lf[0] * lf[1])) - jnp.exp(jnp.sum(lf[2] * lf[3])) + lam_init


def merge_branches(p, o_mla, o_diff, lp, lam_init):
    B, n = p.shape[:2]
    g = jax.nn.sigmoid(p[..., OFF_GATE:].astype(jnp.float32)).astype(p.dtype)
    o_d = rms_norm(o_diff, lp['diff_sub_g']) * (1.0 - lam_init)
    y = (g[..., :D_MODEL] * (o_mla.reshape(B, n, MLA_WIDTH) @ lp['w_o_mla'])
         + g[..., D_MODEL:] * (o_d.reshape(B, n, DIFF_WIDTH) @ lp['w_o_diff']))
    return y @ lp['w_out']


def grouped_moe(h, router_w, router_b, lp):
    N = h.shape[0]
    aff = jax.nn.sigmoid(h.astype(jnp.float32) @ router_w.astype(jnp.float32))
    sel = (aff + router_b.astype(jnp.float32)).reshape(N, N_GROUPS, EXPERTS_PER_GROUP)
    group_score = jnp.sum(lax.top_k(sel, TOP_K)[0], axis=-1)
    best_group = jnp.argmax(group_score, axis=-1)
    in_group = jnp.take_along_axis(sel, best_group[:, None, None], axis=1)[:, 0]
    local = lax.top_k(in_group, TOP_K)[1]
    idx = best_group[:, None] * EXPERTS_PER_GROUP + local
    w = jnp.take_along_axis(aff, idx, axis=-1)
    w = w / jnp.sum(w, axis=-1, keepdims=True)
    combine = jnp.sum(jax.nn.one_hot(idx, N_EXPERTS, dtype=jnp.float32) * w[..., None],
                      axis=1).astype(h.dtype)
    out = (jax.nn.silu(h @ lp['sh_w_gate']) * (h @ lp['sh_w_up'])) @ lp['sh_w_down']
    for e in range(N_EXPERTS):
        he = jax.nn.silu(h @ lp['moe_w_gate'][e]) * (h @ lp['moe_w_up'][e])
        out = out + combine[:, e:e + 1] * (he @ lp['moe_w_down'][e])
    return out


def setup_inputs(seed: int = 0) -> dict:
    key = jax.random.key(seed)
    ks = iter(jax.random.split(key, 40))

    def nrm(shape, scale):
        return jax.random.normal(next(ks), shape, jnp.float32) * scale

    def gain(shape):
        return 1.0 + 0.02 * jax.random.normal(next(ks), shape, jnp.float32)

    L, D = DEPTH, D_MODEL
    return {
        'x': nrm((BATCH, SEQ, D), 1.0),
        'c': nrm((BATCH, D), 1.0),
        'ctx': nrm((BATCH, CTX_LEN, D), 1.0),
        'c_ctx': nrm((D,), 1.0),
        'ada_w': nrm((L, D, N_MOD * D), 0.5 * D ** -0.5),
        'ada_b': nrm((L, N_MOD * D), 0.02),
        'attn_norm_g': gain((L, D)),
        'ffn_norm_g': gain((L, D)),
        'w_in': nrm((L, D, IN_COLS), D ** -0.5),
        'mla_q_norm_g': gain((L, MLA_Q_RANK)),
        'mla_w_uq': nrm((L, MLA_Q_RANK, MLA_HEADS * (MLA_NOPE + MLA_ROPE)), MLA_Q_RANK ** -0.5),
        'mla_kv_norm_g': gain((L, MLA_KV_RANK)),
        'mla_w_ukv': nrm((L, MLA_KV_RANK, MLA_HEADS * (MLA_NOPE + MLA_V)), MLA_KV_RANK ** -0.5),
        'mla_q_g': gain((L, MLA_NOPE + MLA_ROPE)),
        'mla_k_g': gain((L, MLA_NOPE + MLA_ROPE)),
        'diff_q_g': gain((L, 2, DIFF_DH)),
        'diff_k_g': gain((L, 2, DIFF_DH)),
        'diff_lam': nrm((L, 4, DIFF_DH), 0.1),
        'diff_sub_g': gain((L, DIFF_V)),
        'w_o_mla': nrm((L, MLA_WIDTH, D), MLA_WIDTH ** -0.5),
        'w_o_diff': nrm((L, DIFF_WIDTH, D), DIFF_WIDTH ** -0.5),
        'w_out': nrm((L, D, D), D ** -0.5),
        'router_w': nrm((D, N_EXPERTS), D ** -0.5),
        'router_b': nrm((N_EXPERTS,), 0.01),
        'moe_w_gate': nrm((L, N_EXPERTS, D, D_EXPERT), D ** -0.5),
        'moe_w_up': nrm((L, N_EXPERTS, D, D_EXPERT), D ** -0.5),
        'moe_w_down': nrm((L, N_EXPERTS, D_EXPERT, D), D_EXPERT ** -0.5),
        'sh_w_gate': nrm((L, D, D_SHARED), D ** -0.5),
        'sh_w_up': nrm((L, D, D_SHARED), D ** -0.5),
        'sh_w_down': nrm((L, D_SHARED, D), D_SHARED ** -0.5),
    }


def reference(x, c, ctx, c_ctx, ada_w, ada_b, attn_norm_g, ffn_norm_g, w_in,
              mla_q_norm_g, mla_w_uq, mla_kv_norm_g, mla_w_ukv, mla_q_g, mla_k_g,
              diff_q_g, diff_k_g, diff_lam, diff_sub_g, w_o_mla, w_o_diff, w_out,
              router_w, router_b, moe_w_gate, moe_w_up, moe_w_down,
              sh_w_gate, sh_w_up, sh_w_down):
    B, n, _ = x.shape
    n_ctx = ctx.shape[1]
    rows = n // GRID_W
    rope_m = axial_rope_tables(rows, MLA_ROPE)
    rope_d = axial_rope_tables(rows, DIFF_DH)
    silu_c = jax.nn.silu(c)
    silu_cc = jax.nn.silu(c_ctx)
    xc = ctx
    for l in range(DEPTH):
        last = l == DEPTH - 1
        lam_init = 0.8 - 0.6 * math.exp(-0.3 * l)
        lp = {
            'mla_q_norm_g': mla_q_norm_g[l], 'mla_w_uq': mla_w_uq[l],
            'mla_kv_norm_g': mla_kv_norm_g[l], 'mla_w_ukv': mla_w_ukv[l],
            'mla_q_g': mla_q_g[l], 'mla_k_g': mla_k_g[l],
            'diff_q_g': diff_q_g[l], 'diff_k_g': diff_k_g[l], 'diff_sub_g': diff_sub_g[l],
            'w_o_mla': w_o_mla[l], 'w_o_diff': w_o_diff[l], 'w_out': w_out[l],
            'moe_w_gate': moe_w_gate[l], 'moe_w_up': moe_w_up[l], 'moe_w_down': moe_w_down[l],
            'sh_w_gate': sh_w_gate[l], 'sh_w_up': sh_w_up[l], 'sh_w_down': sh_w_down[l],
        }
        mod = (silu_c @ ada_w[l] + ada_b[l])[:, None, :]
        sh_a, sc_a, g_a, sh_f, sc_f, g_f = jnp.split(mod, N_MOD, axis=-1)
        n_mod_ctx = 2 if last else N_MOD
        mod_c = silu_cc @ ada_w[l][:, :n_mod_ctx * D_MODEL] + ada_b[l][:n_mod_ctx * D_MODEL]
        mods_c = jnp.split(mod_c, n_mod_ctx)

        h_lat = modulate(rms_norm(x, attn_norm_g[l]), sh_a, sc_a)
        h_ctx = modulate(rms_norm(xc, attn_norm_g[l]), mods_c[0], mods_c[1])
        p_lat = h_lat @ w_in[l]
        p_ctx = h_ctx @ (w_in[l][:, :KV_COLS] if last else w_in[l])

        kn_c, kr_c, vm_c = mla_keys(p_ctx, lp, None)
        kd_c, vd_c = diff_keys(p_ctx, lp, None)
        kn_l, kr_l, vm_l = mla_keys(p_lat, lp, rope_m)
        kd_l, vd_l = diff_keys(p_lat, lp, rope_d)
        kn_all = jnp.concatenate([kn_c, kn_l], axis=1)
        kr_all = jnp.concatenate([kr_c, kr_l], axis=1)
        vm_all = jnp.concatenate([vm_c, vm_l], axis=1)
        kd_all = jnp.concatenate([kd_c, kd_l], axis=1)
        vd_all = jnp.concatenate([vd_c, vd_l], axis=1)
        lam = lambda_value(diff_lam[l], lam_init)

        qn_l, qr_l = mla_queries(p_lat, lp, rope_m)
        qd_l = diff_queries(p_lat, lp, rope_d)
        o_m = sweep_query_blocks(lambda qn, qr: mla_attend(qn, qr, kn_all, kr_all, vm_all),
                                 (qn_l, qr_l))
        o_d = sweep_query_blocks(lambda q: diff_attend(q, kd_all, vd_all, lam), (qd_l,))
        x = x + g_a * merge_branches(p_lat, o_m, o_d, lp, lam_init)
        if not last:
            qn_c, qr_c = mla_queries(p_ctx, lp, None)
            qd_c = diff_queries(p_ctx, lp, None)
            oc_m = mla_attend(qn_c, qr_c, kn_c, kr_c, vm_c)
            oc_d = diff_attend(qd_c, kd_c, vd_c, lam)
            xc = xc + mods_c[2] * merge_branches(p_ctx, oc_m, oc_d, lp, lam_init)

        f_lat = modulate(rms_norm(x, ffn_norm_g[l]), sh_f, sc_f).reshape(-1, D_MODEL)
        if last:
            y = grouped_moe(f_lat, router_w, router_b, lp)
            x = x + g_f * y.reshape(B, n, D_MODEL)
        else:
            f_ctx = modulate(rms_norm(xc, ffn_norm_g[l]), mods_c[3], mods_c[4]).reshape(-1, D_MODEL)
            y = grouped_moe(jnp.concatenate([f_lat, f_ctx], axis=0), router_w, router_b, lp)
            x = x + g_f * y[:B * n].reshape(B, n, D_MODEL)
            xc = xc + mods_c[5] * y[B * n:].reshape(B, n_ctx, D_MODEL)
    return x
```

```python
import functools
import math

import jax
import jax.numpy as jnp
from jax import lax
from jax.experimental import pallas as pl
from jax.experimental.pallas import tpu as pltpu

F32 = jnp.float32
BF16 = jnp.bfloat16

D_MODEL = 1024
GRID_W = 64
EPS = 1e-6
ROPE_BASE = 10000.0
N_MOD = 6

MLA_HEADS = 8
MLA_NOPE = 64
MLA_ROPE = 32
MLA_V = 64
MLA_Q_RANK = 512
MLA_KV_RANK = 256
MLA_SCALE = 1.0 / math.sqrt(MLA_NOPE + MLA_ROPE)

DIFF_HEADS = 4
DIFF_DH = 64
DIFF_V = 2 * DIFF_DH
DIFF_SCALE = 1.0 / math.sqrt(DIFF_DH)

N_EXPERTS = 16
EXPERTS_PER_GROUP = 4
N_GROUPS = N_EXPERTS // EXPERTS_PER_GROUP
D_EXPERT = 512

LANES = 128
MXU_DIM = 256
HEAD_PAD = LANES
MLA_W = MLA_HEADS * HEAD_PAD
DIFF_QK_W = DIFF_HEADS * 2 * DIFF_DH
DIFF_V_PAD = 2 * LANES
DIFF_V_W = DIFF_HEADS * DIFF_V_PAD

A_CKV = 0
A_KR = A_CKV + MLA_KV_RANK
A_KD = A_KR + LANES
A_VD = A_KD + DIFF_QK_W
A_CQ = A_VD + DIFF_HEADS * DIFF_V
A_QD = A_CQ + MLA_Q_RANK
A_COLS = A_QD + DIFF_QK_W

ROPE_TAB_W = 6 * LANES

ROW_TILE = 256
MOE_TILE = 512
KV_CHUNK = 512
VMEM_LIMIT = 56 * 1024 * 1024


def _sigmoid(x):
    return 1.0 / (1.0 + jnp.exp(-x))


def _dot(a, b):
    return jnp.dot(a, b, preferred_element_type=F32)


def _split_bf16(a):
    hi = a.astype(BF16)
    lo = (a - hi.astype(F32)).astype(BF16)
    return hi, lo


def _dot3(a, b):
    a_hi, a_lo = _split_bf16(a)
    b_hi, b_lo = _split_bf16(b)
    return _dot(a_hi, b_hi) + _dot(a_lo, b_hi) + _dot(a_hi, b_lo)


def _mods_kernel(c_ref, w_ref, b_ref, o_ref):
    c = c_ref[...]
    a = c * _sigmoid(c)
    o_ref[...] = _dot3(a, w_ref[...]) + b_ref[...]


def _mods_call(c_all, ada_w, ada_b):
    depth, d, n = ada_w.shape
    tn = 1536
    rows = c_all.shape[0]
    return pl.pallas_call(
        _mods_kernel,
        grid=(depth, n // tn),
        in_specs=[
            pl.BlockSpec((rows, d), lambda l, j: (0, 0)),
            pl.BlockSpec((None, d, tn), lambda l, j: (l, 0, j)),
            pl.BlockSpec((None, 1, tn), lambda l, j: (l, 0, j)),
        ],
        out_specs=pl.BlockSpec((None, rows, tn), lambda l, j: (l, 0, j)),
        out_shape=jax.ShapeDtypeStruct((depth, rows, n), F32),
        compiler_params=pltpu.CompilerParams(
            dimension_semantics=("parallel", "parallel"), vmem_limit_bytes=VMEM_LIMIT),
        name="adaln_mods",
    )(c_all, ada_w, ada_b.reshape(depth, 1, n))


def _rms(x, g):
    return x * lax.rsqrt(jnp.mean(x * x, axis=-1, keepdims=True) + EPS) * g


def _seg_mean_sq(x, b_ref):
    sq = (x * x).astype(BF16)
    b = b_ref[...]
    parts = [_dot(sq[:, c:c + MXU_DIM], b) for c in range(0, x.shape[1], MXU_DIM)]
    return parts[0] if len(parts) == 1 else jnp.concatenate(parts, axis=1)


def _rope(x, cos, sin_a, sin_b, half):
    return (x * cos + pltpu.roll(x, LANES - half, axis=1) * sin_a
            + pltpu.roll(x, half, axis=1) * sin_b)


def _proj_kernel(x_ref, mod_ref, g_ref, win_ref, wukv_ref, wuq_ref, gkv_ref, gcq_ref, gk_ref, gkr_ref,
                 gq_ref, gkd_ref, gqd_ref, bk_ref, bq_ref, bd_ref, vone_ref, rope_ref,
                 kmla_ref, vmla_ref, qmla_ref, kd_ref, vd_ref, qd_ref):
    x = x_ref[...]
    h = _rms(x, g_ref[...]) * (1.0 + mod_ref[1:2, :]) + mod_ref[0:1, :]
    p = _dot(h.astype(BF16), win_ref[...])

    cos_m = rope_ref[:, 0 * LANES:1 * LANES]
    sin_ma = rope_ref[:, 1 * LANES:2 * LANES]
    sin_mb = rope_ref[:, 2 * LANES:3 * LANES]
    cos_d = rope_ref[:, 3 * LANES:4 * LANES]
    sin_da = rope_ref[:, 4 * LANES:5 * LANES]
    sin_db = rope_ref[:, 5 * LANES:6 * LANES]

    ckv = _rms(p[:, A_CKV:A_KR], gkv_ref[...])
    kv = _dot(ckv.astype(BF16), wukv_ref[...])
    kn = kv[:, :MLA_W]
    kn = kn * lax.rsqrt(_seg_mean_sq(kn, bk_ref) + EPS) * gk_ref[...]
    kr = p[:, A_KR:A_KD]
    kr = kr * lax.rsqrt(jnp.sum(kr * kr, axis=-1, keepdims=True) * (1.0 / MLA_ROPE) + EPS) * gkr_ref[...]
    kr = _rope(kr, cos_m, sin_ma, sin_mb, MLA_ROPE // 4)
    for hd in range(MLA_HEADS):
        sl = slice(hd * HEAD_PAD, (hd + 1) * HEAD_PAD)
        kmla_ref[:, sl] = (kn[:, sl] + kr).astype(BF16)
    vmla_ref[...] = (kv[:, MLA_W:] + vone_ref[...]).astype(BF16)

    cq = _rms(p[:, A_CQ:A_QD], gcq_ref[...])
    q = _dot(cq.astype(BF16), wuq_ref[...])
    q = q * lax.rsqrt(_seg_mean_sq(q, bq_ref) + EPS) * gq_ref[...]
    for hd in range(MLA_HEADS):
        sl = slice(hd * HEAD_PAD, (hd + 1) * HEAD_PAD)
        qmla_ref[:, sl] = _rope(q[:, sl], cos_m, sin_ma, sin_mb, MLA_ROPE // 4).astype(BF16)

    kd = p[:, A_KD:A_VD]
    kd = kd * lax.rsqrt(_seg_mean_sq(kd, bd_ref) + EPS) * gkd_ref[...]
    qd = p[:, A_QD:A_COLS]
    qd = qd * lax.rsqrt(_seg_mean_sq(qd, bd_ref) + EPS) * gqd_ref[...]
    for hd in range(DIFF_HEADS):
        sl = slice(hd * LANES, (hd + 1) * LANES)
        kd_ref[:, sl] = _rope(kd[:, sl], cos_d, sin_da, sin_db, DIFF_DH // 4).astype(BF16)
        qd_ref[:, sl] = _rope(qd[:, sl], cos_d, sin_da, sin_db, DIFF_DH // 4).astype(BF16)

    vd = p[:, A_VD:A_CQ]
    tm = x.shape[0]
    ones_blk = (lax.broadcasted_iota(jnp.int32, (tm, LANES), 1) == 0).astype(BF16)
    for hd in range(DIFF_HEADS):
        vd_ref[:, hd * DIFF_V_PAD:hd * DIFF_V_PAD + DIFF_V] = vd[:, hd * DIFF_V:(hd + 1) * DIFF_V].astype(BF16)
        vd_ref[:, hd * DIFF_V_PAD + DIFF_V:(hd + 1) * DIFF_V_PAD] = ones_blk


def _row_class_maps(n_lat_tiles, tiles_per_batch, n_batch):
    def mod_map(i):
        return (jnp.where(i < n_lat_tiles, i // tiles_per_batch, n_batch), 0, 0)

    def rope_map(i):
        return (jnp.where(i < n_lat_tiles, i % tiles_per_batch, tiles_per_batch), 0)

    return mod_map, rope_map


def _const_spec(shape):
    return pl.BlockSpec(shape, lambda i: tuple(0 for _ in shape))


def _proj_call(x_flat, mods, wl, rope_tab, n_batch, seq):
    rows = x_flat.shape[0]
    tm = ROW_TILE
    n_lat_tiles = n_batch * seq // tm
    mod_map, rope_map = _row_class_maps(n_lat_tiles, seq // tm, n_batch)
    row_spec = lambda w: pl.BlockSpec((tm, w), lambda i: (i, 0))
    consts = [wl["g_attn"], wl["w_in_a"], wl["w_ukv"], wl["w_uq"], wl["g_kv"], wl["g_cq"], wl["g_k"], wl["g_kr"],
              wl["g_q"], wl["g_kd"], wl["g_qd"], wl["b_k"], wl["b_q"], wl["b_d"], wl["v_one"]]
    out_w = [MLA_W, MLA_W, MLA_W, DIFF_QK_W, DIFF_V_W, DIFF_QK_W]
    return pl.pallas_call(
        _proj_kernel,
        grid=(rows // tm,),
        in_specs=[row_spec(D_MODEL), pl.BlockSpec((None, N_MOD, D_MODEL), mod_map)]
        + [_const_spec(c.shape) for c in consts]
        + [pl.BlockSpec((tm, ROPE_TAB_W), rope_map)],
        out_specs=[row_spec(w) for w in out_w],
        out_shape=[jax.ShapeDtypeStruct((rows, w), BF16) for w in out_w],
        compiler_params=pltpu.CompilerParams(
            dimension_semantics=("parallel",), vmem_limit_bytes=VMEM_LIMIT),
        name="proj_qkv",
    )(x_flat, mods, *consts, rope_tab)


def _attn_kernel(*refs, n_maps, tq, tk, n_lat_chunks, n_lat_qtiles, lam_init):
    if n_maps == 2:
        q_ref, kc_ref, vc_ref, kl_ref, vl_ref, lam_ref, subg_ref, o_ref = refs
    else:
        q_ref, kc_ref, vc_ref, kl_ref, vl_ref, o_ref = refs
    qi = pl.program_id(2)
    q = q_ref[...]
    if n_maps == 2:
        lane = lax.broadcasted_iota(jnp.int32, q.shape, 1)
        zero = jnp.zeros_like(q)
        q = jnp.concatenate([jnp.where(lane < DIFF_DH, q, zero), jnp.where(lane >= DIFF_DH, q, zero)], axis=0)
    rows = q.shape[0]
    vw = vc_ref.shape[1]
    sum_col = vw // 2

    def step(k, v, m, acc):
        s = lax.dot_general(q, k, (((1,), (1,)), ((), ())), preferred_element_type=F32)
        m_new = jnp.maximum(m, jnp.max(s, axis=-1, keepdims=True))
        alpha = jnp.exp(m - m_new)
        p = jnp.exp(s - m_new)
        acc = alpha * acc + _dot(p.astype(BF16), v)
        return m_new, acc

    m0 = jnp.full((rows, 1), -jnp.inf, F32)
    acc0 = jnp.zeros((rows, vw), F32)
    m, acc = step(kc_ref[...], vc_ref[...], m0, acc0)

    def body(j, carry):
        off = pl.multiple_of(j * tk, tk)
        return step(kl_ref[pl.ds(off, tk), :], vl_ref[pl.ds(off, tk), :], *carry)

    n_steps = jnp.where(qi < n_lat_qtiles, n_lat_chunks, 0)
    m, acc = lax.fori_loop(0, n_steps, body, (m, acc))

    inv_l = 1.0 / acc[:, sum_col:sum_col + 1]
    if n_maps == 1:
        o_ref[...] = (acc * inv_l).astype(o_ref.dtype)
    else:
        lf = lam_ref[...]
        lam = (jnp.exp(jnp.sum(lf[0:1, :] * lf[1:2, :], axis=-1, keepdims=True))
               - jnp.exp(jnp.sum(lf[2:3, :] * lf[3:4, :], axis=-1, keepdims=True)) + lam_init)
        o = acc[:, :DIFF_V] * inv_l
        o = o[:tq] - lam * o[tq:]
        o_ref[...] = _rms(o, subg_ref[...]).astype(o_ref.dtype)


def _attn_call(q, k, v, n_batch, seq, n_ctx, n_heads, n_maps, extra, lam_init, name):
    rows = q.shape[0]
    tq = n_ctx
    tk = KV_CHUNK
    nq = seq // tq
    vw = v.shape[1] // n_heads
    ctx_blk0 = n_batch * seq // n_ctx

    def q_map(b, h, qi):
        return (jnp.where(qi < nq, b * nq + qi, ctx_blk0 + b), h)

    in_specs = [
        pl.BlockSpec((tq, LANES), q_map),
        pl.BlockSpec((n_ctx, LANES), lambda b, h, qi: (ctx_blk0 + b, h)),
        pl.BlockSpec((n_ctx, vw), lambda b, h, qi: (ctx_blk0 + b, h)),
        pl.BlockSpec((seq, LANES), lambda b, h, qi: (b, h)),
        pl.BlockSpec((seq, vw), lambda b, h, qi: (b, h)),
    ] + [pl.BlockSpec(e.shape, lambda b, h, qi: (0, 0)) for e in extra]
    kern = functools.partial(_attn_kernel, n_maps=n_maps, tq=tq, tk=tk, n_lat_chunks=seq // tk,
                             n_lat_qtiles=nq, lam_init=lam_init)
    return pl.pallas_call(
        kern,
        grid=(n_batch, n_heads, nq + 1),
        in_specs=in_specs,
        out_specs=pl.BlockSpec((tq, LANES), q_map),
        out_shape=jax.ShapeDtypeStruct((rows, n_heads * LANES), BF16),
        compiler_params=pltpu.CompilerParams(
            dimension_semantics=("parallel", "parallel", "arbitrary"), vmem_limit_bytes=VMEM_LIMIT),
        name=name,
    )(q, k, v, k, v, *extra)


def _route(logits, bias):
    lane = lax.broadcasted_iota(jnp.int32, logits.shape, 1)
    is_exp = (lane >= 1) & (lane <= N_EXPERTS)
    grp = (lane - 1) >> 2
    neg = jnp.float32(-jnp.inf)
    big = jnp.int32(1 << 20)
    aff = _sigmoid(logits)
    sel = jnp.where(is_exp, aff + bias, neg)

    def first_argmax(v):
        mx = jnp.max(v, axis=-1, keepdims=True)
        ix = jnp.min(jnp.where(v == mx, lane, big), axis=-1, keepdims=True)
        return mx, ix

    best = None
    for g in range(N_GROUPS):
        vg = jnp.where(grp == g, sel, neg)
        m1, i1 = first_argmax(vg)
        m2, i2 = first_argmax(jnp.where(lane == i1, neg, vg))
        score = m1 + m2
        if best is None:
            best = (score, i1, i2)
        else:
            upd = score > best[0]
            best = (jnp.where(upd, score, best[0]), jnp.where(upd, i1, best[1]), jnp.where(upd, i2, best[2]))
    _, i1, i2 = best
    a1 = jnp.sum(jnp.where(lane == i1, aff, 0.0), axis=-1, keepdims=True)
    a2 = jnp.sum(jnp.where(lane == i2, aff, 0.0), axis=-1, keepdims=True)
    tot = a1 + a2
    comb = jnp.where(lane == i1, a1 / tot, 0.0) + jnp.where(lane == i2, a2 / tot, 0.0)
    return jnp.where(lane == 0, 1.0, comb)


def _merge_kernel(x_ref, mod_ref, ga_ref, gf_ref, om_ref, od_ref, wgate_ref, wom_ref, wod_ref, wout_ref,
                  rw_ref, rb_ref, xo_ref, f_ref, comb_ref):
    x = x_ref[...]
    h = _rms(x, ga_ref[...]) * (1.0 + mod_ref[1:2, :]) + mod_ref[0:1, :]
    gates = _sigmoid(_dot(h.astype(BF16), wgate_ref[...]))
    y = (gates[:, :D_MODEL] * _dot(om_ref[...], wom_ref[...])
         + gates[:, D_MODEL:] * _dot(od_ref[...], wod_ref[...]))
    xn = x + mod_ref[2:3, :] * _dot(y.astype(BF16), wout_ref[...])
    xo_ref[...] = xn
    f = _rms(xn, gf_ref[...]) * (1.0 + mod_ref[4:5, :]) + mod_ref[3:4, :]
    f_ref[...] = f.astype(BF16)
    comb_ref[...] = _route(_dot3(f, rw_ref[...]), rb_ref[...])


def _merge_call(x_flat, mods, o_mla, o_diff, wl, router_w, router_b, n_batch, seq):
    rows = x_flat.shape[0]
    tm = ROW_TILE
    mod_map, _ = _row_class_maps(n_batch * seq // tm, seq // tm, n_batch)
    row_spec = lambda w: pl.BlockSpec((tm, w), lambda i: (i, 0))
    consts = [wl["w_gate"], wl["w_o_mla"], wl["w_o_diff"], wl["w_out"], router_w, router_b]
    return pl.pallas_call(
        _merge_kernel,
        grid=(rows // tm,),
        in_specs=[row_spec(D_MODEL), pl.BlockSpec((None, N_MOD, D_MODEL), mod_map),
                  _const_spec(wl["g_attn"].shape), _const_spec(wl["g_ffn"].shape),
                  row_spec(MLA_W), row_spec(DIFF_HEADS * LANES)]
        + [_const_spec(c.shape) for c in consts],
        out_specs=[row_spec(D_MODEL), row_spec(D_MODEL), row_spec(LANES)],
        out_shape=[jax.ShapeDtypeStruct((rows, D_MODEL), F32), jax.ShapeDtypeStruct((rows, D_MODEL), BF16),
                   jax.ShapeDtypeStruct((rows, LANES), F32)],
        compiler_params=pltpu.CompilerParams(
            dimension_semantics=("parallel",), vmem_limit_bytes=VMEM_LIMIT),
        name="merge_router",
    )(x_flat, mods, wl["g_attn"], wl["g_ffn"], o_mla, o_diff, *consts)


def _moe_kernel(f_ref, comb_ref, x_ref, mod_ref, wgu_ref, wd_ref, o_ref, acc_ref):
    e = pl.program_id(1)

    @pl.when(e == 0)
    def _():
        acc_ref[...] = jnp.zeros_like(acc_ref)

    gu = _dot(f_ref[...], wgu_ref[...])
    g = gu[:, :D_EXPERT]
    he = g * _sigmoid(g) * gu[:, D_EXPERT:]
    comb = comb_ref[...]
    lane = lax.broadcasted_iota(jnp.int32, comb.shape, 1)
    c = jnp.sum(jnp.where(lane == e, comb, 0.0), axis=-1, keepdims=True)
    acc_ref[...] += _dot((he * c).astype(BF16), wd_ref[...])

    @pl.when(e == pl.num_programs(1) - 1)
    def _():
        o_ref[...] = x_ref[...] + mod_ref[5:6, :] * acc_ref[...]


def _moe_call(f, comb, x_flat, mods, w_gu, w_down, n_batch, seq, out_rows):
    tm = MOE_TILE
    mod_map1, _ = _row_class_maps(n_batch * seq // tm, seq // tm, n_batch)
    n_slots = w_gu.shape[0]
    return pl.pallas_call(
        _moe_kernel,
        grid=(out_rows // tm, n_slots),
        in_specs=[
            pl.BlockSpec((tm, D_MODEL), lambda i, e: (i, 0)),
            pl.BlockSpec((tm, LANES), lambda i, e: (i, 0)),
            pl.BlockSpec((tm, D_MODEL), lambda i, e: (i, 0)),
            pl.BlockSpec((None, N_MOD, D_MODEL), lambda i, e: mod_map1(i)),
            pl.BlockSpec((None, D_MODEL, 2 * D_EXPERT), lambda i, e: (e, 0, 0)),
            pl.BlockSpec((None, D_EXPERT, D_MODEL), lambda i, e: (e, 0, 0)),
        ],
        out_specs=pl.BlockSpec((tm, D_MODEL), lambda i, e: (i, 0)),
        out_shape=jax.ShapeDtypeStruct((out_rows, D_MODEL), F32),
        scratch_shapes=[pltpu.VMEM((tm, D_MODEL), F32)],
        compiler_params=pltpu.CompilerParams(
            dimension_semantics=("parallel", "arbitrary"), vmem_limit_bytes=VMEM_LIMIT),
        name="moe_ffn",
    )(f, comb, x_flat, mods, w_gu, w_down)


def _axial_angles(seq, d_rope):
    d_axis = d_rope // 2
    inv = ROPE_BASE ** (-jnp.arange(0, d_axis, 2, dtype=F32) / d_axis)
    t = jnp.arange(seq, dtype=jnp.int32)
    r = (t // GRID_W).astype(F32)
    col = (t % GRID_W).astype(F32)
    ang_r = r[:, None] * inv
    ang_c = col[:, None] * inv
    cos = jnp.concatenate([jnp.cos(ang_r)] * 2 + [jnp.cos(ang_c)] * 2, axis=1)
    sin = jnp.concatenate([jnp.sin(ang_r)] * 2 + [jnp.sin(ang_c)] * 2, axis=1)
    nf = d_rope // 4
    first = jnp.concatenate([jnp.ones((nf,), F32), jnp.zeros((nf,), F32)] * 2)
    return cos, -sin * first, sin * (1.0 - first)


def _rope_table(seq, tile):
    cm, sma, smb = _axial_angles(seq, MLA_ROPE)
    pad_l = jnp.zeros((seq, MLA_NOPE), F32)
    pad_r = jnp.zeros((seq, HEAD_PAD - MLA_NOPE - MLA_ROPE), F32)
    cos_m = jnp.concatenate([pad_l + 1.0, cm, pad_r + 1.0], axis=1)
    sin_ma = jnp.concatenate([pad_l, sma, pad_r], axis=1)
    sin_mb = jnp.concatenate([pad_l, smb, pad_r], axis=1)
    cd, sda, sdb = _axial_angles(seq, DIFF_DH)
    tab = jnp.concatenate([cos_m, sin_ma, sin_mb] + [jnp.concatenate([a, a], axis=1) for a in (cd, sda, sdb)],
                          axis=1)
    ident = jnp.concatenate([jnp.ones((tile, LANES), F32), jnp.zeros((tile, 2 * LANES), F32)] * 2, axis=1)
    return jnp.concatenate([tab, ident], axis=0)


def _block_diag_mean(seg_sizes):
    m = jnp.zeros((MXU_DIM, MXU_DIM), F32)
    off = 0
    for size, active in seg_sizes:
        if active:
            m = m.at[off:off + size, off:off + size].set(1.0 / size)
        off += size
    assert off == MXU_DIM
    return m.astype(BF16)


def _prep_weights(attn_norm_g, ffn_norm_g, w_in, mla_q_norm_g, mla_w_uq, mla_kv_norm_g, mla_w_ukv, mla_q_g,
                  mla_k_g, diff_q_g, diff_k_g, diff_sub_g, w_o_mla, w_o_diff, w_out, moe_w_gate, moe_w_up,
                  moe_w_down, sh_w_gate, sh_w_up, sh_w_down):
    depth = w_in.shape[0]
    d = D_MODEL
    z = lambda w: jnp.zeros((depth, d, w), F32)
    off_kr = MLA_KV_RANK
    off_kd = off_kr + MLA_ROPE
    off_vd = off_kd + DIFF_QK_W
    off_cq = off_vd + DIFF_HEADS * DIFF_V
    off_qd = off_cq + MLA_Q_RANK
    off_gate = off_qd + DIFF_QK_W
    w_in_a = jnp.concatenate([
        w_in[:, :, :off_kr], z(MLA_NOPE), w_in[:, :, off_kr:off_kd], z(HEAD_PAD - MLA_NOPE - MLA_ROPE),
        w_in[:, :, off_kd:off_gate]], axis=2).astype(BF16)
    w_gate = w_in[:, :, off_gate:].astype(BF16)

    ukv = mla_w_ukv.reshape(depth, MLA_KV_RANK, MLA_HEADS, MLA_NOPE + MLA_V)
    pad_k = jnp.zeros((depth, MLA_KV_RANK, MLA_HEADS, HEAD_PAD - MLA_NOPE), F32)
    pad_v = jnp.zeros((depth, MLA_KV_RANK, MLA_HEADS, HEAD_PAD - MLA_V), F32)
    w_ukv = jnp.concatenate([
        jnp.concatenate([ukv[..., :MLA_NOPE], pad_k], axis=-1).reshape(depth, MLA_KV_RANK, MLA_W),
        jnp.concatenate([ukv[..., MLA_NOPE:], pad_v], axis=-1).reshape(depth, MLA_KV_RANK, MLA_W)],
        axis=-1).astype(BF16)
    uq = mla_w_uq.reshape(depth, MLA_Q_RANK, MLA_HEADS, MLA_NOPE + MLA_ROPE)
    w_uq = jnp.pad(uq, ((0, 0), (0, 0), (0, 0), (0, HEAD_PAD - MLA_NOPE - MLA_ROPE))).reshape(
        depth, MLA_Q_RANK, MLA_W).astype(BF16)
    wom = w_o_mla.reshape(depth, MLA_HEADS, MLA_V, d)
    w_o_mla_p = jnp.pad(wom, ((0, 0), (0, 0), (0, HEAD_PAD - MLA_V), (0, 0))).reshape(depth, MLA_W, d).astype(BF16)

    def per_head(g96, scale):
        g = jnp.pad(g96 * scale, ((0, 0), (0, HEAD_PAD - MLA_NOPE - MLA_ROPE)))
        return jnp.tile(g, (1, MLA_HEADS)).reshape(depth, 1, MLA_W)

    k_nope_only = jnp.concatenate([mla_k_g[:, :MLA_NOPE], jnp.zeros((depth, MLA_ROPE), F32)], axis=1)
    g_k = per_head(k_nope_only, 1.0)
    g_kr = jnp.concatenate([jnp.zeros((depth, MLA_NOPE), F32), mla_k_g[:, MLA_NOPE:],
                            jnp.zeros((depth, HEAD_PAD - MLA_NOPE - MLA_ROPE), F32)], axis=1).reshape(depth, 1, LANES)
    g_q = per_head(mla_q_g, MLA_SCALE)
    g_kd = jnp.tile(diff_k_g.reshape(depth, 2 * DIFF_DH), (1, DIFF_HEADS)).reshape(depth, 1, DIFF_QK_W)
    g_qd = jnp.tile(diff_q_g.reshape(depth, 2 * DIFF_DH) * DIFF_SCALE, (1, DIFF_HEADS)).reshape(depth, 1, DIFF_QK_W)
    lam_inits = [0.8 - 0.6 * math.exp(-0.3 * l) for l in range(depth)]
    g_sub = diff_sub_g * (1.0 - jnp.asarray(lam_inits, F32))[:, None]

    v_one = jnp.tile((jnp.arange(HEAD_PAD) == MLA_V).astype(F32), MLA_HEADS).reshape(1, MLA_W)
    b_k = _block_diag_mean([(MLA_NOPE, True), (HEAD_PAD - MLA_NOPE, False)] * 2)
    b_q = _block_diag_mean([(MLA_NOPE, True), (MLA_ROPE, True), (HEAD_PAD - MLA_NOPE - MLA_ROPE, False)] * 2)
    b_d = _block_diag_mean([(DIFF_DH, True)] * 4)

    w_gu = jnp.concatenate([
        jnp.concatenate([sh_w_gate, sh_w_up], axis=-1)[:, None].astype(BF16),
        jnp.concatenate([moe_w_gate.astype(BF16), moe_w_up.astype(BF16)], axis=-1)], axis=1)
    w_dn = jnp.concatenate([sh_w_down[:, None].astype(BF16), moe_w_down.astype(BF16)], axis=1)

    layers = []
    for l in range(depth):
        layers.append({
            "g_attn": attn_norm_g[l].reshape(1, d), "g_ffn": ffn_norm_g[l].reshape(1, d),
            "w_in_a": w_in_a[l], "w_gate": w_gate[l], "w_ukv": w_ukv[l], "w_uq": w_uq[l],
            "g_kv": mla_kv_norm_g[l].reshape(1, MLA_KV_RANK), "g_cq": mla_q_norm_g[l].reshape(1, MLA_Q_RANK),
            "g_k": g_k[l], "g_kr": g_kr[l], "g_q": g_q[l], "g_kd": g_kd[l], "g_qd": g_qd[l],
            "g_sub": g_sub[l].reshape(1, DIFF_V), "b_k": b_k, "b_q": b_q, "b_d": b_d, "v_one": v_one,
            "w_o_mla": w_o_mla_p[l], "w_o_diff": w_o_diff[l].astype(BF16), "w_out": w_out[l].astype(BF16),
            "w_gu": w_gu[l], "w_dn": w_dn[l], "lam_init": lam_inits[l],
        })
    return layers


def kernel(x, c, ctx, c_ctx, ada_w, ada_b, attn_norm_g, ffn_norm_g, w_in, mla_q_norm_g, mla_w_uq, mla_kv_norm_g, mla_w_ukv, mla_q_g, mla_k_g, diff_q_g, diff_k_g, diff_lam, diff_sub_g, w_o_mla, w_o_diff, w_out, router_w, router_b, moe_w_gate, moe_w_up, moe_w_down, sh_w_gate, sh_w_up, sh_w_down):
    n_batch, seq, d = x.shape
    n_ctx = ctx.shape[1]
    depth = w_in.shape[0]
    assert d == D_MODEL and n_ctx == ROW_TILE and seq % KV_CHUNK == 0 and seq % GRID_W == 0
    assert (n_batch * seq) % MOE_TILE == 0 and (n_batch * n_ctx) % MOE_TILE == 0
    lat_rows = n_batch * seq

    layers = _prep_weights(attn_norm_g, ffn_norm_g, w_in, mla_q_norm_g, mla_w_uq, mla_kv_norm_g, mla_w_ukv,
                           mla_q_g, mla_k_g, diff_q_g, diff_k_g, diff_sub_g, w_o_mla, w_o_diff, w_out,
                           moe_w_gate, moe_w_up, moe_w_down, sh_w_gate, sh_w_up, sh_w_down)
    rope_tab = _rope_table(seq, ROW_TILE)
    rw = jnp.pad(router_w, ((0, 0), (1, LANES - 1 - N_EXPERTS)))
    rb = jnp.pad(router_b, (1, LANES - 1 - N_EXPERTS)).reshape(1, LANES)

    pad_rows = (-(n_batch + 1)) % 8
    c_all = jnp.concatenate([c, c_ctx[None, :], jnp.zeros((pad_rows, d), F32)], axis=0)
    mods_all = _mods_call(c_all, ada_w, ada_b).reshape(depth, c_all.shape[0], N_MOD, d)

    xf = jnp.concatenate([x.reshape(lat_rows, d), ctx.reshape(n_batch * n_ctx, d)], axis=0)
    for l in range(depth):
        wl = layers[l]
        mods = mods_all[l]
        k_m, v_m, q_m, k_d, v_d, q_d = _proj_call(xf, mods, wl, rope_tab, n_batch, seq)
        o_m = _attn_call(q_m, k_m, v_m, n_batch, seq, n_ctx, MLA_HEADS, 1, [], 0.0, "attn_mla")
        o_d = _attn_call(q_d, k_d, v_d, n_batch, seq, n_ctx, DIFF_HEADS, 2, [diff_lam[l], wl["g_sub"]],
                         wl["lam_init"], "attn_diff")
        xf, f, comb = _merge_call(xf, mods, o_m, o_d, wl, rw, rb, n_batch, seq)
        out_rows = lat_rows if l == depth - 1 else xf.shape[0]
        xf = _moe_call(f, comb, xf, mods, wl["w_gu"], wl["w_dn"], n_batch, seq, out_rows)
    return xf.reshape(n_batch, seq, d)
```

```python
import functools
import math

import jax
import jax.numpy as jnp
from jax import lax
from jax.experimental import pallas as pl
from jax.experimental.pallas import tpu as pltpu

F32 = jnp.float32
BF16 = jnp.bfloat16

D_MODEL = 1024
GRID_W = 64
EPS = 1e-6
ROPE_BASE = 10000.0
N_MOD = 6

MLA_HEADS = 8
MLA_NOPE = 64
MLA_ROPE = 32
MLA_V = 64
MLA_Q_RANK = 512
MLA_KV_RANK = 256
MLA_SCALE = 1.0 / math.sqrt(MLA_NOPE + MLA_ROPE)

DIFF_HEADS = 4
DIFF_DH = 64
DIFF_V = 2 * DIFF_DH
DIFF_SCALE = 1.0 / math.sqrt(DIFF_DH)

N_EXPERTS = 16
EXPERTS_PER_GROUP = 4
N_GROUPS = N_EXPERTS // EXPERTS_PER_GROUP
D_EXPERT = 512

LANES = 128
MXU_DIM = 256
HEAD_PAD = LANES
MLA_W = MLA_HEADS * HEAD_PAD
DIFF_QK_W = DIFF_HEADS * 2 * DIFF_DH
DIFF_V_PAD = 2 * LANES
DIFF_V_W = DIFF_HEADS * DIFF_V_PAD

A_CKV = 0
A_KR = A_CKV + MLA_KV_RANK
A_KD = A_KR + LANES
A_VD = A_KD + DIFF_QK_W
A_CQ = A_VD + DIFF_HEADS * DIFF_V
A_QD = A_CQ + MLA_Q_RANK
A_COLS = A_QD + DIFF_QK_W

ROPE_TAB_W = 6 * LANES

ROW_TILE = 256
MOE_TILE = 512
KV_CHUNK = 768
MLA_TQ = 512
DIFF_TQ = 256
LOG2E = math.log2(math.e)
VMEM_LIMIT = 56 * 1024 * 1024


def _sigmoid(x):
    return 1.0 / (1.0 + jnp.exp(-x))


def _dot(a, b):
    return jnp.dot(a, b, preferred_element_type=F32)


def _split_bf16(a):
    hi = a.astype(BF16)
    lo = (a - hi.astype(F32)).astype(BF16)
    return hi, lo


def _dot3(a, b):
    a_hi, a_lo = _split_bf16(a)
    b_hi, b_lo = _split_bf16(b)
    return _dot(a_hi, b_hi) + _dot(a_lo, b_hi) + _dot(a_hi, b_lo)


def _mods_kernel(c_ref, w_ref, b_ref, o_ref):
    c = c_ref[...]
    a = c * _sigmoid(c)
    o_ref[...] = _dot3(a, w_ref[...]) + b_ref[...]


def _mods_call(c_all, ada_w, ada_b):
    depth, d, n = ada_w.shape
    tn = 1536
    rows = c_all.shape[0]
    return pl.pallas_call(
        _mods_kernel,
        grid=(depth, n // tn),
        in_specs=[
            pl.BlockSpec((rows, d), lambda l, j: (0, 0)),
            pl.BlockSpec((None, d, tn), lambda l, j: (l, 0, j)),
            pl.BlockSpec((None, 1, tn), lambda l, j: (l, 0, j)),
        ],
        out_specs=pl.BlockSpec((None, rows, tn), lambda l, j: (l, 0, j)),
        out_shape=jax.ShapeDtypeStruct((depth, rows, n), F32),
        compiler_params=pltpu.CompilerParams(
            dimension_semantics=("parallel", "parallel"), vmem_limit_bytes=VMEM_LIMIT),
        name="adaln_mods",
    )(c_all, ada_w, ada_b.reshape(depth, 1, n))


def _rms(x, g):
    return x * lax.rsqrt(jnp.mean(x * x, axis=-1, keepdims=True) + EPS) * g


def _seg_mean_sq(x, b_ref):
    sq = (x * x).astype(BF16)
    b = b_ref[...]
    parts = [_dot(sq[:, c:c + MXU_DIM], b) for c in range(0, x.shape[1], MXU_DIM)]
    return parts[0] if len(parts) == 1 else jnp.concatenate(parts, axis=1)


def _rope(x, cos, sin_a, sin_b, half):
    return (x * cos + pltpu.roll(x, LANES - half, axis=1) * sin_a
            + pltpu.roll(x, half, axis=1) * sin_b)


def _proj_kernel(x_ref, mod_ref, g_ref, win_ref, wukv_ref, wuq_ref, gkv_ref, gcq_ref, gk_ref, gkr_ref,
                 gq_ref, gkd_ref, gqd_ref, bk_ref, bq_ref, bd_ref, vone_ref, rope_ref,
                 kmla_ref, vmla_ref, qmla_ref, kd_ref, vd_ref, qd_ref):
    x = x_ref[...]
    h = _rms(x, g_ref[...]) * (1.0 + mod_ref[1:2, :]) + mod_ref[0:1, :]
    p = _dot(h.astype(BF16), win_ref[...])

    cos_m = rope_ref[:, 0 * LANES:1 * LANES]
    sin_ma = rope_ref[:, 1 * LANES:2 * LANES]
    sin_mb = rope_ref[:, 2 * LANES:3 * LANES]
    cos_d = rope_ref[:, 3 * LANES:4 * LANES]
    sin_da = rope_ref[:, 4 * LANES:5 * LANES]
    sin_db = rope_ref[:, 5 * LANES:6 * LANES]

    ckv = _rms(p[:, A_CKV:A_KR], gkv_ref[...])
    kv = _dot(ckv.astype(BF16), wukv_ref[...])
    kn = kv[:, :MLA_W]
    kn = kn * lax.rsqrt(_seg_mean_sq(kn, bk_ref) + EPS) * gk_ref[...]
    kr = p[:, A_KR:A_KD]
    kr = kr * lax.rsqrt(jnp.sum(kr * kr, axis=-1, keepdims=True) * (1.0 / MLA_ROPE) + EPS) * gkr_ref[...]
    kr = _rope(kr, cos_m, sin_ma, sin_mb, MLA_ROPE // 4)
    for hd in range(MLA_HEADS):
        sl = slice(hd * HEAD_PAD, (hd + 1) * HEAD_PAD)
        kmla_ref[:, sl] = (kn[:, sl] + kr).astype(BF16)
    vmla_ref[...] = (kv[:, MLA_W:] + vone_ref[...]).astype(BF16)

    cq = _rms(p[:, A_CQ:A_QD], gcq_ref[...])
    q = _dot(cq.astype(BF16), wuq_ref[...])
    q = q * lax.rsqrt(_seg_mean_sq(q, bq_ref) + EPS) * gq_ref[...]
    for hd in range(MLA_HEADS):
        sl = slice(hd * HEAD_PAD, (hd + 1) * HEAD_PAD)
        qmla_ref[:, sl] = _rope(q[:, sl], cos_m, sin_ma, sin_mb, MLA_ROPE // 4).astype(BF16)

    kd = p[:, A_KD:A_VD]
    kd = kd * lax.rsqrt(_seg_mean_sq(kd, bd_ref) + EPS) * gkd_ref[...]
    qd = p[:, A_QD:A_COLS]
    qd = qd * lax.rsqrt(_seg_mean_sq(qd, bd_ref) + EPS) * gqd_ref[...]
    for hd in range(DIFF_HEADS):
        sl = slice(hd * LANES, (hd + 1) * LANES)
        kd_ref[:, sl] = _rope(kd[:, sl], cos_d, sin_da, sin_db, DIFF_DH // 4).astype(BF16)
        qd_ref[:, sl] = _rope(qd[:, sl], cos_d, sin_da, sin_db, DIFF_DH // 4).astype(BF16)

    vd = p[:, A_VD:A_CQ]
    tm = x.shape[0]
    ones_blk = (lax.broadcasted_iota(jnp.int32, (tm, LANES), 1) == 0).astype(BF16)
    for hd in range(DIFF_HEADS):
        vd_ref[:, hd * DIFF_V_PAD:hd * DIFF_V_PAD + DIFF_V] = vd[:, hd * DIFF_V:(hd + 1) * DIFF_V].astype(BF16)
        vd_ref[:, hd * DIFF_V_PAD + DIFF_V:(hd + 1) * DIFF_V_PAD] = ones_blk


def _row_class_maps(n_lat_tiles, tiles_per_batch, n_batch):
    def mod_map(i):
        return (jnp.where(i < n_lat_tiles, i // tiles_per_batch, n_batch), 0, 0)

    def rope_map(i):
        return (jnp.where(i < n_lat_tiles, i % tiles_per_batch, tiles_per_batch), 0)

    return mod_map, rope_map


def _const_spec(shape):
    return pl.BlockSpec(shape, lambda i: tuple(0 for _ in shape))


def _proj_call(x_flat, mods, wl, rope_tab, n_batch, seq):
    rows = x_flat.shape[0]
    tm = ROW_TILE
    n_lat_tiles = n_batch * seq // tm
    mod_map, rope_map = _row_class_maps(n_lat_tiles, seq // tm, n_batch)
    row_spec = lambda w: pl.BlockSpec((tm, w), lambda i: (i, 0))
    tiles_per_batch = seq // tm

    def kv_map(i):
        lat = i + i // tiles_per_batch + 1
        ctx = (i - n_lat_tiles) * (tiles_per_batch + 1)
        return (jnp.where(i < n_lat_tiles, lat, ctx), 0)

    kv_spec = lambda w: pl.BlockSpec((tm, w), kv_map)
    consts = [wl["g_attn"], wl["w_in_a"], wl["w_ukv"], wl["w_uq"], wl["g_kv"], wl["g_cq"], wl["g_k"], wl["g_kr"],
              wl["g_q"], wl["g_kd"], wl["g_qd"], wl["b_k"], wl["b_q"], wl["b_d"], wl["v_one"]]
    out_w = [MLA_W, MLA_W, MLA_W, DIFF_QK_W, DIFF_V_W, DIFF_QK_W]
    out_specs = [kv_spec(MLA_W), kv_spec(MLA_W), row_spec(MLA_W), kv_spec(DIFF_QK_W), kv_spec(DIFF_V_W),
                 row_spec(DIFF_QK_W)]
    return pl.pallas_call(
        _proj_kernel,
        grid=(rows // tm,),
        in_specs=[row_spec(D_MODEL), pl.BlockSpec((None, N_MOD, D_MODEL), mod_map)]
        + [_const_spec(c.shape) for c in consts]
        + [pl.BlockSpec((tm, ROPE_TAB_W), rope_map)],
        out_specs=out_specs,
        out_shape=[jax.ShapeDtypeStruct((rows, w), BF16) for w in out_w],
        compiler_params=pltpu.CompilerParams(
            dimension_semantics=("parallel",), vmem_limit_bytes=VMEM_LIMIT),
        name="proj_qkv",
    )(x_flat, mods, *consts, rope_tab)


def _attn_kernel(*refs, n_maps, tq, tk, n_chunks, lam_init, aliased_out):
    refs = list(refs)
    q_ref, k_ref, v_ref = refs[:3]
    n_in = 3 + (2 if n_maps == 2 else 0) + (1 if aliased_out else 0)
    o_ref = refs[n_in]
    scr = refs[n_in + 1:]
    if n_maps == 2:
        lam_ref, subg_ref = refs[3:5]
        qs_ref, s_ref, p_ref, m_ref, a_ref, acc_ref = scr
        q = q_ref[...]
        lane = lax.broadcasted_iota(jnp.int32, q.shape, 1)
        zero = jnp.zeros_like(q)
        qs_ref[0:tq, :] = jnp.where(lane < DIFF_DH, q, zero)
        qs_ref[tq:2 * tq, :] = jnp.where(lane >= DIFF_DH, q, zero)
        q_src = qs_ref
    else:
        s_ref, p_ref, m_ref, a_ref, acc_ref = scr
        q_src = q_ref
    vw = v_ref.shape[1]
    sum_col = vw // 2

    def keys_at(ref, j):
        return ref[pl.ds(pl.multiple_of(j * tk, tk), tk), :]

    def scores(j, slot):
        s_ref[slot] = lax.dot_general(q_src[...], keys_at(k_ref, j), (((1,), (1,)), ((), ())),
                                      preferred_element_type=F32)

    def softmax(slot, first):
        s = s_ref[slot]
        mx = jnp.max(s, axis=-1, keepdims=True)
        if first:
            m_new = mx
        else:
            m_old = m_ref[...]
            m_new = jnp.maximum(m_old, mx)
            a_ref[slot] = jnp.exp2(m_old - m_new)
        m_ref[...] = m_new
        p_ref[slot] = jnp.exp2(s - m_new).astype(BF16)

    def accumulate(j, slot, first):
        r = _dot(p_ref[slot], keys_at(v_ref, j))
        if first:
            acc_ref[...] = r
        else:
            acc_ref[...] = acc_ref[...] * a_ref[slot] + r

    def stage(j, slot, with_scores, first_acc=False):
        if with_scores:
            scores(j + 1, 1 - slot)
        accumulate(j - 1, 1 - slot, first_acc)
        softmax(slot, False)

    assert n_chunks % 2 == 1
    scores(0, 0)
    if n_chunks > 1:
        scores(1, 1)
    softmax(0, True)
    if n_chunks > 1:
        stage(1, 1, True, first_acc=True)

        def pair(i, carry):
            j = 2 + 2 * i
            stage(j, 0, True)
            stage(j + 1, 1, True)
            return carry

        lax.fori_loop(0, (n_chunks - 3) // 2, pair, 0)
        stage(n_chunks - 1, 0, False)
    accumulate(n_chunks - 1, 0, n_chunks == 1)

    acc = acc_ref[...]
    inv_l = 1.0 / acc[:, sum_col:sum_col + 1]
    if n_maps == 1:
        o_ref[...] = (acc * inv_l).astype(o_ref.dtype)
    else:
        lf = lam_ref[...]
        lam = (jnp.exp(jnp.sum(lf[0:1, :] * lf[1:2, :], axis=-1, keepdims=True))
               - jnp.exp(jnp.sum(lf[2:3, :] * lf[3:4, :], axis=-1, keepdims=True)) + lam_init)
        o = acc[:, :DIFF_V] * inv_l
        o = o[:tq] - lam * o[tq:]
        o_ref[...] = _rms(o, subg_ref[...]).astype(o_ref.dtype)


def _attn_scratch(n_maps, tq, tk, vw):
    r = n_maps * tq
    qs = [pltpu.VMEM((r, LANES), BF16)] if n_maps == 2 else []
    return qs + [pltpu.VMEM((2, r, tk), F32), pltpu.VMEM((2, r, tk), BF16), pltpu.VMEM((r, 1), F32),
                 pltpu.VMEM((2, r, 1), F32), pltpu.VMEM((r, vw), F32)]


def _attn_call(q, k, v, n_batch, seq, n_ctx, n_heads, n_maps, extra, lam_init, tq, with_ctx_queries, name):
    rows = q.shape[0]
    vw = v.shape[1] // n_heads
    keys = seq + n_ctx
    tk = KV_CHUNK
    nq = seq // tq
    assert keys % tk == 0 and seq % tq == 0
    ctx_blk0 = n_batch * seq // n_ctx
    blocks_per_batch = keys // n_ctx
    extra_specs = [pl.BlockSpec(e.shape, lambda *_: (0, 0)) for e in extra]
    out_shape = jax.ShapeDtypeStruct((rows, n_heads * LANES), BF16)

    kern = functools.partial(_attn_kernel, n_maps=n_maps, tq=tq, tk=tk, n_chunks=keys // tk, lam_init=lam_init,
                             aliased_out=False)
    out = pl.pallas_call(
        kern,
        grid=(n_batch, n_heads, nq),
        in_specs=[pl.BlockSpec((tq, LANES), lambda b, h, qi: (b * nq + qi, h)),
                  pl.BlockSpec((keys, LANES), lambda b, h, qi: (b, h)),
                  pl.BlockSpec((keys, vw), lambda b, h, qi: (b, h))] + extra_specs,
        out_specs=pl.BlockSpec((tq, LANES), lambda b, h, qi: (b * nq + qi, h)),
        out_shape=out_shape,
        scratch_shapes=_attn_scratch(n_maps, tq, tk, vw),
        compiler_params=pltpu.CompilerParams(
            dimension_semantics=("parallel", "parallel", "arbitrary"), vmem_limit_bytes=VMEM_LIMIT),
        name=name,
    )(q, k, v, *extra)
    if not with_ctx_queries:
        return out

    kern_c = functools.partial(_attn_kernel, n_maps=n_maps, tq=n_ctx, tk=n_ctx, n_chunks=1, lam_init=lam_init,
                               aliased_out=True)
    return pl.pallas_call(
        kern_c,
        grid=(n_batch, n_heads),
        in_specs=[pl.BlockSpec((n_ctx, LANES), lambda b, h: (ctx_blk0 + b, h)),
                  pl.BlockSpec((n_ctx, LANES), lambda b, h: (b * blocks_per_batch, h)),
                  pl.BlockSpec((n_ctx, vw), lambda b, h: (b * blocks_per_batch, h))] + extra_specs
        + [pl.BlockSpec(memory_space=pl.ANY)],
        out_specs=pl.BlockSpec((n_ctx, LANES), lambda b, h: (ctx_blk0 + b, h)),
        out_shape=out_shape,
        scratch_shapes=_attn_scratch(n_maps, n_ctx, n_ctx, vw),
        input_output_aliases={3 + len(extra): 0},
        compiler_params=pltpu.CompilerParams(
            dimension_semantics=("parallel", "parallel"), vmem_limit_bytes=VMEM_LIMIT),
        name=name + "_ctx",
    )(q, k, v, *extra, out)


def _route(logits, bias):
    lane = lax.broadcasted_iota(jnp.int32, logits.shape, 1)
    is_exp = (lane >= 1) & (lane <= N_EXPERTS)
    grp = (lane - 1) >> 2
    neg = jnp.float32(-jnp.inf)
    big = jnp.int32(1 << 20)
    aff = _sigmoid(logits)
    sel = jnp.where(is_exp, aff + bias, neg)

    def first_argmax(v):
        mx = jnp.max(v, axis=-1, keepdims=True)
        ix = jnp.min(jnp.where(v == mx, lane, big), axis=-1, keepdims=True)
        return mx, ix

    best = None
    for g in range(N_GROUPS):
        vg = jnp.where(grp == g, sel, neg)
        m1, i1 = first_argmax(vg)
        m2, i2 = first_argmax(jnp.where(lane == i1, neg, vg))
        score = m1 + m2
        if best is None:
            best = (score, i1, i2)
        else:
            upd = score > best[0]
            best = (jnp.where(upd, score, best[0]), jnp.where(upd, i1, best[1]), jnp.where(upd, i2, best[2]))
    _, i1, i2 = best
    a1 = jnp.sum(jnp.where(lane == i1, aff, 0.0), axis=-1, keepdims=True)
    a2 = jnp.sum(jnp.where(lane == i2, aff, 0.0), axis=-1, keepdims=True)
    tot = a1 + a2
    comb = jnp.where(lane == i1, a1 / tot, 0.0) + jnp.where(lane == i2, a2 / tot, 0.0)
    return jnp.where(lane == 0, 1.0, comb)


def _merge_kernel(x_ref, mod_ref, ga_ref, gf_ref, om_ref, od_ref, wgate_ref, wom_ref, wod_ref, wout_ref,
                  rw_ref, rb_ref, xo_ref, f_ref, comb_ref):
    x = x_ref[...]
    h = _rms(x, ga_ref[...]) * (1.0 + mod_ref[1:2, :]) + mod_ref[0:1, :]
    gates = _sigmoid(_dot(h.astype(BF16), wgate_ref[...]))
    y = (gates[:, :D_MODEL] * _dot(om_ref[...], wom_ref[...])
         + gates[:, D_MODEL:] * _dot(od_ref[...], wod_ref[...]))
    xn = x + mod_ref[2:3, :] * _dot(y.astype(BF16), wout_ref[...])
    xo_ref[...] = xn
    f = _rms(xn, gf_ref[...]) * (1.0 + mod_ref[4:5, :]) + mod_ref[3:4, :]
    f_ref[...] = f.astype(BF16)
    comb_ref[...] = _route(_dot3(f, rw_ref[...]), rb_ref[...])


def _merge_call(x_flat, mods, o_mla, o_diff, wl, router_w, router_b, n_batch, seq, rows):
    tm = ROW_TILE
    mod_map, _ = _row_class_maps(n_batch * seq // tm, seq // tm, n_batch)
    row_spec = lambda w: pl.BlockSpec((tm, w), lambda i: (i, 0))
    consts = [wl["w_gate"], wl["w_o_mla"], wl["w_o_diff"], wl["w_out"], router_w, router_b]
    return pl.pallas_call(
        _merge_kernel,
        grid=(rows // tm,),
        in_specs=[row_spec(D_MODEL), pl.BlockSpec((None, N_MOD, D_MODEL), mod_map),
                  _const_spec(wl["g_attn"].shape), _const_spec(wl["g_ffn"].shape),
                  row_spec(MLA_W), row_spec(DIFF_HEADS * LANES)]
        + [_const_spec(c.shape) for c in consts],
        out_specs=[row_spec(D_MODEL), row_spec(D_MODEL), row_spec(LANES)],
        out_shape=[jax.ShapeDtypeStruct((rows, D_MODEL), F32), jax.ShapeDtypeStruct((rows, D_MODEL), BF16),
                   jax.ShapeDtypeStruct((rows, LANES), F32)],
        compiler_params=pltpu.CompilerParams(
            dimension_semantics=("parallel",), vmem_limit_bytes=VMEM_LIMIT),
        name="merge_router",
    )(x_flat, mods, wl["g_attn"], wl["g_ffn"], o_mla, o_diff, *consts)


def _moe_kernel(f_ref, comb_ref, x_ref, mod_ref, wgu_ref, wd_ref, o_ref, acc_ref):
    e = pl.program_id(1)

    @pl.when(e == 0)
    def _():
        acc_ref[...] = jnp.zeros_like(acc_ref)

    gu = _dot(f_ref[...], wgu_ref[...])
    g = gu[:, :D_EXPERT]
    he = g * _sigmoid(g) * gu[:, D_EXPERT:]
    comb = comb_ref[...]
    lane = lax.broadcasted_iota(jnp.int32, comb.shape, 1)
    c = jnp.sum(jnp.where(lane == e, comb, 0.0), axis=-1, keepdims=True)
    acc_ref[...] += _dot((he * c).astype(BF16), wd_ref[...])

    @pl.when(e == pl.num_programs(1) - 1)
    def _():
        o_ref[...] = x_ref[...] + mod_ref[5:6, :] * acc_ref[...]


def _moe_call(f, comb, x_flat, mods, w_gu, w_down, n_batch, seq, out_rows):
    tm = MOE_TILE
    mod_map1, _ = _row_class_maps(n_batch * seq // tm, seq // tm, n_batch)
    n_slots = w_gu.shape[0]
    return pl.pallas_call(
        _moe_kernel,
        grid=(out_rows // tm, n_slots),
        in_specs=[
            pl.BlockSpec((tm, D_MODEL), lambda i, e: (i, 0)),
            pl.BlockSpec((tm, LANES), lambda i, e: (i, 0)),
            pl.BlockSpec((tm, D_MODEL), lambda i, e: (i, 0)),
            pl.BlockSpec((None, N_MOD, D_MODEL), lambda i, e: mod_map1(i)),
            pl.BlockSpec((None, D_MODEL, 2 * D_EXPERT), lambda i, e: (e, 0, 0)),
            pl.BlockSpec((None, D_EXPERT, D_MODEL), lambda i, e: (e, 0, 0)),
        ],
        out_specs=pl.BlockSpec((tm, D_MODEL), lambda i, e: (i, 0)),
        out_shape=jax.ShapeDtypeStruct((out_rows, D_MODEL), F32),
        scratch_shapes=[pltpu.VMEM((tm, D_MODEL), F32)],
        compiler_params=pltpu.CompilerParams(
            dimension_semantics=("parallel", "arbitrary"), vmem_limit_bytes=VMEM_LIMIT),
        name="moe_ffn",
    )(f, comb, x_flat, mods, w_gu, w_down)


def _axial_angles(seq, d_rope):
    d_axis = d_rope // 2
    inv = ROPE_BASE ** (-jnp.arange(0, d_axis, 2, dtype=F32) / d_axis)
    t = jnp.arange(seq, dtype=jnp.int32)
    r = (t // GRID_W).astype(F32)
    col = (t % GRID_W).astype(F32)
    ang_r = r[:, None] * inv
    ang_c = col[:, None] * inv
    cos = jnp.concatenate([jnp.cos(ang_r)] * 2 + [jnp.cos(ang_c)] * 2, axis=1)
    sin = jnp.concatenate([jnp.sin(ang_r)] * 2 + [jnp.sin(ang_c)] * 2, axis=1)
    nf = d_rope // 4
    first = jnp.concatenate([jnp.ones((nf,), F32), jnp.zeros((nf,), F32)] * 2)
    return cos, -sin * first, sin * (1.0 - first)


def _rope_table(seq, tile):
    cm, sma, smb = _axial_angles(seq, MLA_ROPE)
    pad_l = jnp.zeros((seq, MLA_NOPE), F32)
    pad_r = jnp.zeros((seq, HEAD_PAD - MLA_NOPE - MLA_ROPE), F32)
    cos_m = jnp.concatenate([pad_l + 1.0, cm, pad_r + 1.0], axis=1)
    sin_ma = jnp.concatenate([pad_l, sma, pad_r], axis=1)
    sin_mb = jnp.concatenate([pad_l, smb, pad_r], axis=1)
    cd, sda, sdb = _axial_angles(seq, DIFF_DH)
    tab = jnp.concatenate([cos_m, sin_ma, sin_mb] + [jnp.concatenate([a, a], axis=1) for a in (cd, sda, sdb)],
                          axis=1)
    ident = jnp.concatenate([jnp.ones((tile, LANES), F32), jnp.zeros((tile, 2 * LANES), F32)] * 2, axis=1)
    return jnp.concatenate([tab, ident], axis=0)


def _block_diag_mean(seg_sizes):
    m = jnp.zeros((MXU_DIM, MXU_DIM), F32)
    off = 0
    for size, active in seg_sizes:
        if active:
            m = m.at[off:off + size, off:off + size].set(1.0 / size)
        off += size
    assert off == MXU_DIM
    return m.astype(BF16)


def _prep_weights(attn_norm_g, ffn_norm_g, w_in, mla_q_norm_g, mla_w_uq, mla_kv_norm_g, mla_w_ukv, mla_q_g,
                  mla_k_g, diff_q_g, diff_k_g, diff_sub_g, w_o_mla, w_o_diff, w_out, moe_w_gate, moe_w_up,
                  moe_w_down, sh_w_gate, sh_w_up, sh_w_down):
    depth = w_in.shape[0]
    d = D_MODEL
    z = lambda w: jnp.zeros((depth, d, w), F32)
    off_kr = MLA_KV_RANK
    off_kd = off_kr + MLA_ROPE
    off_vd = off_kd + DIFF_QK_W
    off_cq = off_vd + DIFF_HEADS * DIFF_V
    off_qd = off_cq + MLA_Q_RANK
    off_gate = off_qd + DIFF_QK_W
    w_in_a = jnp.concatenate([
        w_in[:, :, :off_kr], z(MLA_NOPE), w_in[:, :, off_kr:off_kd], z(HEAD_PAD - MLA_NOPE - MLA_ROPE),
        w_in[:, :, off_kd:off_gate]], axis=2).astype(BF16)
    w_gate = w_in[:, :, off_gate:].astype(BF16)

    ukv = mla_w_ukv.reshape(depth, MLA_KV_RANK, MLA_HEADS, MLA_NOPE + MLA_V)
    pad_k = jnp.zeros((depth, MLA_KV_RANK, MLA_HEADS, HEAD_PAD - MLA_NOPE), F32)
    pad_v = jnp.zeros((depth, MLA_KV_RANK, MLA_HEADS, HEAD_PAD - MLA_V), F32)
    w_ukv = jnp.concatenate([
        jnp.concatenate([ukv[..., :MLA_NOPE], pad_k], axis=-1).reshape(depth, MLA_KV_RANK, MLA_W),
        jnp.concatenate([ukv[..., MLA_NOPE:], pad_v], axis=-1).reshape(depth, MLA_KV_RANK, MLA_W)],
        axis=-1).astype(BF16)
    uq = mla_w_uq.reshape(depth, MLA_Q_RANK, MLA_HEADS, MLA_NOPE + MLA_ROPE)
    w_uq = jnp.pad(uq, ((0, 0), (0, 0), (0, 0), (0, HEAD_PAD - MLA_NOPE - MLA_ROPE))).reshape(
        depth, MLA_Q_RANK, MLA_W).astype(BF16)
    wom = w_o_mla.reshape(depth, MLA_HEADS, MLA_V, d)
    w_o_mla_p = jnp.pad(wom, ((0, 0), (0, 0), (0, HEAD_PAD - MLA_V), (0, 0))).reshape(depth, MLA_W, d).astype(BF16)

    def per_head(g96, scale):
        g = jnp.pad(g96 * scale, ((0, 0), (0, HEAD_PAD - MLA_NOPE - MLA_ROPE)))
        return jnp.tile(g, (1, MLA_HEADS)).reshape(depth, 1, MLA_W)

    k_nope_only = jnp.concatenate([mla_k_g[:, :MLA_NOPE], jnp.zeros((depth, MLA_ROPE), F32)], axis=1)
    g_k = per_head(k_nope_only, 1.0)
    g_kr = jnp.concatenate([jnp.zeros((depth, MLA_NOPE), F32), mla_k_g[:, MLA_NOPE:],
                            jnp.zeros((depth, HEAD_PAD - MLA_NOPE - MLA_ROPE), F32)], axis=1).reshape(depth, 1, LANES)
    g_q = per_head(mla_q_g, MLA_SCALE * LOG2E)
    g_kd = jnp.tile(diff_k_g.reshape(depth, 2 * DIFF_DH), (1, DIFF_HEADS)).reshape(depth, 1, DIFF_QK_W)
    g_qd = jnp.tile(diff_q_g.reshape(depth, 2 * DIFF_DH) * (DIFF_SCALE * LOG2E),
                    (1, DIFF_HEADS)).reshape(depth, 1, DIFF_QK_W)
    lam_inits = [0.8 - 0.6 * math.exp(-0.3 * l) for l in range(depth)]
    g_sub = diff_sub_g * (1.0 - jnp.asarray(lam_inits, F32))[:, None]

    v_one = jnp.tile((jnp.arange(HEAD_PAD) == MLA_V).astype(F32), MLA_HEADS).reshape(1, MLA_W)
    b_k = _block_diag_mean([(MLA_NOPE, True), (HEAD_PAD - MLA_NOPE, False)] * 2)
    b_q = _block_diag_mean([(MLA_NOPE, True), (MLA_ROPE, True), (HEAD_PAD - MLA_NOPE - MLA_ROPE, False)] * 2)
    b_d = _block_diag_mean([(DIFF_DH, True)] * 4)

    w_gu = jnp.concatenate([
        jnp.concatenate([sh_w_gate, sh_w_up], axis=-1)[:, None].astype(BF16),
        jnp.concatenate([moe_w_gate.astype(BF16), moe_w_up.astype(BF16)], axis=-1)], axis=1)
    w_dn = jnp.concatenate([sh_w_down[:, None].astype(BF16), moe_w_down.astype(BF16)], axis=1)

    layers = []
    for l in range(depth):
        layers.append({
            "g_attn": attn_norm_g[l].reshape(1, d), "g_ffn": ffn_norm_g[l].reshape(1, d),
            "w_in_a": w_in_a[l], "w_gate": w_gate[l], "w_ukv": w_ukv[l], "w_uq": w_uq[l],
            "g_kv": mla_kv_norm_g[l].reshape(1, MLA_KV_RANK), "g_cq": mla_q_norm_g[l].reshape(1, MLA_Q_RANK),
            "g_k": g_k[l], "g_kr": g_kr[l], "g_q": g_q[l], "g_kd": g_kd[l], "g_qd": g_qd[l],
            "g_sub": g_sub[l].reshape(1, DIFF_V), "b_k": b_k, "b_q": b_q, "b_d": b_d, "v_one": v_one,
            "w_o_mla": w_o_mla_p[l], "w_o_diff": w_o_diff[l].astype(BF16), "w_out": w_out[l].astype(BF16),
            "w_gu": w_gu[l], "w_dn": w_dn[l], "lam_init": lam_inits[l],
        })
    return layers


def kernel(x, c, ctx, c_ctx, ada_w, ada_b, attn_norm_g, ffn_norm_g, w_in, mla_q_norm_g, mla_w_uq, mla_kv_norm_g, mla_w_ukv, mla_q_g, mla_k_g, diff_q_g, diff_k_g, diff_lam, diff_sub_g, w_o_mla, w_o_diff, w_out, router_w, router_b, moe_w_gate, moe_w_up, moe_w_down, sh_w_gate, sh_w_up, sh_w_down):
    n_batch, seq, d = x.shape
    n_ctx = ctx.shape[1]
    depth = w_in.shape[0]
    assert d == D_MODEL and n_ctx == ROW_TILE and seq % GRID_W == 0
    assert (seq + n_ctx) % KV_CHUNK == 0 and seq % MLA_TQ == 0 and seq % DIFF_TQ == 0
    assert (n_batch * seq) % MOE_TILE == 0 and (n_batch * n_ctx) % MOE_TILE == 0
    lat_rows = n_batch * seq
    all_rows = lat_rows + n_batch * n_ctx

    layers = _prep_weights(attn_norm_g, ffn_norm_g, w_in, mla_q_norm_g, mla_w_uq, mla_kv_norm_g, mla_w_ukv,
                           mla_q_g, mla_k_g, diff_q_g, diff_k_g, diff_sub_g, w_o_mla, w_o_diff, w_out,
                           moe_w_gate, moe_w_up, moe_w_down, sh_w_gate, sh_w_up, sh_w_down)
    rope_tab = _rope_table(seq, ROW_TILE)
    rw = jnp.pad(router_w, ((0, 0), (1, LANES - 1 - N_EXPERTS)))
    rb = jnp.pad(router_b, (1, LANES - 1 - N_EXPERTS)).reshape(1, LANES)

    pad_rows = (-(n_batch + 1)) % 8
    c_all = jnp.concatenate([c, c_ctx[None, :], jnp.zeros((pad_rows, d), F32)], axis=0)
    mods_all = _mods_call(c_all, ada_w, ada_b).reshape(depth, c_all.shape[0], N_MOD, d)

    xf = jnp.concatenate([x.reshape(lat_rows, d), ctx.reshape(n_batch * n_ctx, d)], axis=0)
    for l in range(depth):
        wl = layers[l]
        mods = mods_all[l]
        k_m, v_m, q_m, k_d, v_d, q_d = _proj_call(xf, mods, wl, rope_tab, n_batch, seq)
        last = l == depth - 1
        o_m = _attn_call(q_m, k_m, v_m, n_batch, seq, n_ctx, MLA_HEADS, 1, [], 0.0, MLA_TQ, not last, "attn_mla")
        o_d = _attn_call(q_d, k_d, v_d, n_batch, seq, n_ctx, DIFF_HEADS, 2, [diff_lam[l], wl["g_sub"]],
                         wl["lam_init"], DIFF_TQ, not last, "attn_diff")
        out_rows = lat_rows if last else all_rows
        xf, f, comb = _merge_call(xf, mods, o_m, o_d, wl, rw, rb, n_batch, seq, out_rows)
        xf = _moe_call(f, comb, xf, mods, wl["w_gu"], wl["w_dn"], n_batch, seq, out_rows)
    return xf.reshape(n_batch, seq, d)
```

```python
import functools
import math

import jax
import jax.numpy as jnp
from jax import lax
from jax.experimental import pallas as pl
from jax.experimental.pallas import tpu as pltpu

F32 = jnp.float32
BF16 = jnp.bfloat16

D_MODEL = 1024
GRID_W = 64
EPS = 1e-6
ROPE_BASE = 10000.0
N_MOD = 6

MLA_HEADS = 8
MLA_NOPE = 64
MLA_ROPE = 32
MLA_V = 64
MLA_Q_RANK = 512
MLA_KV_RANK = 256
MLA_SCALE = 1.0 / math.sqrt(MLA_NOPE + MLA_ROPE)

DIFF_HEADS = 4
DIFF_DH = 64
DIFF_V = 2 * DIFF_DH
DIFF_SCALE = 1.0 / math.sqrt(DIFF_DH)

N_EXPERTS = 16
EXPERTS_PER_GROUP = 4
N_GROUPS = N_EXPERTS // EXPERTS_PER_GROUP
D_EXPERT = 512

LANES = 128
MXU_DIM = 256
HEAD_PAD = LANES
MLA_W = MLA_HEADS * HEAD_PAD
DIFF_QK_W = DIFF_HEADS * 2 * DIFF_DH
DIFF_V_PAD = 2 * LANES
DIFF_V_W = DIFF_HEADS * DIFF_V_PAD

A_CKV = 0
A_KR = A_CKV + MLA_KV_RANK
A_KD = A_KR + LANES
A_VD = A_KD + DIFF_QK_W
A_CQ = A_VD + DIFF_HEADS * DIFF_V
A_QD = A_CQ + MLA_Q_RANK
A_COLS = A_QD + DIFF_QK_W

ROPE_TAB_W = 6 * LANES

ROW_TILE = 256
MOE_TILE = 512
SORT_TILE = 256
KV_CHUNK = 768
MLA_TQ = 512
DIFF_TQ = 256
LOOP_STAGES = 4
LOG2E = math.log2(math.e)
VMEM_LIMIT = 56 * 1024 * 1024


def _sigmoid(x):
    return 1.0 / (1.0 + jnp.exp(-x))


def _dot(a, b):
    return jnp.dot(a, b, preferred_element_type=F32)


def _split_bf16(a):
    hi = a.astype(BF16)
    lo = (a - hi.astype(F32)).astype(BF16)
    return hi, lo


def _dot3(a, b):
    a_hi, a_lo = _split_bf16(a)
    b_hi, b_lo = _split_bf16(b)
    return _dot(a_hi, b_hi) + _dot(a_lo, b_hi) + _dot(a_hi, b_lo)


def _mods_kernel(c_ref, w_ref, b_ref, o_ref):
    c = c_ref[...]
    a = c * _sigmoid(c)
    o_ref[...] = _dot3(a, w_ref[...]) + b_ref[...]


def _mods_call(c_all, ada_w, ada_b):
    depth, d, n = ada_w.shape
    tn = 1536
    rows = c_all.shape[0]
    return pl.pallas_call(
        _mods_kernel,
        grid=(depth, n // tn),
        in_specs=[
            pl.BlockSpec((rows, d), lambda l, j: (0, 0)),
            pl.BlockSpec((None, d, tn), lambda l, j: (l, 0, j)),
            pl.BlockSpec((None, 1, tn), lambda l, j: (l, 0, j)),
        ],
        out_specs=pl.BlockSpec((None, rows, tn), lambda l, j: (l, 0, j)),
        out_shape=jax.ShapeDtypeStruct((depth, rows, n), F32),
        compiler_params=pltpu.CompilerParams(
            dimension_semantics=("parallel", "parallel"), vmem_limit_bytes=VMEM_LIMIT),
        name="adaln_mods",
    )(c_all, ada_w, ada_b.reshape(depth, 1, n))


def _rms(x, g):
    return x * lax.rsqrt(jnp.mean(x * x, axis=-1, keepdims=True) + EPS) * g


def _seg_mean_sq(x, b_ref):
    sq = (x * x).astype(BF16)
    b = b_ref[...]
    parts = [_dot(sq[:, c:c + MXU_DIM], b) for c in range(0, x.shape[1], MXU_DIM)]
    return parts[0] if len(parts) == 1 else jnp.concatenate(parts, axis=1)


def _rope(x, cos, sin_a, sin_b, half):
    return (x * cos + pltpu.roll(x, LANES - half, axis=1) * sin_a
            + pltpu.roll(x, half, axis=1) * sin_b)


def _proj_kernel(x_ref, mod_ref, g_ref, win_ref, wukv_ref, wuq_ref, gkv_ref, gcq_ref, gk_ref, gkr_ref,
                 gq_ref, gkd_ref, gqd_ref, bk_ref, bq_ref, bd_ref, vone_ref, rope_ref,
                 kmla_ref, vmla_ref, qmla_ref, kd_ref, vd_ref, qd_ref):
    x = x_ref[...]
    h = _rms(x, g_ref[...]) * (1.0 + mod_ref[1:2, :]) + mod_ref[0:1, :]
    p = _dot(h.astype(BF16), win_ref[...])

    cos_m = rope_ref[:, 0 * LANES:1 * LANES]
    sin_ma = rope_ref[:, 1 * LANES:2 * LANES]
    sin_mb = rope_ref[:, 2 * LANES:3 * LANES]
    cos_d = rope_ref[:, 3 * LANES:4 * LANES]
    sin_da = rope_ref[:, 4 * LANES:5 * LANES]
    sin_db = rope_ref[:, 5 * LANES:6 * LANES]

    ckv = _rms(p[:, A_CKV:A_KR], gkv_ref[...])
    kv = _dot(ckv.astype(BF16), wukv_ref[...])
    kn = kv[:, :MLA_W]
    kn = kn * lax.rsqrt(_seg_mean_sq(kn, bk_ref) + EPS) * gk_ref[...]
    kr = p[:, A_KR:A_KD]
    kr = kr * lax.rsqrt(jnp.sum(kr * kr, axis=-1, keepdims=True) * (1.0 / MLA_ROPE) + EPS) * gkr_ref[...]
    kr = _rope(kr, cos_m, sin_ma, sin_mb, MLA_ROPE // 4)
    for hd in range(MLA_HEADS):
        sl = slice(hd * HEAD_PAD, (hd + 1) * HEAD_PAD)
        kmla_ref[:, sl] = (kn[:, sl] + kr).astype(BF16)
    vmla_ref[...] = (kv[:, MLA_W:] + vone_ref[...]).astype(BF16)

    cq = _rms(p[:, A_CQ:A_QD], gcq_ref[...])
    q = _dot(cq.astype(BF16), wuq_ref[...])
    q = q * lax.rsqrt(_seg_mean_sq(q, bq_ref) + EPS) * gq_ref[...]
    for hd in range(MLA_HEADS):
        sl = slice(hd * HEAD_PAD, (hd + 1) * HEAD_PAD)
        qmla_ref[:, sl] = _rope(q[:, sl], cos_m, sin_ma, sin_mb, MLA_ROPE // 4).astype(BF16)

    kd = p[:, A_KD:A_VD]
    kd = kd * lax.rsqrt(_seg_mean_sq(kd, bd_ref) + EPS) * gkd_ref[...]
    qd = p[:, A_QD:A_COLS]
    qd = qd * lax.rsqrt(_seg_mean_sq(qd, bd_ref) + EPS) * gqd_ref[...]
    for hd in range(DIFF_HEADS):
        sl = slice(hd * LANES, (hd + 1) * LANES)
        kd_ref[:, sl] = _rope(kd[:, sl], cos_d, sin_da, sin_db, DIFF_DH // 4).astype(BF16)
        qd_ref[:, sl] = _rope(qd[:, sl], cos_d, sin_da, sin_db, DIFF_DH // 4).astype(BF16)

    vd = p[:, A_VD:A_CQ]
    tm = x.shape[0]
    ones_blk = (lax.broadcasted_iota(jnp.int32, (tm, LANES), 1) == 0).astype(BF16)
    for hd in range(DIFF_HEADS):
        vd_ref[:, hd * DIFF_V_PAD:hd * DIFF_V_PAD + DIFF_V] = vd[:, hd * DIFF_V:(hd + 1) * DIFF_V].astype(BF16)
        vd_ref[:, hd * DIFF_V_PAD + DIFF_V:(hd + 1) * DIFF_V_PAD] = ones_blk


def _row_class_maps(n_lat_tiles, tiles_per_batch, n_batch):
    def mod_map(i):
        return (jnp.where(i < n_lat_tiles, i // tiles_per_batch, n_batch), 0, 0)

    def rope_map(i):
        return (jnp.where(i < n_lat_tiles, i % tiles_per_batch, tiles_per_batch), 0)

    return mod_map, rope_map


def _const_spec(shape):
    return pl.BlockSpec(shape, lambda i: tuple(0 for _ in shape))


def _proj_call(x_flat, mods, wl, rope_tab, n_batch, seq):
    rows = x_flat.shape[0]
    tm = ROW_TILE
    n_lat_tiles = n_batch * seq // tm
    mod_map, rope_map = _row_class_maps(n_lat_tiles, seq // tm, n_batch)
    row_spec = lambda w: pl.BlockSpec((tm, w), lambda i: (i, 0))
    tiles_per_batch = seq // tm

    def kv_map(i):
        lat = i + i // tiles_per_batch + 1
        ctx = (i - n_lat_tiles) * (tiles_per_batch + 1)
        return (jnp.where(i < n_lat_tiles, lat, ctx), 0)

    kv_spec = lambda w: pl.BlockSpec((tm, w), kv_map)
    consts = [wl["g_attn"], wl["w_in_a"], wl["w_ukv"], wl["w_uq"], wl["g_kv"], wl["g_cq"], wl["g_k"], wl["g_kr"],
              wl["g_q"], wl["g_kd"], wl["g_qd"], wl["b_k"], wl["b_q"], wl["b_d"], wl["v_one"]]
    out_w = [MLA_W, MLA_W, MLA_W, DIFF_QK_W, DIFF_V_W, DIFF_QK_W]
    out_specs = [kv_spec(MLA_W), kv_spec(MLA_W), row_spec(MLA_W), kv_spec(DIFF_QK_W), kv_spec(DIFF_V_W),
                 row_spec(DIFF_QK_W)]
    return pl.pallas_call(
        _proj_kernel,
        grid=(rows // tm,),
        in_specs=[row_spec(D_MODEL), pl.BlockSpec((None, N_MOD, D_MODEL), mod_map)]
        + [_const_spec(c.shape) for c in consts]
        + [pl.BlockSpec((tm, ROPE_TAB_W), rope_map)],
        out_specs=out_specs,
        out_shape=[jax.ShapeDtypeStruct((rows, w), BF16) for w in out_w],
        compiler_params=pltpu.CompilerParams(
            dimension_semantics=("parallel",), vmem_limit_bytes=VMEM_LIMIT),
        name="proj_qkv",
    )(x_flat, mods, *consts, rope_tab)


def _attn_kernel(*refs, n_maps, tq, tk, n_chunks, lam_init, aliased_out):
    refs = list(refs)
    q_ref, k_ref, v_ref = refs[:3]
    n_in = 3 + (2 if n_maps == 2 else 0) + (1 if aliased_out else 0)
    o_ref = refs[n_in]
    scr = refs[n_in + 1:]
    if n_maps == 2:
        lam_ref, subg_ref = refs[3:5]
        qs_ref, s_ref, p_ref, m_ref, a_ref, acc_ref = scr
        q = q_ref[...]
        lane = lax.broadcasted_iota(jnp.int32, q.shape, 1)
        zero = jnp.zeros_like(q)
        qs_ref[0:tq, :] = jnp.where(lane < DIFF_DH, q, zero)
        qs_ref[tq:2 * tq, :] = jnp.where(lane >= DIFF_DH, q, zero)
        q_src = qs_ref
    else:
        s_ref, p_ref, m_ref, a_ref, acc_ref = scr
        q_src = q_ref
    vw = v_ref.shape[1]
    sum_col = vw // 2

    def keys_at(ref, j):
        return ref[pl.ds(pl.multiple_of(j * tk, tk), tk), :]

    def scores(j, slot):
        s_ref[slot] = lax.dot_general(q_src[...], keys_at(k_ref, j), (((1,), (1,)), ((), ())),
                                      preferred_element_type=F32)

    def softmax(slot, first):
        s = s_ref[slot]
        mx = jnp.max(s, axis=-1, keepdims=True)
        if first:
            m_new = mx
        else:
            m_old = m_ref[...]
            m_new = jnp.maximum(m_old, mx)
            a_ref[slot] = jnp.exp2(m_old - m_new)
        m_ref[...] = m_new
        p_ref[slot] = jnp.exp2(s - m_new).astype(BF16)

    def accumulate(j, slot, first):
        r = _dot(p_ref[slot], keys_at(v_ref, j))
        if first:
            acc_ref[...] = r
        else:
            acc_ref[...] = acc_ref[...] * a_ref[slot] + r

    def stage(j, slot, with_scores, first_acc=False):
        if with_scores:
            scores(j + 1, 1 - slot)
        accumulate(j - 1, 1 - slot, first_acc)
        softmax(slot, False)

    assert n_chunks % 2 == 1
    scores(0, 0)
    if n_chunks > 1:
        scores(1, 1)
    softmax(0, True)
    if n_chunks > 1:
        stage(1, 1, True, first_acc=True)

        n_loop = (n_chunks - 3) // LOOP_STAGES

        def group(i, carry):
            j0 = 2 + LOOP_STAGES * i
            for t in range(LOOP_STAGES):
                stage(j0 + t, t % 2, True)
            return carry

        lax.fori_loop(0, n_loop, group, 0)
        for j in range(2 + LOOP_STAGES * n_loop, n_chunks - 1):
            stage(j, j % 2, True)
        stage(n_chunks - 1, 0, False)
    accumulate(n_chunks - 1, 0, n_chunks == 1)

    acc = acc_ref[...]
    inv_l = 1.0 / acc[:, sum_col:sum_col + 1]
    if n_maps == 1:
        o_ref[...] = (acc * inv_l).astype(o_ref.dtype)
    else:
        lf = lam_ref[...]
        lam = (jnp.exp(jnp.sum(lf[0:1, :] * lf[1:2, :], axis=-1, keepdims=True))
               - jnp.exp(jnp.sum(lf[2:3, :] * lf[3:4, :], axis=-1, keepdims=True)) + lam_init)
        o = acc[:, :DIFF_V] * inv_l
        o = o[:tq] - lam * o[tq:]
        o_ref[...] = _rms(o, subg_ref[...]).astype(o_ref.dtype)


def _attn_scratch(n_maps, tq, tk, vw):
    r = n_maps * tq
    qs = [pltpu.VMEM((r, LANES), BF16)] if n_maps == 2 else []
    return qs + [pltpu.VMEM((2, r, tk), F32), pltpu.VMEM((2, r, tk), BF16), pltpu.VMEM((r, 1), F32),
                 pltpu.VMEM((2, r, 1), F32), pltpu.VMEM((r, vw), F32)]


def _attn_call(q, k, v, n_batch, seq, n_ctx, n_heads, n_maps, extra, lam_init, tq, with_ctx_queries, name):
    rows = q.shape[0]
    vw = v.shape[1] // n_heads
    keys = seq + n_ctx
    tk = KV_CHUNK
    nq = seq // tq
    assert keys % tk == 0 and seq % tq == 0
    ctx_blk0 = n_batch * seq // n_ctx
    blocks_per_batch = keys // n_ctx
    extra_specs = [pl.BlockSpec(e.shape, lambda *_: (0, 0)) for e in extra]
    out_shape = jax.ShapeDtypeStruct((rows, n_heads * LANES), BF16)

    kern = functools.partial(_attn_kernel, n_maps=n_maps, tq=tq, tk=tk, n_chunks=keys // tk, lam_init=lam_init,
                             aliased_out=False)
    out = pl.pallas_call(
        kern,
        grid=(n_batch, n_heads, nq),
        in_specs=[pl.BlockSpec((tq, LANES), lambda b, h, qi: (b * nq + qi, h)),
                  pl.BlockSpec((keys, LANES), lambda b, h, qi: (b, h)),
                  pl.BlockSpec((keys, vw), lambda b, h, qi: (b, h))] + extra_specs,
        out_specs=pl.BlockSpec((tq, LANES), lambda b, h, qi: (b * nq + qi, h)),
        out_shape=out_shape,
        scratch_shapes=_attn_scratch(n_maps, tq, tk, vw),
        compiler_params=pltpu.CompilerParams(
            dimension_semantics=("parallel", "parallel", "arbitrary"), vmem_limit_bytes=VMEM_LIMIT),
        name=name,
    )(q, k, v, *extra)
    if not with_ctx_queries:
        return out

    kern_c = functools.partial(_attn_kernel, n_maps=n_maps, tq=n_ctx, tk=n_ctx, n_chunks=1, lam_init=lam_init,
                               aliased_out=True)
    return pl.pallas_call(
        kern_c,
        grid=(n_batch, n_heads),
        in_specs=[pl.BlockSpec((n_ctx, LANES), lambda b, h: (ctx_blk0 + b, h)),
                  pl.BlockSpec((n_ctx, LANES), lambda b, h: (b * blocks_per_batch, h)),
                  pl.BlockSpec((n_ctx, vw), lambda b, h: (b * blocks_per_batch, h))] + extra_specs
        + [pl.BlockSpec(memory_space=pl.ANY)],
        out_specs=pl.BlockSpec((n_ctx, LANES), lambda b, h: (ctx_blk0 + b, h)),
        out_shape=out_shape,
        scratch_shapes=_attn_scratch(n_maps, n_ctx, n_ctx, vw),
        input_output_aliases={3 + len(extra): 0},
        compiler_params=pltpu.CompilerParams(
            dimension_semantics=("parallel", "parallel"), vmem_limit_bytes=VMEM_LIMIT),
        name=name + "_ctx",
    )(q, k, v, *extra, out)


def _route(logits, bias):
    lane = lax.broadcasted_iota(jnp.int32, logits.shape, 1)
    is_exp = (lane >= 1) & (lane <= N_EXPERTS)
    grp = (lane - 1) >> 2
    neg = jnp.float32(-jnp.inf)
    big = jnp.int32(1 << 20)
    aff = _sigmoid(logits)
    sel = jnp.where(is_exp, aff + bias, neg)

    def first_argmax(v):
        mx = jnp.max(v, axis=-1, keepdims=True)
        ix = jnp.min(jnp.where(v == mx, lane, big), axis=-1, keepdims=True)
        return mx, ix

    best = None
    for g in range(N_GROUPS):
        vg = jnp.where(grp == g, sel, neg)
        m1, i1 = first_argmax(vg)
        m2, i2 = first_argmax(jnp.where(lane == i1, neg, vg))
        score = m1 + m2
        if best is None:
            best = (score, i1, i2)
        else:
            upd = score > best[0]
            best = (jnp.where(upd, score, best[0]), jnp.where(upd, i1, best[1]), jnp.where(upd, i2, best[2]))
    _, i1, i2 = best
    a1 = jnp.sum(jnp.where(lane == i1, aff, 0.0), axis=-1, keepdims=True)
    a2 = jnp.sum(jnp.where(lane == i2, aff, 0.0), axis=-1, keepdims=True)
    tot = a1 + a2
    rec = jnp.where(lane == 0, (i1 - 1).astype(F32), 0.0) + jnp.where(lane == 1, (i2 - 1).astype(F32), 0.0)
    return rec + jnp.where(lane == 2, a1 / tot, 0.0) + jnp.where(lane == 3, a2 / tot, 0.0)


def _merge_kernel(x_ref, mod_ref, ga_ref, gf_ref, om_ref, od_ref, wgate_ref, wom_ref, wod_ref, wout_ref,
                  rw_ref, rb_ref, xo_ref, f_ref, route_ref):
    x = x_ref[...]
    h = _rms(x, ga_ref[...]) * (1.0 + mod_ref[1:2, :]) + mod_ref[0:1, :]
    gates = _sigmoid(_dot(h.astype(BF16), wgate_ref[...]))
    y = (gates[:, :D_MODEL] * _dot(om_ref[...], wom_ref[...])
         + gates[:, D_MODEL:] * _dot(od_ref[...], wod_ref[...]))
    xn = x + mod_ref[2:3, :] * _dot(y.astype(BF16), wout_ref[...])
    xo_ref[...] = xn
    f = _rms(xn, gf_ref[...]) * (1.0 + mod_ref[4:5, :]) + mod_ref[3:4, :]
    f_ref[...] = f
    route_ref[...] = _route(_dot3(f, rw_ref[...]), rb_ref[...])


def _merge_call(x_flat, mods, o_mla, o_diff, wl, router_w, router_b, n_batch, seq, rows):
    tm = ROW_TILE
    mod_map, _ = _row_class_maps(n_batch * seq // tm, seq // tm, n_batch)
    row_spec = lambda w: pl.BlockSpec((tm, w), lambda i: (i, 0))
    consts = [wl["w_gate"], wl["w_o_mla"], wl["w_o_diff"], wl["w_out"], router_w, router_b]
    return pl.pallas_call(
        _merge_kernel,
        grid=(rows // tm,),
        in_specs=[row_spec(D_MODEL), pl.BlockSpec((None, N_MOD, D_MODEL), mod_map),
                  _const_spec(wl["g_attn"].shape), _const_spec(wl["g_ffn"].shape),
                  row_spec(MLA_W), row_spec(DIFF_HEADS * LANES)]
        + [_const_spec(c.shape) for c in consts],
        out_specs=[row_spec(D_MODEL), row_spec(D_MODEL), row_spec(LANES)],
        out_shape=[jax.ShapeDtypeStruct((rows, D_MODEL), F32), jax.ShapeDtypeStruct((rows, D_MODEL), F32),
                   jax.ShapeDtypeStruct((rows, LANES), F32)],
        compiler_params=pltpu.CompilerParams(
            dimension_semantics=("parallel",), vmem_limit_bytes=VMEM_LIMIT),
        name="merge_router",
    )(x_flat, mods, wl["g_attn"], wl["g_ffn"], o_mla, o_diff, *consts)


PAIR_LO = (0, 0, 0, 1, 1, 2)
PAIR_HI = (1, 2, 3, 2, 3, 3)
N_CLASSES = N_GROUPS * len(PAIR_LO)


def _moe_plan(route, n_rows):
    t = SORT_TILE
    e1 = route[:n_rows, 0].astype(jnp.int32)
    e2 = route[:n_rows, 1].astype(jnp.int32)
    swap = e2 < e1
    lo = jnp.where(swap, e2, e1)
    hi = jnp.where(swap, e1, e2)
    w_lo = jnp.where(swap, route[:n_rows, 3], route[:n_rows, 2])
    w_hi = jnp.where(swap, route[:n_rows, 2], route[:n_rows, 3])
    l = lo % EXPERTS_PER_GROUP
    h = hi % EXPERTS_PER_GROUP
    cls = (lo // EXPERTS_PER_GROUP) * len(PAIR_LO) + (l * (7 - l)) // 2 + (h - l - 1)
    onehot = (cls[:, None] == jnp.arange(N_CLASSES, dtype=jnp.int32)[None, :]).astype(jnp.int32)
    csum = jnp.cumsum(onehot, axis=0)
    rank = jnp.take_along_axis(csum, cls[:, None], axis=1)[:, 0] - 1
    padded = ((csum[-1] + t - 1) // t) * t
    ends = jnp.cumsum(padded)
    pos = (ends - padded)[cls] + rank
    n_tiles = n_rows // t + N_CLASSES
    p = n_tiles * t
    src = jnp.zeros((p,), jnp.int32).at[pos].set(jnp.arange(n_rows, dtype=jnp.int32))
    valid = jnp.zeros((p,), jnp.int32).at[pos].set(1)
    spare = n_rows + jnp.cumsum(1 - valid) - 1
    dst = jnp.where(valid == 1, src, spare)
    tile_cls = jnp.minimum(jnp.searchsorted(ends, jnp.arange(n_tiles, dtype=jnp.int32) * t, side="right"),
                           N_CLASSES - 1).astype(jnp.int32)
    base = (tile_cls // len(PAIR_LO)) * EXPERTS_PER_GROUP
    pair = tile_cls % len(PAIR_LO)
    ea = base + jnp.asarray(PAIR_LO, jnp.int32)[pair]
    eb = base + jnp.asarray(PAIR_HI, jnp.int32)[pair]
    n_used = (ends[-1] // t).astype(jnp.int32).reshape(1)
    w_sorted = jnp.stack([w_lo, w_hi], axis=1)[src]
    return ea, eb, n_used, src.reshape(n_tiles, 1, t), dst.reshape(n_tiles, 1, t), w_sorted


def _moe_sparse_kernel(ea_ref, eb_ref, nu_ref, src0_ref, srcn_ref, dst_ref, w_ref, f_hbm,
                       wgu_s, wd_s, wgu_a, wd_a, wgu_b, wd_b, y_hbm, xbuf, ybuf, gsem, ssem):
    del ea_ref, eb_ref
    t = SORT_TILE
    i = pl.program_id(0)
    n_used = nu_ref[0]
    slot = i % 2

    def start_gather(idx_ref, s):
        for r in range(t):
            pltpu.make_async_copy(f_hbm.at[pl.ds(idx_ref[0, r], 1), :], xbuf.at[s, pl.ds(r, 1), :],
                                  gsem.at[s]).start()

    def gather_done(s):
        return pltpu.make_async_copy(f_hbm.at[pl.ds(0, t), :], xbuf.at[s], gsem.at[s])

    def scatter_done(s):
        return pltpu.make_async_copy(ybuf.at[s], y_hbm.at[pl.ds(0, t), :], ssem.at[s])

    @pl.when(i == 0)
    def _():
        start_gather(src0_ref, 0)

    @pl.when(i < n_used)
    def _():
        gather_done(slot).wait()
        start_gather(srcn_ref, 1 - slot)
        x = xbuf[slot].astype(BF16)

        def ffn(wgu_ref, wd_ref, scale):
            gu = _dot(x, wgu_ref[...])
            g = gu[:, :D_EXPERT]
            he = g * _sigmoid(g) * gu[:, D_EXPERT:]
            if scale is not None:
                he = he * scale
            return _dot(he.astype(BF16), wd_ref[...])

        w = w_ref[...]
        ybuf[slot] = ffn(wgu_s, wd_s, None) + ffn(wgu_a, wd_a, w[:, 0:1]) + ffn(wgu_b, wd_b, w[:, 1:2])
        for r in range(t):
            pltpu.make_async_copy(ybuf.at[slot, pl.ds(r, 1), :], y_hbm.at[pl.ds(dst_ref[0, r], 1), :],
                                  ssem.at[slot]).start()

        @pl.when(i >= 1)
        def _():
            scatter_done(1 - slot).wait()

        @pl.when(i == n_used - 1)
        def _():
            scatter_done(slot).wait()
            gather_done(1 - slot).wait()


def _moe_sparse_call(f, plan, w_gu, w_down):
    ea, eb, n_used, src, dst, w_sorted = plan
    t = SORT_TILE
    n_tiles = src.shape[0]
    smem_spec = lambda imap: pl.BlockSpec((None, 1, t), imap, memory_space=pltpu.SMEM)
    wspec = lambda shape, imap: pl.BlockSpec((None,) + shape, imap)
    gu_shape = (D_MODEL, 2 * D_EXPERT)
    dn_shape = (D_EXPERT, D_MODEL)
    grid_spec = pltpu.PrefetchScalarGridSpec(
        num_scalar_prefetch=3,
        grid=(n_tiles,),
        in_specs=[
            smem_spec(lambda i, ea, eb, nu: (0, 0, 0)),
            smem_spec(lambda i, ea, eb, nu: (jnp.minimum(i + 1, n_tiles - 1), 0, 0)),
            smem_spec(lambda i, ea, eb, nu: (i, 0, 0)),
            pl.BlockSpec((t, 2), lambda i, ea, eb, nu: (i, 0)),
            pl.BlockSpec(memory_space=pl.ANY),
            wspec(gu_shape, lambda i, ea, eb, nu: (0, 0, 0)),
            wspec(dn_shape, lambda i, ea, eb, nu: (0, 0, 0)),
            wspec(gu_shape, lambda i, ea, eb, nu: (ea[i] + 1, 0, 0)),
            wspec(dn_shape, lambda i, ea, eb, nu: (ea[i] + 1, 0, 0)),
            wspec(gu_shape, lambda i, ea, eb, nu: (eb[i] + 1, 0, 0)),
            wspec(dn_shape, lambda i, ea, eb, nu: (eb[i] + 1, 0, 0)),
        ],
        out_specs=pl.BlockSpec(memory_space=pl.ANY),
        scratch_shapes=[pltpu.VMEM((2, t, D_MODEL), F32), pltpu.VMEM((2, t, D_MODEL), F32),
                        pltpu.SemaphoreType.DMA((2,)), pltpu.SemaphoreType.DMA((2,))],
    )
    return pl.pallas_call(
        _moe_sparse_kernel,
        grid_spec=grid_spec,
        out_shape=jax.ShapeDtypeStruct((n_tiles * t, D_MODEL), F32),
        compiler_params=pltpu.CompilerParams(
            dimension_semantics=("arbitrary",), vmem_limit_bytes=VMEM_LIMIT),
        name="moe_sorted",
    )(ea, eb, n_used, src, src, dst, w_sorted, f, w_gu, w_down, w_gu, w_down, w_gu, w_down)


def _residual_kernel(x_ref, y_ref, mod_ref, o_ref):
    o_ref[...] = x_ref[...] + mod_ref[5:6, :] * y_ref[...]


def _residual_call(x_flat, y, mods, n_batch, seq, out_rows):
    tm = MOE_TILE
    mod_map, _ = _row_class_maps(n_batch * seq // tm, seq // tm, n_batch)
    row_spec = pl.BlockSpec((tm, D_MODEL), lambda i: (i, 0))
    return pl.pallas_call(
        _residual_kernel,
        grid=(out_rows // tm,),
        in_specs=[row_spec, row_spec, pl.BlockSpec((None, N_MOD, D_MODEL), mod_map)],
        out_specs=row_spec,
        out_shape=jax.ShapeDtypeStruct((out_rows, D_MODEL), F32),
        compiler_params=pltpu.CompilerParams(
            dimension_semantics=("parallel",), vmem_limit_bytes=VMEM_LIMIT),
        name="ffn_residual",
    )(x_flat, y, mods)


def _axial_angles(seq, d_rope):
    d_axis = d_rope // 2
    inv = ROPE_BASE ** (-jnp.arange(0, d_axis, 2, dtype=F32) / d_axis)
    t = jnp.arange(seq, dtype=jnp.int32)
    r = (t // GRID_W).astype(F32)
    col = (t % GRID_W).astype(F32)
    ang_r = r[:, None] * inv
    ang_c = col[:, None] * inv
    cos = jnp.concatenate([jnp.cos(ang_r)] * 2 + [jnp.cos(ang_c)] * 2, axis=1)
    sin = jnp.concatenate([jnp.sin(ang_r)] * 2 + [jnp.sin(ang_c)] * 2, axis=1)
    nf = d_rope // 4
    first = jnp.concatenate([jnp.ones((nf,), F32), jnp.zeros((nf,), F32)] * 2)
    return cos, -sin * first, sin * (1.0 - first)


def _rope_table(seq, tile):
    cm, sma, smb = _axial_angles(seq, MLA_ROPE)
    pad_l = jnp.zeros((seq, MLA_NOPE), F32)
    pad_r = jnp.zeros((seq, HEAD_PAD - MLA_NOPE - MLA_ROPE), F32)
    cos_m = jnp.concatenate([pad_l + 1.0, cm, pad_r + 1.0], axis=1)
    sin_ma = jnp.concatenate([pad_l, sma, pad_r], axis=1)
    sin_mb = jnp.concatenate([pad_l, smb, pad_r], axis=1)
    cd, sda, sdb = _axial_angles(seq, DIFF_DH)
    tab = jnp.concatenate([cos_m, sin_ma, sin_mb] + [jnp.concatenate([a, a], axis=1) for a in (cd, sda, sdb)],
                          axis=1)
    ident = jnp.concatenate([jnp.ones((tile, LANES), F32), jnp.zeros((tile, 2 * LANES), F32)] * 2, axis=1)
    return jnp.concatenate([tab, ident], axis=0)


def _block_diag_mean(seg_sizes):
    m = jnp.zeros((MXU_DIM, MXU_DIM), F32)
    off = 0
    for size, active in seg_sizes:
        if active:
            m = m.at[off:off + size, off:off + size].set(1.0 / size)
        off += size
    assert off == MXU_DIM
    return m.astype(BF16)


def _prep_weights(attn_norm_g, ffn_norm_g, w_in, mla_q_norm_g, mla_w_uq, mla_kv_norm_g, mla_w_ukv, mla_q_g,
                  mla_k_g, diff_q_g, diff_k_g, diff_sub_g, w_o_mla, w_o_diff, w_out, moe_w_gate, moe_w_up,
                  moe_w_down, sh_w_gate, sh_w_up, sh_w_down):
    depth = w_in.shape[0]
    d = D_MODEL
    z = lambda w: jnp.zeros((depth, d, w), F32)
    off_kr = MLA_KV_RANK
    off_kd = off_kr + MLA_ROPE
    off_vd = off_kd + DIFF_QK_W
    off_cq = off_vd + DIFF_HEADS * DIFF_V
    off_qd = off_cq + MLA_Q_RANK
    off_gate = off_qd + DIFF_QK_W
    w_in_a = jnp.concatenate([
        w_in[:, :, :off_kr], z(MLA_NOPE), w_in[:, :, off_kr:off_kd], z(HEAD_PAD - MLA_NOPE - MLA_ROPE),
        w_in[:, :, off_kd:off_gate]], axis=2).astype(BF16)
    w_gate = w_in[:, :, off_gate:].astype(BF16)

    ukv = mla_w_ukv.reshape(depth, MLA_KV_RANK, MLA_HEADS, MLA_NOPE + MLA_V)
    pad_k = jnp.zeros((depth, MLA_KV_RANK, MLA_HEADS, HEAD_PAD - MLA_NOPE), F32)
    pad_v = jnp.zeros((depth, MLA_KV_RANK, MLA_HEADS, HEAD_PAD - MLA_V), F32)
    w_ukv = jnp.concatenate([
        jnp.concatenate([ukv[..., :MLA_NOPE], pad_k], axis=-1).reshape(depth, MLA_KV_RANK, MLA_W),
        jnp.concatenate([ukv[..., MLA_NOPE:], pad_v], axis=-1).reshape(depth, MLA_KV_RANK, MLA_W)],
        axis=-1).astype(BF16)
    uq = mla_w_uq.reshape(depth, MLA_Q_RANK, MLA_HEADS, MLA_NOPE + MLA_ROPE)
    w_uq = jnp.pad(uq, ((0, 0), (0, 0), (0, 0), (0, HEAD_PAD - MLA_NOPE - MLA_ROPE))).reshape(
        depth, MLA_Q_RANK, MLA_W).astype(BF16)
    wom = w_o_mla.reshape(depth, MLA_HEADS, MLA_V, d)
    w_o_mla_p = jnp.pad(wom, ((0, 0), (0, 0), (0, HEAD_PAD - MLA_V), (0, 0))).reshape(depth, MLA_W, d).astype(BF16)

    def per_head(g96, scale):
        g = jnp.pad(g96 * scale, ((0, 0), (0, HEAD_PAD - MLA_NOPE - MLA_ROPE)))
        return jnp.tile(g, (1, MLA_HEADS)).reshape(depth, 1, MLA_W)

    k_nope_only = jnp.concatenate([mla_k_g[:, :MLA_NOPE], jnp.zeros((depth, MLA_ROPE), F32)], axis=1)
    g_k = per_head(k_nope_only, 1.0)
    g_kr = jnp.concatenate([jnp.zeros((depth, MLA_NOPE), F32), mla_k_g[:, MLA_NOPE:],
                            jnp.zeros((depth, HEAD_PAD - MLA_NOPE - MLA_ROPE), F32)], axis=1).reshape(depth, 1, LANES)
    g_q = per_head(mla_q_g, MLA_SCALE * LOG2E)
    g_kd = jnp.tile(diff_k_g.reshape(depth, 2 * DIFF_DH), (1, DIFF_HEADS)).reshape(depth, 1, DIFF_QK_W)
    g_qd = jnp.tile(diff_q_g.reshape(depth, 2 * DIFF_DH) * (DIFF_SCALE * LOG2E),
                    (1, DIFF_HEADS)).reshape(depth, 1, DIFF_QK_W)
    lam_inits = [0.8 - 0.6 * math.exp(-0.3 * l) for l in range(depth)]
    g_sub = diff_sub_g * (1.0 - jnp.asarray(lam_inits, F32))[:, None]

    v_one = jnp.tile((jnp.arange(HEAD_PAD) == MLA_V).astype(F32), MLA_HEADS).reshape(1, MLA_W)
    b_k = _block_diag_mean([(MLA_NOPE, True), (HEAD_PAD - MLA_NOPE, False)] * 2)
    b_q = _block_diag_mean([(MLA_NOPE, True), (MLA_ROPE, True), (HEAD_PAD - MLA_NOPE - MLA_ROPE, False)] * 2)
    b_d = _block_diag_mean([(DIFF_DH, True)] * 4)

    w_gu = jnp.concatenate([
        jnp.concatenate([sh_w_gate, sh_w_up], axis=-1)[:, None].astype(BF16),
        jnp.concatenate([moe_w_gate.astype(BF16), moe_w_up.astype(BF16)], axis=-1)], axis=1)
    w_dn = jnp.concatenate([sh_w_down[:, None].astype(BF16), moe_w_down.astype(BF16)], axis=1)

    layers = []
    for l in range(depth):
        layers.append({
            "g_attn": attn_norm_g[l].reshape(1, d), "g_ffn": ffn_norm_g[l].reshape(1, d),
            "w_in_a": w_in_a[l], "w_gate": w_gate[l], "w_ukv": w_ukv[l], "w_uq": w_uq[l],
            "g_kv": mla_kv_norm_g[l].reshape(1, MLA_KV_RANK), "g_cq": mla_q_norm_g[l].reshape(1, MLA_Q_RANK),
            "g_k": g_k[l], "g_kr": g_kr[l], "g_q": g_q[l], "g_kd": g_kd[l], "g_qd": g_qd[l],
            "g_sub": g_sub[l].reshape(1, DIFF_V), "b_k": b_k, "b_q": b_q, "b_d": b_d, "v_one": v_one,
            "w_o_mla": w_o_mla_p[l], "w_o_diff": w_o_diff[l].astype(BF16), "w_out": w_out[l].astype(BF16),
            "w_gu": w_gu[l], "w_dn": w_dn[l], "lam_init": lam_inits[l],
        })
    return layers


def kernel(x, c, ctx, c_ctx, ada_w, ada_b, attn_norm_g, ffn_norm_g, w_in, mla_q_norm_g, mla_w_uq, mla_kv_norm_g, mla_w_ukv, mla_q_g, mla_k_g, diff_q_g, diff_k_g, diff_lam, diff_sub_g, w_o_mla, w_o_diff, w_out, router_w, router_b, moe_w_gate, moe_w_up, moe_w_down, sh_w_gate, sh_w_up, sh_w_down):
    n_batch, seq, d = x.shape
    n_ctx = ctx.shape[1]
    depth = w_in.shape[0]
    assert d == D_MODEL and n_ctx == ROW_TILE and seq % GRID_W == 0
    assert (seq + n_ctx) % KV_CHUNK == 0 and seq % MLA_TQ == 0 and seq % DIFF_TQ == 0
    assert (n_batch * seq) % MOE_TILE == 0 and (n_batch * n_ctx) % MOE_TILE == 0
    lat_rows = n_batch * seq
    all_rows = lat_rows + n_batch * n_ctx

    layers = _prep_weights(attn_norm_g, ffn_norm_g, w_in, mla_q_norm_g, mla_w_uq, mla_kv_norm_g, mla_w_ukv,
                           mla_q_g, mla_k_g, diff_q_g, diff_k_g, diff_sub_g, w_o_mla, w_o_diff, w_out,
                           moe_w_gate, moe_w_up, moe_w_down, sh_w_gate, sh_w_up, sh_w_down)
    rope_tab = _rope_table(seq, ROW_TILE)
    rw = jnp.pad(router_w, ((0, 0), (1, LANES - 1 - N_EXPERTS)))
    rb = jnp.pad(router_b, (1, LANES - 1 - N_EXPERTS)).reshape(1, LANES)

    pad_rows = (-(n_batch + 1)) % 8
    c_all = jnp.concatenate([c, c_ctx[None, :], jnp.zeros((pad_rows, d), F32)], axis=0)
    mods_all = _mods_call(c_all, ada_w, ada_b).reshape(depth, c_all.shape[0], N_MOD, d)

    xf = jnp.concatenate([x.reshape(lat_rows, d), ctx.reshape(n_batch * n_ctx, d)], axis=0)
    for l in range(depth):
        wl = layers[l]
        mods = mods_all[l]
        k_m, v_m, q_m, k_d, v_d, q_d = _proj_call(xf, mods, wl, rope_tab, n_batch, seq)
        last = l == depth - 1
        o_m = _attn_call(q_m, k_m, v_m, n_batch, seq, n_ctx, MLA_HEADS, 1, [], 0.0, MLA_TQ, not last, "attn_mla")
        o_d = _attn_call(q_d, k_d, v_d, n_batch, seq, n_ctx, DIFF_HEADS, 2, [diff_lam[l], wl["g_sub"]],
                         wl["lam_init"], DIFF_TQ, not last, "attn_diff")
        out_rows = lat_rows if last else all_rows
        xf, f, route = _merge_call(xf, mods, o_m, o_d, wl, rw, rb, n_batch, seq, out_rows)
        y = _moe_sparse_call(f, _moe_plan(route, out_rows), wl["w_gu"], wl["w_dn"])
        xf = _residual_call(xf, y, mods, n_batch, seq, out_rows)
    return xf.reshape(n_batch, seq, d)
```

```python
import functools
import math

import jax
import jax.numpy as jnp
from jax import lax
from jax.experimental import pallas as pl
from jax.experimental.pallas import tpu as pltpu

F32 = jnp.float32
BF16 = jnp.bfloat16

D_MODEL = 1024
GRID_W = 64
EPS = 1e-6
ROPE_BASE = 10000.0
N_MOD = 6

MLA_HEADS = 8
MLA_NOPE = 64
MLA_ROPE = 32
MLA_V = 64
MLA_Q_RANK = 512
MLA_KV_RANK = 256
MLA_SCALE = 1.0 / math.sqrt(MLA_NOPE + MLA_ROPE)

DIFF_HEADS = 4
DIFF_DH = 64
DIFF_V = 2 * DIFF_DH
DIFF_SCALE = 1.0 / math.sqrt(DIFF_DH)

N_EXPERTS = 16
EXPERTS_PER_GROUP = 4
N_GROUPS = N_EXPERTS // EXPERTS_PER_GROUP
D_EXPERT = 512
PAIR_LO = (0, 0, 0, 1, 1, 2)
PAIR_HI = (1, 2, 3, 2, 3, 3)
N_PAIRS = len(PAIR_LO)
N_CLASSES = N_GROUPS * N_PAIRS

LANES = 128
MXU_DIM = 256
HEAD_PAD = LANES
MLA_W = MLA_HEADS * HEAD_PAD
DIFF_QK_W = DIFF_HEADS * 2 * DIFF_DH
DIFF_V_PAD = 2 * LANES
DIFF_V_W = DIFF_HEADS * DIFF_V_PAD

A_CKV = 0
A_KR = A_CKV + MLA_KV_RANK
A_KD = A_KR + LANES
A_VD = A_KD + DIFF_QK_W
A_CQ = A_VD + DIFF_HEADS * DIFF_V
A_QD = A_CQ + MLA_Q_RANK
A_COLS = A_QD + DIFF_QK_W

ROPE_TAB_W = 6 * LANES
F_EXT_W = D_MODEL + LANES

ROW_TILE = 256
MOE_TILE = 512
SORT_TILE = 256
KV_CHUNK = 768
MLA_TQ = 512
DIFF_TQ = 256
LOOP_STAGES = 4
LOG2E = math.log2(math.e)
VMEM_LIMIT = 56 * 1024 * 1024


def _sigmoid(x):
    return 1.0 / (1.0 + jnp.exp(-x))


def _dot(a, b):
    return jnp.dot(a, b, preferred_element_type=F32)


def _split_bf16(a):
    hi = a.astype(BF16)
    lo = (a - hi.astype(F32)).astype(BF16)
    return hi, lo


def _dot3(a, b):
    a_hi, a_lo = _split_bf16(a)
    b_hi, b_lo = _split_bf16(b)
    return _dot(a_hi, b_hi) + _dot(a_lo, b_hi) + _dot(a_hi, b_lo)


def _mods_kernel(c_ref, w_ref, b_ref, o_ref):
    c = c_ref[...]
    a = c * _sigmoid(c)
    o_ref[...] = _dot3(a, w_ref[...]) + b_ref[...]


def _mods_call(c_all, ada_w, ada_b):
    depth, d, n = ada_w.shape
    tn = 1536
    rows = c_all.shape[0]
    return pl.pallas_call(
        _mods_kernel,
        grid=(depth, n // tn),
        in_specs=[
            pl.BlockSpec((rows, d), lambda l, j: (0, 0)),
            pl.BlockSpec((None, d, tn), lambda l, j: (l, 0, j)),
            pl.BlockSpec((None, 1, tn), lambda l, j: (l, 0, j)),
        ],
        out_specs=pl.BlockSpec((None, rows, tn), lambda l, j: (l, 0, j)),
        out_shape=jax.ShapeDtypeStruct((depth, rows, n), F32),
        compiler_params=pltpu.CompilerParams(
            dimension_semantics=("parallel", "parallel"), vmem_limit_bytes=VMEM_LIMIT),
        name="adaln_mods",
    )(c_all, ada_w, ada_b.reshape(depth, 1, n))


def _rms(x, g):
    return x * lax.rsqrt(jnp.mean(x * x, axis=-1, keepdims=True) + EPS) * g


def _seg_mean_sq(x, b_ref):
    sq = (x * x).astype(BF16)
    b = b_ref[...]
    parts = [_dot(sq[:, c:c + MXU_DIM], b) for c in range(0, x.shape[1], MXU_DIM)]
    return parts[0] if len(parts) == 1 else jnp.concatenate(parts, axis=1)


def _rope(x, cos, sin_a, sin_b, half):
    return (x * cos + pltpu.roll(x, LANES - half, axis=1) * sin_a
            + pltpu.roll(x, half, axis=1) * sin_b)


def _proj_kernel(x_ref, mod_ref, g_ref, win_ref, wukv_ref, wuq_ref, gkv_ref, gcq_ref, gk_ref, gkr_ref,
                 gq_ref, gkd_ref, gqd_ref, bk_ref, bq_ref, bd_ref, vone_ref, rope_ref,
                 kmla_ref, vmla_ref, qmla_ref, kd_ref, vd_ref, qd_ref):
    x = x_ref[...]
    h = _rms(x, g_ref[...]) * (1.0 + mod_ref[1:2, :]) + mod_ref[0:1, :]
    p = _dot(h.astype(BF16), win_ref[...])

    cos_m = rope_ref[:, 0 * LANES:1 * LANES]
    sin_ma = rope_ref[:, 1 * LANES:2 * LANES]
    sin_mb = rope_ref[:, 2 * LANES:3 * LANES]
    cos_d = rope_ref[:, 3 * LANES:4 * LANES]
    sin_da = rope_ref[:, 4 * LANES:5 * LANES]
    sin_db = rope_ref[:, 5 * LANES:6 * LANES]

    ckv = _rms(p[:, A_CKV:A_KR], gkv_ref[...])
    kv = _dot(ckv.astype(BF16), wukv_ref[...])
    kn = kv[:, :MLA_W]
    kn = kn * lax.rsqrt(_seg_mean_sq(kn, bk_ref) + EPS) * gk_ref[...]
    kr = p[:, A_KR:A_KD]
    kr = kr * lax.rsqrt(jnp.sum(kr * kr, axis=-1, keepdims=True) * (1.0 / MLA_ROPE) + EPS) * gkr_ref[...]
    kr = _rope(kr, cos_m, sin_ma, sin_mb, MLA_ROPE // 4)
    for hd in range(MLA_HEADS):
        sl = slice(hd * HEAD_PAD, (hd + 1) * HEAD_PAD)
        kmla_ref[:, sl] = (kn[:, sl] + kr).astype(BF16)
    vmla_ref[...] = (kv[:, MLA_W:] + vone_ref[...]).astype(BF16)

    cq = _rms(p[:, A_CQ:A_QD], gcq_ref[...])
    q = _dot(cq.astype(BF16), wuq_ref[...])
    q = q * lax.rsqrt(_seg_mean_sq(q, bq_ref) + EPS) * gq_ref[...]
    for hd in range(MLA_HEADS):
        sl = slice(hd * HEAD_PAD, (hd + 1) * HEAD_PAD)
        qmla_ref[:, sl] = _rope(q[:, sl], cos_m, sin_ma, sin_mb, MLA_ROPE // 4).astype(BF16)

    kd = p[:, A_KD:A_VD]
    kd = kd * lax.rsqrt(_seg_mean_sq(kd, bd_ref) + EPS) * gkd_ref[...]
    qd = p[:, A_QD:A_COLS]
    qd = qd * lax.rsqrt(_seg_mean_sq(qd, bd_ref) + EPS) * gqd_ref[...]
    for hd in range(DIFF_HEADS):
        sl = slice(hd * LANES, (hd + 1) * LANES)
        kd_ref[:, sl] = _rope(kd[:, sl], cos_d, sin_da, sin_db, DIFF_DH // 4).astype(BF16)
        qd_ref[:, sl] = _rope(qd[:, sl], cos_d, sin_da, sin_db, DIFF_DH // 4).astype(BF16)

    vd = p[:, A_VD:A_CQ]
    tm = x.shape[0]
    ones_blk = (lax.broadcasted_iota(jnp.int32, (tm, LANES), 1) == 0).astype(BF16)
    for hd in range(DIFF_HEADS):
        vd_ref[:, hd * DIFF_V_PAD:hd * DIFF_V_PAD + DIFF_V] = vd[:, hd * DIFF_V:(hd + 1) * DIFF_V].astype(BF16)
        vd_ref[:, hd * DIFF_V_PAD + DIFF_V:(hd + 1) * DIFF_V_PAD] = ones_blk


def _row_class_maps(n_lat_tiles, tiles_per_batch, n_batch):
    def mod_map(i):
        return (jnp.where(i < n_lat_tiles, i // tiles_per_batch, n_batch), 0, 0)

    def rope_map(i):
        return (jnp.where(i < n_lat_tiles, i % tiles_per_batch, tiles_per_batch), 0)

    return mod_map, rope_map


def _const_spec(shape):
    return pl.BlockSpec(shape, lambda i: tuple(0 for _ in shape))


def _proj_call(x_flat, mods, wl, rope_tab, n_batch, seq):
    rows = x_flat.shape[0]
    tm = ROW_TILE
    n_lat_tiles = n_batch * seq // tm
    mod_map, rope_map = _row_class_maps(n_lat_tiles, seq // tm, n_batch)
    row_spec = lambda w: pl.BlockSpec((tm, w), lambda i: (i, 0))
    tiles_per_batch = seq // tm

    def kv_map(i):
        lat = i + i // tiles_per_batch + 1
        ctx = (i - n_lat_tiles) * (tiles_per_batch + 1)
        return (jnp.where(i < n_lat_tiles, lat, ctx), 0)

    kv_spec = lambda w: pl.BlockSpec((tm, w), kv_map)
    consts = [wl["g_attn"], wl["w_in_a"], wl["w_ukv"], wl["w_uq"], wl["g_kv"], wl["g_cq"], wl["g_k"], wl["g_kr"],
              wl["g_q"], wl["g_kd"], wl["g_qd"], wl["b_k"], wl["b_q"], wl["b_d"], wl["v_one"]]
    out_w = [MLA_W, MLA_W, MLA_W, DIFF_QK_W, DIFF_V_W, DIFF_QK_W]
    out_specs = [kv_spec(MLA_W), kv_spec(MLA_W), row_spec(MLA_W), kv_spec(DIFF_QK_W), kv_spec(DIFF_V_W),
                 row_spec(DIFF_QK_W)]
    return pl.pallas_call(
        _proj_kernel,
        grid=(rows // tm,),
        in_specs=[row_spec(D_MODEL), pl.BlockSpec((None, N_MOD, D_MODEL), mod_map)]
        + [_const_spec(c.shape) for c in consts]
        + [pl.BlockSpec((tm, ROPE_TAB_W), rope_map)],
        out_specs=out_specs,
        out_shape=[jax.ShapeDtypeStruct((rows, w), BF16) for w in out_w],
        compiler_params=pltpu.CompilerParams(
            dimension_semantics=("parallel",), vmem_limit_bytes=VMEM_LIMIT),
        name="proj_qkv",
    )(x_flat, mods, *consts, rope_tab)


def _attn_kernel(*refs, n_maps, tq, tk, chunks, lam_init, aliased_out):
    refs = list(refs)
    q_ref, k_ref, v_ref = refs[:3]
    n_in = 3 + (2 if n_maps == 2 else 0) + (1 if aliased_out else 0)
    o_ref = refs[n_in]
    scr = refs[n_in + 1:]
    if n_maps == 2:
        lam_ref, subg_ref = refs[3:5]
        qs_ref, s_ref, p_ref, m_ref, a_ref, acc_ref = scr
        q = q_ref[...]
        lane = lax.broadcasted_iota(jnp.int32, q.shape, 1)
        zero = jnp.zeros_like(q)
        qs_ref[0:tq, :] = jnp.where(lane < DIFF_DH, q, zero)
        qs_ref[tq:2 * tq, :] = jnp.where(lane >= DIFF_DH, q, zero)
        q_src = qs_ref
    else:
        s_ref, p_ref, m_ref, a_ref, acc_ref = scr
        q_src = q_ref
    vw = v_ref.shape[1]
    sum_col = vw // 2

    head, n_mid, tail = chunks
    n_chunks = 1 + n_mid + (1 if tail else 0)

    def geom(j):
        if isinstance(j, int):
            if j == 0:
                return 0, head
            if j <= n_mid:
                return head + (j - 1) * tk, tk
            return head + n_mid * tk, tail
        return pl.multiple_of(head + (j - 1) * tk, math.gcd(head, tk)), tk

    def rows_of(ref, j):
        off, size = geom(j)
        return ref[pl.ds(off, size), :]

    def scores(j, slot):
        size = geom(j)[1]
        s_ref[slot, :, :size] = lax.dot_general(q_src[...], rows_of(k_ref, j), (((1,), (1,)), ((), ())),
                                                preferred_element_type=F32)

    def softmax(j, slot):
        size = geom(j)[1]
        s = s_ref[slot, :, :size]
        mx = jnp.max(s, axis=-1, keepdims=True)
        if isinstance(j, int) and j == 0:
            m_new = mx
        else:
            m_old = m_ref[...]
            m_new = jnp.maximum(m_old, mx)
            a_ref[slot] = jnp.exp2(m_old - m_new)
        m_ref[...] = m_new
        p_ref[slot, :, :size] = jnp.exp2(s - m_new).astype(BF16)

    def accumulate(j, slot):
        size = geom(j)[1]
        r = _dot(p_ref[slot, :, :size], rows_of(v_ref, j))
        if isinstance(j, int) and j == 0:
            acc_ref[...] = r
        else:
            acc_ref[...] = acc_ref[...] * a_ref[slot] + r

    def stage(j, slot, with_scores):
        softmax(j, slot)
        accumulate(j - 1, 1 - slot)
        if with_scores:
            scores(j + 1, 1 - slot)

    scores(0, 0)
    if n_chunks > 1:
        scores(1, 1)
    softmax(0, 0)
    if n_chunks > 1:
        stage(1, 1, n_chunks > 2)
        n_loop = max(n_mid - 2, 0) // LOOP_STAGES

        def group(i, carry):
            j0 = 2 + LOOP_STAGES * i
            for t in range(LOOP_STAGES):
                stage(j0 + t, t % 2, True)
            return carry

        lax.fori_loop(0, n_loop, group, 0)
        for j in range(2 + LOOP_STAGES * n_loop, n_chunks):
            stage(j, j % 2, j + 1 < n_chunks)
    accumulate(n_chunks - 1, (n_chunks - 1) % 2)

    acc = acc_ref[...]
    inv_l = 1.0 / acc[:, sum_col:sum_col + 1]
    if n_maps == 1:
        o_ref[...] = (acc * inv_l).astype(o_ref.dtype)
    else:
        lf = lam_ref[...]
        lam = (jnp.exp(jnp.sum(lf[0:1, :] * lf[1:2, :], axis=-1, keepdims=True))
               - jnp.exp(jnp.sum(lf[2:3, :] * lf[3:4, :], axis=-1, keepdims=True)) + lam_init)
        o = acc[:, :DIFF_V] * inv_l
        o = o[:tq] - lam * o[tq:]
        o_ref[...] = _rms(o, subg_ref[...]).astype(o_ref.dtype)


def _attn_scratch(n_maps, tq, tk, vw):
    r = n_maps * tq
    qs = [pltpu.VMEM((r, LANES), BF16)] if n_maps == 2 else []
    return qs + [pltpu.VMEM((2, r, tk), F32), pltpu.VMEM((2, r, tk), BF16), pltpu.VMEM((r, 1), F32),
                 pltpu.VMEM((2, r, 1), F32), pltpu.VMEM((r, vw), F32)]


def _attn_call(q, k, v, n_batch, seq, n_ctx, n_heads, n_maps, extra, lam_init, tq, with_ctx_queries, name):
    rows = q.shape[0]
    vw = v.shape[1] // n_heads
    keys = seq + n_ctx
    tk = KV_CHUNK
    nq = seq // tq
    assert seq % tq == 0
    chunks = (tk, keys // tk - 1, keys % tk)
    width = tk
    ctx_blk0 = n_batch * seq // n_ctx
    blocks_per_batch = keys // n_ctx
    extra_specs = [pl.BlockSpec(e.shape, lambda *_: (0, 0)) for e in extra]
    out_shape = jax.ShapeDtypeStruct((rows, n_heads * LANES), BF16)

    kern = functools.partial(_attn_kernel, n_maps=n_maps, tq=tq, tk=tk, chunks=chunks, lam_init=lam_init,
                             aliased_out=False)
    out = pl.pallas_call(
        kern,
        grid=(n_batch, n_heads, nq),
        in_specs=[pl.BlockSpec((tq, LANES), lambda b, h, qi: (b * nq + qi, h)),
                  pl.BlockSpec((keys, LANES), lambda b, h, qi: (b, h)),
                  pl.BlockSpec((keys, vw), lambda b, h, qi: (b, h))] + extra_specs,
        out_specs=pl.BlockSpec((tq, LANES), lambda b, h, qi: (b * nq + qi, h)),
        out_shape=out_shape,
        scratch_shapes=_attn_scratch(n_maps, tq, width, vw),
        compiler_params=pltpu.CompilerParams(
            dimension_semantics=("parallel", "parallel", "arbitrary"), vmem_limit_bytes=VMEM_LIMIT),
        name=name,
    )(q, k, v, *extra)
    if not with_ctx_queries:
        return out

    kern_c = functools.partial(_attn_kernel, n_maps=n_maps, tq=n_ctx, tk=n_ctx, chunks=(n_ctx, 0, 0),
                               lam_init=lam_init, aliased_out=True)
    return pl.pallas_call(
        kern_c,
        grid=(n_batch, n_heads),
        in_specs=[pl.BlockSpec((n_ctx, LANES), lambda b, h: (ctx_blk0 + b, h)),
                  pl.BlockSpec((n_ctx, LANES), lambda b, h: (b * blocks_per_batch, h)),
                  pl.BlockSpec((n_ctx, vw), lambda b, h: (b * blocks_per_batch, h))] + extra_specs
        + [pl.BlockSpec(memory_space=pl.ANY)],
        out_specs=pl.BlockSpec((n_ctx, LANES), lambda b, h: (ctx_blk0 + b, h)),
        out_shape=out_shape,
        scratch_shapes=_attn_scratch(n_maps, n_ctx, n_ctx, vw),
        input_output_aliases={3 + len(extra): 0},
        compiler_params=pltpu.CompilerParams(
            dimension_semantics=("parallel", "parallel"), vmem_limit_bytes=VMEM_LIMIT),
        name=name + "_ctx",
    )(q, k, v, *extra, out)


def _route(logits, bias):
    lane = lax.broadcasted_iota(jnp.int32, logits.shape, 1)
    is_exp = (lane >= 1) & (lane <= N_EXPERTS)
    grp = (lane - 1) >> 2
    neg = jnp.float32(-jnp.inf)
    big = jnp.int32(1 << 20)
    aff = _sigmoid(logits)
    sel = jnp.where(is_exp, aff + bias, neg)

    def first_argmax(v):
        mx = jnp.max(v, axis=-1, keepdims=True)
        ix = jnp.min(jnp.where(v == mx, lane, big), axis=-1, keepdims=True)
        return mx, ix

    best = None
    for g in range(N_GROUPS):
        vg = jnp.where(grp == g, sel, neg)
        m1, i1 = first_argmax(vg)
        m2, i2 = first_argmax(jnp.where(lane == i1, neg, vg))
        score = m1 + m2
        if best is None:
            best = (score, i1, i2)
        else:
            upd = score > best[0]
            best = (jnp.where(upd, score, best[0]), jnp.where(upd, i1, best[1]), jnp.where(upd, i2, best[2]))
    _, i1, i2 = best
    a1 = jnp.sum(jnp.where(lane == i1, aff, 0.0), axis=-1, keepdims=True)
    a2 = jnp.sum(jnp.where(lane == i2, aff, 0.0), axis=-1, keepdims=True)
    tot = a1 + a2
    e1 = i1 - 1
    e2 = i2 - 1
    swap = e2 < e1
    lo = jnp.where(swap, e2, e1)
    hi = jnp.where(swap, e1, e2)
    w_lo = jnp.where(swap, a2, a1) / tot
    w_hi = jnp.where(swap, a1, a2) / tot
    l = lo & (EXPERTS_PER_GROUP - 1)
    h = hi & (EXPERTS_PER_GROUP - 1)
    cls = (lo >> 2) * N_PAIRS + ((l * (7 - l)) >> 1) + (h - l - 1)
    return cls, w_lo, w_hi


def _merge_kernel(x_ref, mod_ref, ga_ref, gf_ref, om_ref, od_ref, wgate_ref, wom_ref, wod_ref, wout_ref,
                  rw_ref, rb_ref, tri_ref, xo_ref, f_ref, pos_ref, cnt_ref, *, cap):
    @pl.when(pl.program_id(0) == 0)
    def _():
        cnt_ref[...] = jnp.zeros_like(cnt_ref)

    x = x_ref[...]
    h = _rms(x, ga_ref[...]) * (1.0 + mod_ref[1:2, :]) + mod_ref[0:1, :]
    gates = _sigmoid(_dot(h.astype(BF16), wgate_ref[...]))
    y = (gates[:, :D_MODEL] * _dot(om_ref[...], wom_ref[...])
         + gates[:, D_MODEL:] * _dot(od_ref[...], wod_ref[...]))
    xn = x + mod_ref[2:3, :] * _dot(y.astype(BF16), wout_ref[...])
    xo_ref[...] = xn
    f = _rms(xn, gf_ref[...]) * (1.0 + mod_ref[4:5, :]) + mod_ref[3:4, :]
    cls, w_lo, w_hi = _route(_dot3(f, rw_ref[...]), rb_ref[...])
    lane = lax.broadcasted_iota(jnp.int32, (x.shape[0], LANES), 1)
    f_ref[:, :D_MODEL] = f
    f_ref[:, D_MODEL:] = jnp.where(lane == 0, w_lo, jnp.where(lane == 1, w_hi, 0.0))
    onehot = (lane == cls).astype(F32)
    before = _dot(tri_ref[...], onehot.astype(BF16)) + cnt_ref[...]
    rank = jnp.sum(onehot * before, axis=-1, keepdims=True)
    cnt_ref[...] += jnp.sum(onehot, axis=0, keepdims=True)
    pos = cls.astype(F32) * float(cap) + rank
    pos_ref[...] = jnp.where(lane == 0, pos, 0.0)


def _merge_call(x_flat, mods, o_mla, o_diff, wl, router_w, router_b, n_batch, seq, rows):
    tm = ROW_TILE
    mod_map, _ = _row_class_maps(n_batch * seq // tm, seq // tm, n_batch)
    row_spec = lambda w: pl.BlockSpec((tm, w), lambda i: (i, 0))
    tri = (jnp.arange(tm)[:, None] > jnp.arange(tm)[None, :]).astype(BF16)
    consts = [wl["w_gate"], wl["w_o_mla"], wl["w_o_diff"], wl["w_out"], router_w, router_b, tri]
    return pl.pallas_call(
        functools.partial(_merge_kernel, cap=rows),
        grid=(rows // tm,),
        in_specs=[row_spec(D_MODEL), pl.BlockSpec((None, N_MOD, D_MODEL), mod_map),
                  _const_spec(wl["g_attn"].shape), _const_spec(wl["g_ffn"].shape),
                  row_spec(MLA_W), row_spec(DIFF_HEADS * LANES)]
        + [_const_spec(c.shape) for c in consts],
        out_specs=[row_spec(D_MODEL), row_spec(F_EXT_W), row_spec(LANES), _const_spec((1, LANES))],
        out_shape=[jax.ShapeDtypeStruct((rows, D_MODEL), F32), jax.ShapeDtypeStruct((rows, F_EXT_W), F32),
                   jax.ShapeDtypeStruct((rows, LANES), F32), jax.ShapeDtypeStruct((1, LANES), F32)],
        compiler_params=pltpu.CompilerParams(
            dimension_semantics=("arbitrary",), vmem_limit_bytes=VMEM_LIMIT),
        name="merge_router",
    )(x_flat, mods, wl["g_attn"], wl["g_ffn"], o_mla, o_diff, *consts)


def _moe_plan(pos_rec, counts_rec, n_rows):
    t = SORT_TILE
    tiles_per_bucket = n_rows // t
    counts = counts_rec[0, :N_CLASSES].astype(jnp.int32)
    pos = pos_rec[:, 0].astype(jnp.int32)
    src = jnp.zeros((N_CLASSES * n_rows,), jnp.int32).at[pos].set(jnp.arange(n_rows, dtype=jnp.int32))
    class_tiles = (counts + t - 1) // t
    tile_end = jnp.cumsum(class_tiles)
    n_tiles = tiles_per_bucket + N_CLASSES
    tile = jnp.arange(n_tiles, dtype=jnp.int32)
    tile_cls = jnp.minimum(jnp.searchsorted(tile_end, tile, side="right"), N_CLASSES - 1).astype(jnp.int32)
    in_class = tile - (tile_end - class_tiles)[tile_cls]
    n_used = tile_end[-1].reshape(1)
    used = tile < n_used[0]
    blk = jnp.where(used, tile_cls * tiles_per_bucket + in_class, 0)
    n_valid = jnp.where(used, jnp.clip(counts[tile_cls] - in_class * t, 0, t), 0)
    base = (tile_cls // N_PAIRS) * EXPERTS_PER_GROUP
    pair = tile_cls % N_PAIRS
    ea = base + jnp.asarray(PAIR_LO, jnp.int32)[pair]
    eb = base + jnp.asarray(PAIR_HI, jnp.int32)[pair]
    return ea, eb, n_used, blk, n_valid, src.reshape(N_CLASSES * tiles_per_bucket, 1, t)


def _moe_sparse_kernel(ea_ref, eb_ref, nu_ref, blk_ref, nv_ref, src0_ref, srcn_ref, srcc_ref, f_hbm,
                       wgu_s, wd_s, wgu_a, wd_a, wgu_b, wd_b, y_hbm, xbuf, ybuf, gsem, ssem, *, n_rows):
    del ea_ref, eb_ref, blk_ref
    t = SORT_TILE
    i = pl.program_id(0)
    n_used = nu_ref[0]
    n_valid = nv_ref[i]
    slot = i % 2

    def start_gather(idx_ref, s):
        for r in range(t):
            pltpu.make_async_copy(f_hbm.at[pl.ds(idx_ref[0, r], 1), :], xbuf.at[s, pl.ds(r, 1), :],
                                  gsem.at[s]).start()

    def gather_done(s):
        return pltpu.make_async_copy(f_hbm.at[pl.ds(0, t), :], xbuf.at[s], gsem.at[s])

    def scatter_done(s):
        return pltpu.make_async_copy(ybuf.at[s], y_hbm.at[pl.ds(0, t), :], ssem.at[s])

    @pl.when(i == 0)
    def _():
        start_gather(src0_ref, 0)

    @pl.when(i < n_used)
    def _():
        gather_done(slot).wait()
        start_gather(srcn_ref, 1 - slot)
        x = xbuf[slot, :, :D_MODEL].astype(BF16)
        w = xbuf[slot, :, D_MODEL:]

        def ffn(wgu_ref, wd_ref, scale):
            gu = _dot(x, wgu_ref[...])
            g = gu[:, :D_EXPERT]
            he = g * _sigmoid(g) * gu[:, D_EXPERT:]
            if scale is not None:
                he = he * scale
            return _dot(he.astype(BF16), wd_ref[...])

        ybuf[slot] = ffn(wgu_s, wd_s, None) + ffn(wgu_a, wd_a, w[:, 0:1]) + ffn(wgu_b, wd_b, w[:, 1:2])
        for r in range(t):
            dst = jnp.where(r < n_valid, srcc_ref[0, r], n_rows + slot * t + r)
            pltpu.make_async_copy(ybuf.at[slot, pl.ds(r, 1), :], y_hbm.at[pl.ds(dst, 1), :],
                                  ssem.at[slot]).start()

        @pl.when(i >= 1)
        def _():
            scatter_done(1 - slot).wait()

        @pl.when(i == n_used - 1)
        def _():
            scatter_done(slot).wait()
            gather_done(1 - slot).wait()


def _moe_sparse_call(f_ext, plan, w_gu, w_down, n_rows):
    ea, eb, n_used, blk, n_valid, src = plan
    t = SORT_TILE
    n_tiles = blk.shape[0]
    smem_spec = lambda imap: pl.BlockSpec((None, 1, t), imap, memory_space=pltpu.SMEM)
    wspec = lambda shape, imap: pl.BlockSpec((None,) + shape, imap)
    gu_shape = (D_MODEL, 2 * D_EXPERT)
    dn_shape = (D_EXPERT, D_MODEL)
    grid_spec = pltpu.PrefetchScalarGridSpec(
        num_scalar_prefetch=5,
        grid=(n_tiles,),
        in_specs=[
            smem_spec(lambda i, ea, eb, nu, blk, nv: (blk[0], 0, 0)),
            smem_spec(lambda i, ea, eb, nu, blk, nv: (blk[jnp.minimum(i + 1, n_tiles - 1)], 0, 0)),
            smem_spec(lambda i, ea, eb, nu, blk, nv: (blk[i], 0, 0)),
            pl.BlockSpec(memory_space=pl.ANY),
            wspec(gu_shape, lambda i, ea, eb, nu, blk, nv: (0, 0, 0)),
            wspec(dn_shape, lambda i, ea, eb, nu, blk, nv: (0, 0, 0)),
            wspec(gu_shape, lambda i, ea, eb, nu, blk, nv: (ea[i] + 1, 0, 0)),
            wspec(dn_shape, lambda i, ea, eb, nu, blk, nv: (ea[i] + 1, 0, 0)),
            wspec(gu_shape, lambda i, ea, eb, nu, blk, nv: (eb[i] + 1, 0, 0)),
            wspec(dn_shape, lambda i, ea, eb, nu, blk, nv: (eb[i] + 1, 0, 0)),
        ],
        out_specs=pl.BlockSpec(memory_space=pl.ANY),
        scratch_shapes=[pltpu.VMEM((2, t, F_EXT_W), F32), pltpu.VMEM((2, t, D_MODEL), F32),
                        pltpu.SemaphoreType.DMA((2,)), pltpu.SemaphoreType.DMA((2,))],
    )
    return pl.pallas_call(
        functools.partial(_moe_sparse_kernel, n_rows=n_rows),
        grid_spec=grid_spec,
        out_shape=jax.ShapeDtypeStruct((n_rows + 2 * t, D_MODEL), F32),
        compiler_params=pltpu.CompilerParams(
            dimension_semantics=("arbitrary",), vmem_limit_bytes=VMEM_LIMIT),
        name="moe_sorted",
    )(ea, eb, n_used, blk, n_valid, src, src, src, f_ext, w_gu, w_down, w_gu, w_down, w_gu, w_down)


def _residual_kernel(x_ref, y_ref, mod_ref, o_ref):
    o_ref[...] = x_ref[...] + mod_ref[5:6, :] * y_ref[...]


def _residual_call(x_flat, y, mods, n_batch, seq, out_rows):
    tm = MOE_TILE
    mod_map, _ = _row_class_maps(n_batch * seq // tm, seq // tm, n_batch)
    row_spec = pl.BlockSpec((tm, D_MODEL), lambda i: (i, 0))
    return pl.pallas_call(
        _residual_kernel,
        grid=(out_rows // tm,),
        in_specs=[row_spec, row_spec, pl.BlockSpec((None, N_MOD, D_MODEL), mod_map)],
        out_specs=row_spec,
        out_shape=jax.ShapeDtypeStruct((out_rows, D_MODEL), F32),
        compiler_params=pltpu.CompilerParams(
            dimension_semantics=("parallel",), vmem_limit_bytes=VMEM_LIMIT),
        name="ffn_residual",
    )(x_flat, y, mods)


def _axial_angles(seq, d_rope):
    d_axis = d_rope // 2
    inv = ROPE_BASE ** (-jnp.arange(0, d_axis, 2, dtype=F32) / d_axis)
    t = jnp.arange(seq, dtype=jnp.int32)
    r = (t // GRID_W).astype(F32)
    col = (t % GRID_W).astype(F32)
    ang_r = r[:, None] * inv
    ang_c = col[:, None] * inv
    cos = jnp.concatenate([jnp.cos(ang_r)] * 2 + [jnp.cos(ang_c)] * 2, axis=1)
    sin = jnp.concatenate([jnp.sin(ang_r)] * 2 + [jnp.sin(ang_c)] * 2, axis=1)
    nf = d_rope // 4
    first = jnp.concatenate([jnp.ones((nf,), F32), jnp.zeros((nf,), F32)] * 2)
    return cos, -sin * first, sin * (1.0 - first)


def _rope_table(seq, tile):
    cm, sma, smb = _axial_angles(seq, MLA_ROPE)
    pad_l = jnp.zeros((seq, MLA_NOPE), F32)
    pad_r = jnp.zeros((seq, HEAD_PAD - MLA_NOPE - MLA_ROPE), F32)
    cos_m = jnp.concatenate([pad_l + 1.0, cm, pad_r + 1.0], axis=1)
    sin_ma = jnp.concatenate([pad_l, sma, pad_r], axis=1)
    sin_mb = jnp.concatenate([pad_l, smb, pad_r], axis=1)
    cd, sda, sdb = _axial_angles(seq, DIFF_DH)
    tab = jnp.concatenate([cos_m, sin_ma, sin_mb] + [jnp.concatenate([a, a], axis=1) for a in (cd, sda, sdb)],
                          axis=1)
    ident = jnp.concatenate([jnp.ones((tile, LANES), F32), jnp.zeros((tile, 2 * LANES), F32)] * 2, axis=1)
    return jnp.concatenate([tab, ident], axis=0)


def _block_diag_mean(seg_sizes):
    m = jnp.zeros((MXU_DIM, MXU_DIM), F32)
    off = 0
    for size, active in seg_sizes:
        if active:
            m = m.at[off:off + size, off:off + size].set(1.0 / size)
        off += size
    assert off == MXU_DIM
    return m.astype(BF16)


def _prep_weights(attn_norm_g, ffn_norm_g, w_in, mla_q_norm_g, mla_w_uq, mla_kv_norm_g, mla_w_ukv, mla_q_g,
                  mla_k_g, diff_q_g, diff_k_g, diff_sub_g, w_o_mla, w_o_diff, w_out, moe_w_gate, moe_w_up,
                  moe_w_down, sh_w_gate, sh_w_up, sh_w_down):
    depth = w_in.shape[0]
    d = D_MODEL
    z = lambda w: jnp.zeros((depth, d, w), F32)
    off_kr = MLA_KV_RANK
    off_kd = off_kr + MLA_ROPE
    off_vd = off_kd + DIFF_QK_W
    off_cq = off_vd + DIFF_HEADS * DIFF_V
    off_qd = off_cq + MLA_Q_RANK
    off_gate = off_qd + DIFF_QK_W
    w_in_a = jnp.concatenate([
        w_in[:, :, :off_kr], z(MLA_NOPE), w_in[:, :, off_kr:off_kd], z(HEAD_PAD - MLA_NOPE - MLA_ROPE),
        w_in[:, :, off_kd:off_gate]], axis=2).astype(BF16)
    w_gate = w_in[:, :, off_gate:].astype(BF16)

    ukv = mla_w_ukv.reshape(depth, MLA_KV_RANK, MLA_HEADS, MLA_NOPE + MLA_V)
    pad_k = jnp.zeros((depth, MLA_KV_RANK, MLA_HEADS, HEAD_PAD - MLA_NOPE), F32)
    pad_v = jnp.zeros((depth, MLA_KV_RANK, MLA_HEADS, HEAD_PAD - MLA_V), F32)
    w_ukv = jnp.concatenate([
        jnp.concatenate([ukv[..., :MLA_NOPE], pad_k], axis=-1).reshape(depth, MLA_KV_RANK, MLA_W),
        jnp.concatenate([ukv[..., MLA_NOPE:], pad_v], axis=-1).reshape(depth, MLA_KV_RANK, MLA_W)],
        axis=-1).astype(BF16)
    uq = mla_w_uq.reshape(depth, MLA_Q_RANK, MLA_HEADS, MLA_NOPE + MLA_ROPE)
    w_uq = jnp.pad(uq, ((0, 0), (0, 0), (0, 0), (0, HEAD_PAD - MLA_NOPE - MLA_ROPE))).reshape(
        depth, MLA_Q_RANK, MLA_W).astype(BF16)
    wom = w_o_mla.reshape(depth, MLA_HEADS, MLA_V, d)
    w_o_mla_p = jnp.pad(wom, ((0, 0), (0, 0), (0, HEAD_PAD - MLA_V), (0, 0))).reshape(depth, MLA_W, d).astype(BF16)

    def per_head(g96, scale):
        g = jnp.pad(g96 * scale, ((0, 0), (0, HEAD_PAD - MLA_NOPE - MLA_ROPE)))
        return jnp.tile(g, (1, MLA_HEADS)).reshape(depth, 1, MLA_W)

    k_nope_only = jnp.concatenate([mla_k_g[:, :MLA_NOPE], jnp.zeros((depth, MLA_ROPE), F32)], axis=1)
    g_k = per_head(k_nope_only, 1.0)
    g_kr = jnp.concatenate([jnp.zeros((depth, MLA_NOPE), F32), mla_k_g[:, MLA_NOPE:],
                            jnp.zeros((depth, HEAD_PAD - MLA_NOPE - MLA_ROPE), F32)], axis=1).reshape(depth, 1, LANES)
    g_q = per_head(mla_q_g, MLA_SCALE * LOG2E)
    g_kd = jnp.tile(diff_k_g.reshape(depth, 2 * DIFF_DH), (1, DIFF_HEADS)).reshape(depth, 1, DIFF_QK_W)
    g_qd = jnp.tile(diff_q_g.reshape(depth, 2 * DIFF_DH) * (DIFF_SCALE * LOG2E),
                    (1, DIFF_HEADS)).reshape(depth, 1, DIFF_QK_W)
    lam_inits = [0.8 - 0.6 * math.exp(-0.3 * l) for l in range(depth)]
    g_sub = diff_sub_g * (1.0 - jnp.asarray(lam_inits, F32))[:, None]

    v_one = jnp.tile((jnp.arange(HEAD_PAD) == MLA_V).astype(F32), MLA_HEADS).reshape(1, MLA_W)
    b_k = _block_diag_mean([(MLA_NOPE, True), (HEAD_PAD - MLA_NOPE, False)] * 2)
    b_q = _block_diag_mean([(MLA_NOPE, True), (MLA_ROPE, True), (HEAD_PAD - MLA_NOPE - MLA_ROPE, False)] * 2)
    b_d = _block_diag_mean([(DIFF_DH, True)] * 4)

    w_gu = jnp.concatenate([
        jnp.concatenate([sh_w_gate, sh_w_up], axis=-1)[:, None].astype(BF16),
        jnp.concatenate([moe_w_gate.astype(BF16), moe_w_up.astype(BF16)], axis=-1)], axis=1)
    w_dn = jnp.concatenate([sh_w_down[:, None].astype(BF16), moe_w_down.astype(BF16)], axis=1)

    layers = []
    for l in range(depth):
        layers.append({
            "g_attn": attn_norm_g[l].reshape(1, d), "g_ffn": ffn_norm_g[l].reshape(1, d),
            "w_in_a": w_in_a[l], "w_gate": w_gate[l], "w_ukv": w_ukv[l], "w_uq": w_uq[l],
            "g_kv": mla_kv_norm_g[l].reshape(1, MLA_KV_RANK), "g_cq": mla_q_norm_g[l].reshape(1, MLA_Q_RANK),
            "g_k": g_k[l], "g_kr": g_kr[l], "g_q": g_q[l], "g_kd": g_kd[l], "g_qd": g_qd[l],
            "g_sub": g_sub[l].reshape(1, DIFF_V), "b_k": b_k, "b_q": b_q, "b_d": b_d, "v_one": v_one,
            "w_o_mla": w_o_mla_p[l], "w_o_diff": w_o_diff[l].astype(BF16), "w_out": w_out[l].astype(BF16),
            "w_gu": w_gu[l], "w_dn": w_dn[l], "lam_init": lam_inits[l],
        })
    return layers


def kernel(x, c, ctx, c_ctx, ada_w, ada_b, attn_norm_g, ffn_norm_g, w_in, mla_q_norm_g, mla_w_uq, mla_kv_norm_g, mla_w_ukv, mla_q_g, mla_k_g, diff_q_g, diff_k_g, diff_lam, diff_sub_g, w_o_mla, w_o_diff, w_out, router_w, router_b, moe_w_gate, moe_w_up, moe_w_down, sh_w_gate, sh_w_up, sh_w_down):
    n_batch, seq, d = x.shape
    n_ctx = ctx.shape[1]
    depth = w_in.shape[0]
    assert d == D_MODEL and n_ctx == ROW_TILE and seq % GRID_W == 0
    assert seq % MLA_TQ == 0 and seq % DIFF_TQ == 0 and ((seq + n_ctx) % KV_CHUNK) % 16 == 0
    assert (n_batch * seq) % MOE_TILE == 0 and (n_batch * n_ctx) % MOE_TILE == 0
    lat_rows = n_batch * seq
    all_rows = lat_rows + n_batch * n_ctx

    layers = _prep_weights(attn_norm_g, ffn_norm_g, w_in, mla_q_norm_g, mla_w_uq, mla_kv_norm_g, mla_w_ukv,
                           mla_q_g, mla_k_g, diff_q_g, diff_k_g, diff_sub_g, w_o_mla, w_o_diff, w_out,
                           moe_w_gate, moe_w_up, moe_w_down, sh_w_gate, sh_w_up, sh_w_down)
    rope_tab = _rope_table(seq, ROW_TILE)
    rw = jnp.pad(router_w, ((0, 0), (1, LANES - 1 - N_EXPERTS)))
    rb = jnp.pad(router_b, (1, LANES - 1 - N_EXPERTS)).reshape(1, LANES)

    pad_rows = (-(n_batch + 1)) % 8
    c_all = jnp.concatenate([c, c_ctx[None, :], jnp.zeros((pad_rows, d), F32)], axis=0)
    mods_all = _mods_call(c_all, ada_w, ada_b).reshape(depth, c_all.shape[0], N_MOD, d)

    xf = jnp.concatenate([x.reshape(lat_rows, d), ctx.reshape(n_batch * n_ctx, d)], axis=0)
    for l in range(depth):
        wl = layers[l]
        mods = mods_all[l]
        k_m, v_m, q_m, k_d, v_d, q_d = _proj_call(xf, mods, wl, rope_tab, n_batch, seq)
        last = l == depth - 1
        o_m = _attn_call(q_m, k_m, v_m, n_batch, seq, n_ctx, MLA_HEADS, 1, [], 0.0, MLA_TQ, not last, "attn_mla")
        o_d = _attn_call(q_d, k_d, v_d, n_batch, seq, n_ctx, DIFF_HEADS, 2, [diff_lam[l], wl["g_sub"]],
                         wl["lam_init"], DIFF_TQ, not last, "attn_diff")
        out_rows = lat_rows if last else all_rows
        xf, f_ext, pos, counts = _merge_call(xf, mods, o_m, o_d, wl, rw, rb, n_batch, seq, out_rows)
        y = _moe_sparse_call(f_ext, _moe_plan(pos, counts, out_rows), wl["w_gu"], wl["w_dn"], out_rows)
        xf = _residual_call(xf, y, mods, n_batch, seq, out_rows)
    return xf.reshape(n_batch, seq, d)
```

```python
import functools
import math

import jax
import jax.numpy as jnp
from jax import lax
from jax.experimental import pallas as pl
from jax.experimental.pallas import tpu as pltpu

F32 = jnp.float32
BF16 = jnp.bfloat16

D_MODEL = 1024
GRID_W = 64
EPS = 1e-6
ROPE_BASE = 10000.0
N_MOD = 6

MLA_HEADS = 8
MLA_NOPE = 64
MLA_ROPE = 32
MLA_V = 64
MLA_Q_RANK = 512
MLA_KV_RANK = 256
MLA_SCALE = 1.0 / math.sqrt(MLA_NOPE + MLA_ROPE)

DIFF_HEADS = 4
DIFF_DH = 64
DIFF_V = 2 * DIFF_DH
DIFF_SCALE = 1.0 / math.sqrt(DIFF_DH)

N_EXPERTS = 16
EXPERTS_PER_GROUP = 4
N_GROUPS = N_EXPERTS // EXPERTS_PER_GROUP
D_EXPERT = 512
PAIR_LO = (0, 0, 0, 1, 1, 2)
PAIR_HI = (1, 2, 3, 2, 3, 3)
N_PAIRS = len(PAIR_LO)
N_CLASSES = N_GROUPS * N_PAIRS

LANES = 128
MXU_DIM = 256
HEAD_PAD = LANES
MLA_W = MLA_HEADS * HEAD_PAD
DIFF_QK_W = DIFF_HEADS * 2 * DIFF_DH
DIFF_K_W = DIFF_HEADS * 2 * LANES
DIFF_V_PAD = 2 * LANES
DIFF_V_W = DIFF_HEADS * DIFF_V_PAD

A_CKV = 0
A_KR = A_CKV + MLA_KV_RANK
A_KD = A_KR + LANES
A_VD = A_KD + DIFF_QK_W
A_CQ = A_VD + DIFF_HEADS * DIFF_V
A_QD = A_CQ + MLA_Q_RANK
A_COLS = A_QD + DIFF_QK_W

ROPE_TAB_W = 6 * LANES
F_EXT_W = D_MODEL + LANES

ROW_TILE = 256
MOE_TILE = 512
SORT_TILE = 256
KV_CHUNK = 768
MLA_TQ = 512
DIFF_TQ = 256
LOOP_STAGES = 4
LOG2E = math.log2(math.e)
VMEM_LIMIT = 56 * 1024 * 1024


def _sigmoid(x):
    return 1.0 / (1.0 + jnp.exp(-x))


def _dot(a, b):
    return jnp.dot(a, b, preferred_element_type=F32)


def _split_bf16(a):
    hi = a.astype(BF16)
    lo = (a - hi.astype(F32)).astype(BF16)
    return hi, lo


def _dot3(a, b):
    a_hi, a_lo = _split_bf16(a)
    b_hi, b_lo = _split_bf16(b)
    return _dot(a_hi, b_hi) + _dot(a_lo, b_hi) + _dot(a_hi, b_lo)


def _mods_kernel(c_ref, w_ref, b_ref, o_ref):
    c = c_ref[...]
    a = c * _sigmoid(c)
    o_ref[...] = _dot3(a, w_ref[...]) + b_ref[...]


def _mods_call(c_all, ada_w, ada_b):
    depth, d, n = ada_w.shape
    tn = 1536
    rows = c_all.shape[0]
    return pl.pallas_call(
        _mods_kernel,
        grid=(depth, n // tn),
        in_specs=[
            pl.BlockSpec((rows, d), lambda l, j: (0, 0)),
            pl.BlockSpec((None, d, tn), lambda l, j: (l, 0, j)),
            pl.BlockSpec((None, 1, tn), lambda l, j: (l, 0, j)),
        ],
        out_specs=pl.BlockSpec((None, rows, tn), lambda l, j: (l, 0, j)),
        out_shape=jax.ShapeDtypeStruct((depth, rows, n), F32),
        compiler_params=pltpu.CompilerParams(
            dimension_semantics=("parallel", "parallel"), vmem_limit_bytes=VMEM_LIMIT),
        name="adaln_mods",
    )(c_all, ada_w, ada_b.reshape(depth, 1, n))


def _rms(x, g):
    return x * lax.rsqrt(jnp.mean(x * x, axis=-1, keepdims=True) + EPS) * g


def _seg_mean_sq(x, b_ref):
    sq = (x * x).astype(BF16)
    b = b_ref[...]
    parts = [_dot(sq[:, c:c + MXU_DIM], b) for c in range(0, x.shape[1], MXU_DIM)]
    return parts[0] if len(parts) == 1 else jnp.concatenate(parts, axis=1)


def _rope(x, cos, sin_a, sin_b, half):
    return (x * cos + pltpu.roll(x, LANES - half, axis=1) * sin_a
            + pltpu.roll(x, half, axis=1) * sin_b)


def _proj_kernel(x_ref, mod_ref, g_ref, win_ref, wukv_ref, wuq_ref, gkv_ref, gcq_ref, gk_ref, gkr_ref,
                 gq_ref, gkd_ref, gqd_ref, bk_ref, bq_ref, bd_ref, vone_ref, qaug_ref, qdaug_ref, rope_ref,
                 kmla_ref, vmla_ref, qmla_ref, kd_ref, vd_ref, qd_ref):
    x = x_ref[...]
    h = _rms(x, g_ref[...]) * (1.0 + mod_ref[1:2, :]) + mod_ref[0:1, :]
    p = _dot(h.astype(BF16), win_ref[...])

    cos_m = rope_ref[:, 0 * LANES:1 * LANES]
    sin_ma = rope_ref[:, 1 * LANES:2 * LANES]
    sin_mb = rope_ref[:, 2 * LANES:3 * LANES]
    cos_d = rope_ref[:, 3 * LANES:4 * LANES]
    sin_da = rope_ref[:, 4 * LANES:5 * LANES]
    sin_db = rope_ref[:, 5 * LANES:6 * LANES]

    ckv = _rms(p[:, A_CKV:A_KR], gkv_ref[...])
    kv = _dot(ckv.astype(BF16), wukv_ref[...])
    kn = kv[:, :MLA_W]
    kn = kn * lax.rsqrt(_seg_mean_sq(kn, bk_ref) + EPS) * gk_ref[...]
    kr = p[:, A_KR:A_KD]
    kr = kr * lax.rsqrt(jnp.sum(kr * kr, axis=-1, keepdims=True) * (1.0 / MLA_ROPE) + EPS) * gkr_ref[...]
    kr = _rope(kr, cos_m, sin_ma, sin_mb, MLA_ROPE // 4)
    tm = x.shape[0]
    lane_blk = lax.broadcasted_iota(jnp.int32, (tm, LANES), 1)
    kr = kr + (lane_blk == MLA_NOPE + MLA_ROPE).astype(F32)
    for hd in range(MLA_HEADS):
        sl = slice(hd * HEAD_PAD, (hd + 1) * HEAD_PAD)
        kmla_ref[:, sl] = (kn[:, sl] + kr).astype(BF16)
    vmla_ref[...] = (kv[:, MLA_W:] + vone_ref[...]).astype(BF16)

    cq = _rms(p[:, A_CQ:A_QD], gcq_ref[...])
    q = _dot(cq.astype(BF16), wuq_ref[...])
    q = q * lax.rsqrt(_seg_mean_sq(q, bq_ref) + EPS) * gq_ref[...]
    for hd in range(MLA_HEADS):
        sl = slice(hd * HEAD_PAD, (hd + 1) * HEAD_PAD)
        qmla_ref[:, sl] = (_rope(q[:, sl], cos_m, sin_ma, sin_mb, MLA_ROPE // 4) + qaug_ref[:, sl]).astype(BF16)

    kd = p[:, A_KD:A_VD]
    kd = kd * lax.rsqrt(_seg_mean_sq(kd, bd_ref) + EPS) * gkd_ref[...]
    qd = p[:, A_QD:A_COLS]
    qd = qd * lax.rsqrt(_seg_mean_sq(qd, bd_ref) + EPS) * gqd_ref[...]
    ones_blk = (lane_blk == 0).astype(BF16)
    two_ones_blk = (lane_blk < 2).astype(BF16)
    qd_shift = jnp.broadcast_to(qdaug_ref[...], (tm, LANES)).astype(BF16)
    for hd in range(DIFF_HEADS):
        sl = slice(hd * LANES, (hd + 1) * LANES)
        lo = slice(2 * hd * LANES, (2 * hd + 1) * LANES)
        hi = slice((2 * hd + 1) * LANES, (2 * hd + 2) * LANES)
        kd_ref[:, lo] = _rope(kd[:, sl], cos_d, sin_da, sin_db, DIFF_DH // 4).astype(BF16)
        kd_ref[:, hi] = two_ones_blk
        qd_ref[:, lo] = _rope(qd[:, sl], cos_d, sin_da, sin_db, DIFF_DH // 4).astype(BF16)
        qd_ref[:, hi] = qd_shift

    vd = p[:, A_VD:A_CQ]
    for hd in range(DIFF_HEADS):
        vd_ref[:, hd * DIFF_V_PAD:hd * DIFF_V_PAD + DIFF_V] = vd[:, hd * DIFF_V:(hd + 1) * DIFF_V].astype(BF16)
        vd_ref[:, hd * DIFF_V_PAD + DIFF_V:(hd + 1) * DIFF_V_PAD] = ones_blk


def _row_class_maps(n_lat_tiles, tiles_per_batch, n_batch):
    def mod_map(i):
        return (jnp.where(i < n_lat_tiles, i // tiles_per_batch, n_batch), 0, 0)

    def rope_map(i):
        return (jnp.where(i < n_lat_tiles, i % tiles_per_batch, tiles_per_batch), 0)

    return mod_map, rope_map


def _const_spec(shape):
    return pl.BlockSpec(shape, lambda i: tuple(0 for _ in shape))


def _proj_call(x_flat, mods, wl, rope_tab, n_batch, seq):
    rows = x_flat.shape[0]
    tm = ROW_TILE
    n_lat_tiles = n_batch * seq // tm
    mod_map, rope_map = _row_class_maps(n_lat_tiles, seq // tm, n_batch)
    row_spec = lambda w: pl.BlockSpec((tm, w), lambda i: (i, 0))
    tiles_per_batch = seq // tm

    def kv_map(i):
        lat = i + i // tiles_per_batch + 1
        ctx = (i - n_lat_tiles) * (tiles_per_batch + 1)
        return (jnp.where(i < n_lat_tiles, lat, ctx), 0)

    kv_spec = lambda w: pl.BlockSpec((tm, w), kv_map)
    consts = [wl["g_attn"], wl["w_in_a"], wl["w_ukv"], wl["w_uq"], wl["g_kv"], wl["g_cq"], wl["g_k"], wl["g_kr"],
              wl["g_q"], wl["g_kd"], wl["g_qd"], wl["b_k"], wl["b_q"], wl["b_d"], wl["v_one"], wl["q_aug"],
              wl["qd_aug"]]
    out_w = [MLA_W, MLA_W, MLA_W, DIFF_K_W, DIFF_V_W, DIFF_K_W]
    out_specs = [kv_spec(MLA_W), kv_spec(MLA_W), row_spec(MLA_W), kv_spec(DIFF_K_W), kv_spec(DIFF_V_W),
                 row_spec(DIFF_K_W)]
    return pl.pallas_call(
        _proj_kernel,
        grid=(rows // tm,),
        in_specs=[row_spec(D_MODEL), pl.BlockSpec((None, N_MOD, D_MODEL), mod_map)]
        + [_const_spec(c.shape) for c in consts]
        + [pl.BlockSpec((tm, ROPE_TAB_W), rope_map)],
        out_specs=out_specs,
        out_shape=[jax.ShapeDtypeStruct((rows, w), BF16) for w in out_w],
        compiler_params=pltpu.CompilerParams(
            dimension_semantics=("parallel",), vmem_limit_bytes=VMEM_LIMIT),
        name="proj_qkv",
    )(x_flat, mods, *consts, rope_tab)


SAFE_SHIFT = 50.0
SHIFT_MARGIN = 1.02


def _attn_kernel(flag_ref, *refs, n_maps, tq, tk, chunks, lam_init, aliased_out):
    refs = list(refs)
    q_ref, k_ref, v_ref = refs[:3]
    n_in = 3 + (2 if n_maps == 2 else 0) + (1 if aliased_out else 0)
    o_ref = refs[n_in]
    if n_maps == 2:
        lam_ref, subg_ref = refs[3:5]
        qs_ref, s_ref, p_ref, m_ref, a_ref, acc_ref = refs[n_in + 1:]
        qf = q_ref[...].astype(F32)
        lane = lax.broadcasted_iota(jnp.int32, qf.shape, 1)
        keep0 = (lane < DIFF_DH) | (lane == 2 * DIFF_DH)
        keep1 = ((lane >= DIFF_DH) & (lane < 2 * DIFF_DH)) | (lane == 2 * DIFF_DH + 1)
        qs_ref[0:tq, :] = jnp.where(keep0, qf, 0.0).astype(BF16)
        qs_ref[tq:2 * tq, :] = jnp.where(keep1, qf, 0.0).astype(BF16)
        q_src = qs_ref
    else:
        s_ref, p_ref, m_ref, a_ref, acc_ref = refs[n_in + 1:]
        q_src = q_ref
    vw = v_ref.shape[1]
    sum_col = vw // 2

    head, n_mid, tail = chunks
    n_chunks = 1 + n_mid + (1 if tail else 0)

    def geom(j):
        if isinstance(j, int):
            if j == 0:
                return 0, head
            if j <= n_mid:
                return head + (j - 1) * tk, tk
            return head + n_mid * tk, tail
        return pl.multiple_of(head + (j - 1) * tk, math.gcd(head, tk)), tk

    def rows_of(ref, j):
        off, size = geom(j)
        return ref[pl.ds(off, size), :]

    def scores(j, slot):
        size = geom(j)[1]
        s_ref[slot, :, :size] = lax.dot_general(q_src[...], rows_of(k_ref, j), (((1,), (1,)), ((), ())),
                                                preferred_element_type=F32)

    def probs_shifted(j, slot):
        size = geom(j)[1]
        p_ref[slot, :, :size] = jnp.exp2(s_ref[slot, :, :size]).astype(BF16)

    def accumulate_plain(j, slot):
        size = geom(j)[1]
        r = _dot(p_ref[slot, :, :size], rows_of(v_ref, j))
        if isinstance(j, int) and j == 0:
            acc_ref[...] = r
        else:
            acc_ref[...] += r

    def probs_online(j, slot):
        size = geom(j)[1]
        s = s_ref[slot, :, :size]
        mx = jnp.max(s, axis=-1, keepdims=True)
        if isinstance(j, int) and j == 0:
            m_new = mx
        else:
            m_old = m_ref[...]
            m_new = jnp.maximum(m_old, mx)
            a_ref[slot] = jnp.exp2(m_old - m_new)
        m_ref[...] = m_new
        p_ref[slot, :, :size] = jnp.exp2(s - m_new).astype(BF16)

    def accumulate_online(j, slot):
        size = geom(j)[1]
        r = _dot(p_ref[slot, :, :size], rows_of(v_ref, j))
        if isinstance(j, int) and j == 0:
            acc_ref[...] = r
        else:
            acc_ref[...] = acc_ref[...] * a_ref[slot] + r

    def sweep(probs, accumulate):
        def stage(j, slot, with_scores):
            probs(j, slot)
            accumulate(j - 1, 1 - slot)
            if with_scores:
                scores(j + 1, 1 - slot)

        scores(0, 0)
        if n_chunks > 1:
            scores(1, 1)
        probs(0, 0)
        if n_chunks > 1:
            stage(1, 1, n_chunks > 2)
            n_loop = max(n_mid - 2, 0) // LOOP_STAGES

            def group(i, carry):
                j0 = 2 + LOOP_STAGES * i
                for t in range(LOOP_STAGES):
                    stage(j0 + t, t % 2, True)
                return carry

            lax.fori_loop(0, n_loop, group, 0)
            for j in range(2 + LOOP_STAGES * n_loop, n_chunks):
                stage(j, j % 2, j + 1 < n_chunks)
        accumulate(n_chunks - 1, (n_chunks - 1) % 2)

    @pl.when(flag_ref[0] == 1)
    def _():
        sweep(probs_shifted, accumulate_plain)

    @pl.when(flag_ref[0] != 1)
    def _():
        sweep(probs_online, accumulate_online)

    acc = acc_ref[...]
    inv_l = 1.0 / acc[:, sum_col:sum_col + 1]
    if n_maps == 1:
        o_ref[...] = (acc * inv_l).astype(o_ref.dtype)
    else:
        lf = lam_ref[...]
        lam = (jnp.exp(jnp.sum(lf[0:1, :] * lf[1:2, :], axis=-1, keepdims=True))
               - jnp.exp(jnp.sum(lf[2:3, :] * lf[3:4, :], axis=-1, keepdims=True)) + lam_init)
        o = acc[:, :DIFF_V] * inv_l
        o = o[:tq] - lam * o[tq:]
        o_ref[...] = _rms(o, subg_ref[...]).astype(o_ref.dtype)


def _attn_scratch(n_maps, tq, tk, kw, vw):
    r = n_maps * tq
    qs = [pltpu.VMEM((r, kw), BF16)] if n_maps == 2 else []
    return qs + [pltpu.VMEM((2, r, tk), F32), pltpu.VMEM((2, r, tk), BF16), pltpu.VMEM((r, 1), F32),
                 pltpu.VMEM((2, r, 1), F32), pltpu.VMEM((r, vw), F32)]


def _attn_call(flag, q, k, v, n_batch, seq, n_ctx, n_heads, n_maps, extra, lam_init, tq, with_ctx_queries, name):
    rows = q.shape[0]
    vw = v.shape[1] // n_heads
    kw = k.shape[1] // n_heads
    qw = q.shape[1] // n_heads
    keys = seq + n_ctx
    tk = KV_CHUNK
    nq = seq // tq
    assert seq % tq == 0
    chunks = (tk, keys // tk - 1, keys % tk)
    ctx_blk0 = n_batch * seq // n_ctx
    blocks_per_batch = keys // n_ctx
    extra_specs = [pl.BlockSpec(e.shape, lambda *_: (0, 0)) for e in extra]
    out_shape = jax.ShapeDtypeStruct((rows, n_heads * LANES), BF16)

    kern = functools.partial(_attn_kernel, n_maps=n_maps, tq=tq, tk=tk, chunks=chunks, lam_init=lam_init,
                             aliased_out=False)
    out = pl.pallas_call(
        kern,
        grid_spec=pltpu.PrefetchScalarGridSpec(
            num_scalar_prefetch=1,
            grid=(n_batch, n_heads, nq),
            in_specs=[pl.BlockSpec((tq, qw), lambda b, h, qi, f: (b * nq + qi, h)),
                      pl.BlockSpec((keys, kw), lambda b, h, qi, f: (b, h)),
                      pl.BlockSpec((keys, vw), lambda b, h, qi, f: (b, h))] + extra_specs,
            out_specs=pl.BlockSpec((tq, LANES), lambda b, h, qi, f: (b * nq + qi, h)),
            scratch_shapes=_attn_scratch(n_maps, tq, tk, kw, vw)),
        out_shape=out_shape,
        compiler_params=pltpu.CompilerParams(
            dimension_semantics=("parallel", "parallel", "arbitrary"), vmem_limit_bytes=VMEM_LIMIT),
        name=name,
    )(flag, q, k, v, *extra)
    if not with_ctx_queries:
        return out

    kern_c = functools.partial(_attn_kernel, n_maps=n_maps, tq=n_ctx, tk=n_ctx, chunks=(n_ctx, 0, 0),
                               lam_init=lam_init, aliased_out=True)
    return pl.pallas_call(
        kern_c,
        grid_spec=pltpu.PrefetchScalarGridSpec(
            num_scalar_prefetch=1,
            grid=(n_batch, n_heads),
            in_specs=[pl.BlockSpec((n_ctx, qw), lambda b, h, f: (ctx_blk0 + b, h)),
                      pl.BlockSpec((n_ctx, kw), lambda b, h, f: (b * blocks_per_batch, h)),
                      pl.BlockSpec((n_ctx, vw), lambda b, h, f: (b * blocks_per_batch, h))] + extra_specs
            + [pl.BlockSpec(memory_space=pl.ANY)],
            out_specs=pl.BlockSpec((n_ctx, LANES), lambda b, h, f: (ctx_blk0 + b, h)),
            scratch_shapes=_attn_scratch(n_maps, n_ctx, n_ctx, kw, vw)),
        out_shape=out_shape,
        input_output_aliases={4 + len(extra): 0},
        compiler_params=pltpu.CompilerParams(
            dimension_semantics=("parallel", "parallel"), vmem_limit_bytes=VMEM_LIMIT),
        name=name + "_ctx",
    )(flag, q, k, v, *extra, out)


def _route(logits, bias):
    lane = lax.broadcasted_iota(jnp.int32, logits.shape, 1)
    is_exp = (lane >= 1) & (lane <= N_EXPERTS)
    grp = (lane - 1) >> 2
    neg = jnp.float32(-jnp.inf)
    big = jnp.int32(1 << 20)
    aff = _sigmoid(logits)
    sel = jnp.where(is_exp, aff + bias, neg)

    def first_argmax(v):
        mx = jnp.max(v, axis=-1, keepdims=True)
        ix = jnp.min(jnp.where(v == mx, lane, big), axis=-1, keepdims=True)
        return mx, ix

    best = None
    for g in range(N_GROUPS):
        vg = jnp.where(grp == g, sel, neg)
        m1, i1 = first_argmax(vg)
        m2, i2 = first_argmax(jnp.where(lane == i1, neg, vg))
        score = m1 + m2
        if best is None:
            best = (score, i1, i2)
        else:
            upd = score > best[0]
            best = (jnp.where(upd, score, best[0]), jnp.where(upd, i1, best[1]), jnp.where(upd, i2, best[2]))
    _, i1, i2 = best
    a1 = jnp.sum(jnp.where(lane == i1, aff, 0.0), axis=-1, keepdims=True)
    a2 = jnp.sum(jnp.where(lane == i2, aff, 0.0), axis=-1, keepdims=True)
    tot = a1 + a2
    e1 = i1 - 1
    e2 = i2 - 1
    swap = e2 < e1
    lo = jnp.where(swap, e2, e1)
    hi = jnp.where(swap, e1, e2)
    w_lo = jnp.where(swap, a2, a1) / tot
    w_hi = jnp.where(swap, a1, a2) / tot
    l = lo & (EXPERTS_PER_GROUP - 1)
    h = hi & (EXPERTS_PER_GROUP - 1)
    cls = (lo >> 2) * N_PAIRS + ((l * (7 - l)) >> 1) + (h - l - 1)
    return cls, w_lo, w_hi


def _merge_kernel(x_ref, mod_ref, ga_ref, gf_ref, om_ref, od_ref, wgate_ref, wom_ref, wod_ref, wout_ref,
                  rw_ref, rb_ref, tri_ref, xo_ref, f_ref, pos_ref, cnt_ref, *, cap):
    @pl.when(pl.program_id(0) == 0)
    def _():
        cnt_ref[...] = jnp.zeros_like(cnt_ref)

    x = x_ref[...]
    h = _rms(x, ga_ref[...]) * (1.0 + mod_ref[1:2, :]) + mod_ref[0:1, :]
    gates = _sigmoid(_dot(h.astype(BF16), wgate_ref[...]))
    y = (gates[:, :D_MODEL] * _dot(om_ref[...], wom_ref[...])
         + gates[:, D_MODEL:] * _dot(od_ref[...], wod_ref[...]))
    xn = x + mod_ref[2:3, :] * _dot(y.astype(BF16), wout_ref[...])
    xo_ref[...] = xn
    f = _rms(xn, gf_ref[...]) * (1.0 + mod_ref[4:5, :]) + mod_ref[3:4, :]
    cls, w_lo, w_hi = _route(_dot3(f, rw_ref[...]), rb_ref[...])
    lane = lax.broadcasted_iota(jnp.int32, (x.shape[0], LANES), 1)
    f_ref[:, :D_MODEL] = f
    f_ref[:, D_MODEL:] = jnp.where(lane == 0, w_lo, jnp.where(lane == 1, w_hi, 0.0))
    onehot = (lane == cls).astype(F32)
    before = _dot(tri_ref[...], onehot.astype(BF16)) + cnt_ref[...]
    rank = jnp.sum(onehot * before, axis=-1, keepdims=True)
    cnt_ref[...] += jnp.sum(onehot, axis=0, keepdims=True)
    pos = cls.astype(F32) * float(cap) + rank
    pos_ref[...] = jnp.where(lane == 0, pos, 0.0)


def _merge_call(x_flat, mods, o_mla, o_diff, wl, router_w, router_b, n_batch, seq, rows):
    tm = ROW_TILE
    mod_map, _ = _row_class_maps(n_batch * seq // tm, seq // tm, n_batch)
    row_spec = lambda w: pl.BlockSpec((tm, w), lambda i: (i, 0))
    tri = (jnp.arange(tm)[:, None] > jnp.arange(tm)[None, :]).astype(BF16)
    consts = [wl["w_gate"], wl["w_o_mla"], wl["w_o_diff"], wl["w_out"], router_w, router_b, tri]
    return pl.pallas_call(
        functools.partial(_merge_kernel, cap=rows),
        grid=(rows // tm,),
        in_specs=[row_spec(D_MODEL), pl.BlockSpec((None, N_MOD, D_MODEL), mod_map),
                  _const_spec(wl["g_attn"].shape), _const_spec(wl["g_ffn"].shape),
                  row_spec(MLA_W), row_spec(DIFF_HEADS * LANES)]
        + [_const_spec(c.shape) for c in consts],
        out_specs=[row_spec(D_MODEL), row_spec(F_EXT_W), row_spec(LANES), _const_spec((1, LANES))],
        out_shape=[jax.ShapeDtypeStruct((rows, D_MODEL), F32), jax.ShapeDtypeStruct((rows, F_EXT_W), F32),
                   jax.ShapeDtypeStruct((rows, LANES), F32), jax.ShapeDtypeStruct((1, LANES), F32)],
        compiler_params=pltpu.CompilerParams(
            dimension_semantics=("arbitrary",), vmem_limit_bytes=VMEM_LIMIT),
        name="merge_router",
    )(x_flat, mods, wl["g_attn"], wl["g_ffn"], o_mla, o_diff, *consts)


def _moe_plan(pos_rec, counts_rec, n_rows):
    t = SORT_TILE
    tiles_per_bucket = n_rows // t
    counts = counts_rec[0, :N_CLASSES].astype(jnp.int32)
    pos = pos_rec[:, 0].astype(jnp.int32)
    src = jnp.zeros((N_CLASSES * n_rows,), jnp.int32).at[pos].set(jnp.arange(n_rows, dtype=jnp.int32))
    class_tiles = (counts + t - 1) // t
    tile_end = jnp.cumsum(class_tiles)
    n_tiles = tiles_per_bucket + N_CLASSES
    tile = jnp.arange(n_tiles, dtype=jnp.int32)
    tile_cls = jnp.minimum(jnp.searchsorted(tile_end, tile, side="right"), N_CLASSES - 1).astype(jnp.int32)
    in_class = tile - (tile_end - class_tiles)[tile_cls]
    n_used = tile_end[-1].reshape(1)
    used = tile < n_used[0]
    blk = jnp.where(used, tile_cls * tiles_per_bucket + in_class, 0)
    n_valid = jnp.where(used, jnp.clip(counts[tile_cls] - in_class * t, 0, t), 0)
    base = (tile_cls // N_PAIRS) * EXPERTS_PER_GROUP
    pair = tile_cls % N_PAIRS
    ea = base + jnp.asarray(PAIR_LO, jnp.int32)[pair]
    eb = base + jnp.asarray(PAIR_HI, jnp.int32)[pair]
    return ea, eb, n_used, blk, n_valid, src.reshape(N_CLASSES * tiles_per_bucket, 1, t)


def _moe_sparse_kernel(ea_ref, eb_ref, nu_ref, blk_ref, nv_ref, src0_ref, srcn_ref, srcc_ref, f_hbm,
                       wgu_s, wd_s, wgu_a, wd_a, wgu_b, wd_b, y_hbm, xbuf, ybuf, gsem, ssem, *, n_rows):
    del ea_ref, eb_ref, blk_ref
    t = SORT_TILE
    i = pl.program_id(0)
    n_used = nu_ref[0]
    n_valid = nv_ref[i]
    slot = i % 2

    def start_gather(idx_ref, s):
        for r in range(t):
            pltpu.make_async_copy(f_hbm.at[pl.ds(idx_ref[0, r], 1), :], xbuf.at[s, pl.ds(r, 1), :],
                                  gsem.at[s]).start()

    def gather_done(s):
        return pltpu.make_async_copy(f_hbm.at[pl.ds(0, t), :], xbuf.at[s], gsem.at[s])

    def scatter_done(s):
        return pltpu.make_async_copy(ybuf.at[s], y_hbm.at[pl.ds(0, t), :], ssem.at[s])

    @pl.when(i == 0)
    def _():
        start_gather(src0_ref, 0)

    @pl.when(i < n_used)
    def _():
        gather_done(slot).wait()
        start_gather(srcn_ref, 1 - slot)
        x = xbuf[slot, :, :D_MODEL].astype(BF16)
        w = xbuf[slot, :, D_MODEL:]

        def ffn(wgu_ref, wd_ref, scale):
            gu = _dot(x, wgu_ref[...])
            g = gu[:, :D_EXPERT]
            he = g * _sigmoid(g) * gu[:, D_EXPERT:]
            if scale is not None:
                he = he * scale
            return _dot(he.astype(BF16), wd_ref[...])

        ybuf[slot] = ffn(wgu_s, wd_s, None) + ffn(wgu_a, wd_a, w[:, 0:1]) + ffn(wgu_b, wd_b, w[:, 1:2])
        for r in range(t):
            dst = jnp.where(r < n_valid, srcc_ref[0, r], n_rows + slot * t + r)
            pltpu.make_async_copy(ybuf.at[slot, pl.ds(r, 1), :], y_hbm.at[pl.ds(dst, 1), :],
                                  ssem.at[slot]).start()

        @pl.when(i >= 1)
        def _():
            scatter_done(1 - slot).wait()

        @pl.when(i == n_used - 1)
        def _():
            scatter_done(slot).wait()
            gather_done(1 - slot).wait()


def _moe_sparse_call(f_ext, plan, w_gu, w_down, n_rows):
    ea, eb, n_used, blk, n_valid, src = plan
    t = SORT_TILE
    n_tiles = blk.shape[0]
    smem_spec = lambda imap: pl.BlockSpec((None, 1, t), imap, memory_space=pltpu.SMEM)
    wspec = lambda shape, imap: pl.BlockSpec((None,) + shape, imap)
    gu_shape = (D_MODEL, 2 * D_EXPERT)
    dn_shape = (D_EXPERT, D_MODEL)
    grid_spec = pltpu.PrefetchScalarGridSpec(
        num_scalar_prefetch=5,
        grid=(n_tiles,),
        in_specs=[
            smem_spec(lambda i, ea, eb, nu, blk, nv: (blk[0], 0, 0)),
            smem_spec(lambda i, ea, eb, nu, blk, nv: (blk[jnp.minimum(i + 1, n_tiles - 1)], 0, 0)),
            smem_spec(lambda i, ea, eb, nu, blk, nv: (blk[i], 0, 0)),
            pl.BlockSpec(memory_space=pl.ANY),
            wspec(gu_shape, lambda i, ea, eb, nu, blk, nv: (0, 0, 0)),
            wspec(dn_shape, lambda i, ea, eb, nu, blk, nv: (0, 0, 0)),
            wspec(gu_shape, lambda i, ea, eb, nu, blk, nv: (ea[i] + 1, 0, 0)),
            wspec(dn_shape, lambda i, ea, eb, nu, blk, nv: (ea[i] + 1, 0, 0)),
            wspec(gu_shape, lambda i, ea, eb, nu, blk, nv: (eb[i] + 1, 0, 0)),
            wspec(dn_shape, lambda i, ea, eb, nu, blk, nv: (eb[i] + 1, 0, 0)),
        ],
        out_specs=pl.BlockSpec(memory_space=pl.ANY),
        scratch_shapes=[pltpu.VMEM((2, t, F_EXT_W), F32), pltpu.VMEM((2, t, D_MODEL), F32),
                        pltpu.SemaphoreType.DMA((2,)), pltpu.SemaphoreType.DMA((2,))],
    )
    return pl.pallas_call(
        functools.partial(_moe_sparse_kernel, n_rows=n_rows),
        grid_spec=grid_spec,
        out_shape=jax.ShapeDtypeStruct((n_rows + 2 * t, D_MODEL), F32),
        compiler_params=pltpu.CompilerParams(
            dimension_semantics=("arbitrary",), vmem_limit_bytes=VMEM_LIMIT),
        name="moe_sorted",
    )(ea, eb, n_used, blk, n_valid, src, src, src, f_ext, w_gu, w_down, w_gu, w_down, w_gu, w_down)


def _residual_kernel(x_ref, y_ref, mod_ref, o_ref):
    o_ref[...] = x_ref[...] + mod_ref[5:6, :] * y_ref[...]


def _residual_call(x_flat, y, mods, n_batch, seq, out_rows):
    tm = MOE_TILE
    mod_map, _ = _row_class_maps(n_batch * seq // tm, seq // tm, n_batch)
    row_spec = pl.BlockSpec((tm, D_MODEL), lambda i: (i, 0))
    return pl.pallas_call(
        _residual_kernel,
        grid=(out_rows // tm,),
        in_specs=[row_spec, row_spec, pl.BlockSpec((None, N_MOD, D_MODEL), mod_map)],
        out_specs=row_spec,
        out_shape=jax.ShapeDtypeStruct((out_rows, D_MODEL), F32),
        compiler_params=pltpu.CompilerParams(
            dimension_semantics=("parallel",), vmem_limit_bytes=VMEM_LIMIT),
        name="ffn_residual",
    )(x_flat, y, mods)


def _axial_angles(seq, d_rope):
    d_axis = d_rope // 2
    inv = ROPE_BASE ** (-jnp.arange(0, d_axis, 2, dtype=F32) / d_axis)
    t = jnp.arange(seq, dtype=jnp.int32)
    r = (t // GRID_W).astype(F32)
    col = (t % GRID_W).astype(F32)
    ang_r = r[:, None] * inv
    ang_c = col[:, None] * inv
    cos = jnp.concatenate([jnp.cos(ang_r)] * 2 + [jnp.cos(ang_c)] * 2, axis=1)
    sin = jnp.concatenate([jnp.sin(ang_r)] * 2 + [jnp.sin(ang_c)] * 2, axis=1)
    nf = d_rope // 4
    first = jnp.concatenate([jnp.ones((nf,), F32), jnp.zeros((nf,), F32)] * 2)
    return cos, -sin * first, sin * (1.0 - first)


def _rope_table(seq, tile):
    cm, sma, smb = _axial_angles(seq, MLA_ROPE)
    pad_l = jnp.zeros((seq, MLA_NOPE), F32)
    pad_r = jnp.zeros((seq, HEAD_PAD - MLA_NOPE - MLA_ROPE), F32)
    cos_m = jnp.concatenate([pad_l + 1.0, cm, pad_r + 1.0], axis=1)
    sin_ma = jnp.concatenate([pad_l, sma, pad_r], axis=1)
    sin_mb = jnp.concatenate([pad_l, smb, pad_r], axis=1)
    cd, sda, sdb = _axial_angles(seq, DIFF_DH)
    tab = jnp.concatenate([cos_m, sin_ma, sin_mb] + [jnp.concatenate([a, a], axis=1) for a in (cd, sda, sdb)],
                          axis=1)
    ident = jnp.concatenate([jnp.ones((tile, LANES), F32), jnp.zeros((tile, 2 * LANES), F32)] * 2, axis=1)
    return jnp.concatenate([tab, ident], axis=0)


def _block_diag_mean(seg_sizes):
    m = jnp.zeros((MXU_DIM, MXU_DIM), F32)
    off = 0
    for size, active in seg_sizes:
        if active:
            m = m.at[off:off + size, off:off + size].set(1.0 / size)
        off += size
    assert off == MXU_DIM
    return m.astype(BF16)


def _prep_weights(attn_norm_g, ffn_norm_g, w_in, mla_q_norm_g, mla_w_uq, mla_kv_norm_g, mla_w_ukv, mla_q_g,
                  mla_k_g, diff_q_g, diff_k_g, diff_sub_g, w_o_mla, w_o_diff, w_out, moe_w_gate, moe_w_up,
                  moe_w_down, sh_w_gate, sh_w_up, sh_w_down):
    depth = w_in.shape[0]
    d = D_MODEL
    z = lambda w: jnp.zeros((depth, d, w), F32)
    off_kr = MLA_KV_RANK
    off_kd = off_kr + MLA_ROPE
    off_vd = off_kd + DIFF_QK_W
    off_cq = off_vd + DIFF_HEADS * DIFF_V
    off_qd = off_cq + MLA_Q_RANK
    off_gate = off_qd + DIFF_QK_W
    w_in_a = jnp.concatenate([
        w_in[:, :, :off_kr], z(MLA_NOPE), w_in[:, :, off_kr:off_kd], z(HEAD_PAD - MLA_NOPE - MLA_ROPE),
        w_in[:, :, off_kd:off_gate]], axis=2).astype(BF16)
    w_gate = w_in[:, :, off_gate:].astype(BF16)

    ukv = mla_w_ukv.reshape(depth, MLA_KV_RANK, MLA_HEADS, MLA_NOPE + MLA_V)
    pad_k = jnp.zeros((depth, MLA_KV_RANK, MLA_HEADS, HEAD_PAD - MLA_NOPE), F32)
    pad_v = jnp.zeros((depth, MLA_KV_RANK, MLA_HEADS, HEAD_PAD - MLA_V), F32)
    w_ukv = jnp.concatenate([
        jnp.concatenate([ukv[..., :MLA_NOPE], pad_k], axis=-1).reshape(depth, MLA_KV_RANK, MLA_W),
        jnp.concatenate([ukv[..., MLA_NOPE:], pad_v], axis=-1).reshape(depth, MLA_KV_RANK, MLA_W)],
        axis=-1).astype(BF16)
    uq = mla_w_uq.reshape(depth, MLA_Q_RANK, MLA_HEADS, MLA_NOPE + MLA_ROPE)
    w_uq = jnp.pad(uq, ((0, 0), (0, 0), (0, 0), (0, HEAD_PAD - MLA_NOPE - MLA_ROPE))).reshape(
        depth, MLA_Q_RANK, MLA_W).astype(BF16)
    wom = w_o_mla.reshape(depth, MLA_HEADS, MLA_V, d)
    w_o_mla_p = jnp.pad(wom, ((0, 0), (0, 0), (0, HEAD_PAD - MLA_V), (0, 0))).reshape(depth, MLA_W, d).astype(BF16)

    def per_head(g96, scale):
        g = jnp.pad(g96 * scale, ((0, 0), (0, HEAD_PAD - MLA_NOPE - MLA_ROPE)))
        return jnp.tile(g, (1, MLA_HEADS)).reshape(depth, 1, MLA_W)

    k_nope_only = jnp.concatenate([mla_k_g[:, :MLA_NOPE], jnp.zeros((depth, MLA_ROPE), F32)], axis=1)
    g_k = per_head(k_nope_only, 1.0)
    g_kr = jnp.concatenate([jnp.zeros((depth, MLA_NOPE), F32), mla_k_g[:, MLA_NOPE:],
                            jnp.zeros((depth, HEAD_PAD - MLA_NOPE - MLA_ROPE), F32)], axis=1).reshape(depth, 1, LANES)
    g_q = per_head(mla_q_g, MLA_SCALE * LOG2E)
    g_kd = jnp.tile(diff_k_g.reshape(depth, 2 * DIFF_DH), (1, DIFF_HEADS)).reshape(depth, 1, DIFF_QK_W)
    g_qd = jnp.tile(diff_q_g.reshape(depth, 2 * DIFF_DH) * (DIFF_SCALE * LOG2E),
                    (1, DIFF_HEADS)).reshape(depth, 1, DIFF_QK_W)
    lam_inits = [0.8 - 0.6 * math.exp(-0.3 * l) for l in range(depth)]
    g_sub = diff_sub_g * (1.0 - jnp.asarray(lam_inits, F32))[:, None]

    def seg_norm2(g, n):
        return n * jnp.max(g * g, axis=-1)

    q_n2 = seg_norm2(mla_q_g[:, :MLA_NOPE], MLA_NOPE) + seg_norm2(mla_q_g[:, MLA_NOPE:], MLA_ROPE)
    k_n2 = seg_norm2(mla_k_g[:, :MLA_NOPE], MLA_NOPE) + seg_norm2(mla_k_g[:, MLA_NOPE:], MLA_ROPE)
    shift_m = (jnp.sqrt(q_n2 * k_n2) * (MLA_SCALE * LOG2E * SHIFT_MARGIN)).astype(BF16).astype(F32)
    shift_d = (jnp.sqrt(seg_norm2(diff_q_g, DIFF_DH) * seg_norm2(diff_k_g, DIFF_DH))
               * (DIFF_SCALE * LOG2E * SHIFT_MARGIN)).astype(BF16).astype(F32)
    flag_m = shift_m <= SAFE_SHIFT
    flag_d = jnp.max(shift_d, axis=-1) <= SAFE_SHIFT
    aug_lane = (jnp.arange(HEAD_PAD) == MLA_NOPE + MLA_ROPE).astype(F32)
    q_aug = jnp.tile(aug_lane[None, :] * jnp.where(flag_m, -shift_m, 0.0)[:, None], (1, MLA_HEADS))
    qd_aug = jnp.pad(jnp.where(flag_d[:, None], -shift_d, 0.0), ((0, 0), (0, LANES - 2)))

    v_one = jnp.tile((jnp.arange(HEAD_PAD) == MLA_V).astype(F32), MLA_HEADS).reshape(1, MLA_W)
    b_k = _block_diag_mean([(MLA_NOPE, True), (HEAD_PAD - MLA_NOPE, False)] * 2)
    b_q = _block_diag_mean([(MLA_NOPE, True), (MLA_ROPE, True), (HEAD_PAD - MLA_NOPE - MLA_ROPE, False)] * 2)
    b_d = _block_diag_mean([(DIFF_DH, True)] * 4)

    w_gu = jnp.concatenate([
        jnp.concatenate([sh_w_gate, sh_w_up], axis=-1)[:, None].astype(BF16),
        jnp.concatenate([moe_w_gate.astype(BF16), moe_w_up.astype(BF16)], axis=-1)], axis=1)
    w_dn = jnp.concatenate([sh_w_down[:, None].astype(BF16), moe_w_down.astype(BF16)], axis=1)

    layers = []
    for l in range(depth):
        layers.append({
            "g_attn": attn_norm_g[l].reshape(1, d), "g_ffn": ffn_norm_g[l].reshape(1, d),
            "w_in_a": w_in_a[l], "w_gate": w_gate[l], "w_ukv": w_ukv[l], "w_uq": w_uq[l],
            "g_kv": mla_kv_norm_g[l].reshape(1, MLA_KV_RANK), "g_cq": mla_q_norm_g[l].reshape(1, MLA_Q_RANK),
            "g_k": g_k[l], "g_kr": g_kr[l], "g_q": g_q[l], "g_kd": g_kd[l], "g_qd": g_qd[l],
            "g_sub": g_sub[l].reshape(1, DIFF_V), "b_k": b_k, "b_q": b_q, "b_d": b_d, "v_one": v_one,
            "w_o_mla": w_o_mla_p[l], "w_o_diff": w_o_diff[l].astype(BF16), "w_out": w_out[l].astype(BF16),
            "w_gu": w_gu[l], "w_dn": w_dn[l], "lam_init": lam_inits[l],
            "q_aug": q_aug[l].reshape(1, MLA_W), "qd_aug": qd_aug[l].reshape(1, LANES),
            "flag_m": flag_m[l].astype(jnp.int32).reshape(1), "flag_d": flag_d[l].astype(jnp.int32).reshape(1),
        })
    return layers


def kernel(x, c, ctx, c_ctx, ada_w, ada_b, attn_norm_g, ffn_norm_g, w_in, mla_q_norm_g, mla_w_uq, mla_kv_norm_g, mla_w_ukv, mla_q_g, mla_k_g, diff_q_g, diff_k_g, diff_lam, diff_sub_g, w_o_mla, w_o_diff, w_out, router_w, router_b, moe_w_gate, moe_w_up, moe_w_down, sh_w_gate, sh_w_up, sh_w_down):
    n_batch, seq, d = x.shape
    n_ctx = ctx.shape[1]
    depth = w_in.shape[0]
    assert d == D_MODEL and n_ctx == ROW_TILE and seq % GRID_W == 0
    assert seq % MLA_TQ == 0 and seq % DIFF_TQ == 0 and ((seq + n_ctx) % KV_CHUNK) % 16 == 0
    assert (n_batch * seq) % MOE_TILE == 0 and (n_batch * n_ctx) % MOE_TILE == 0
    lat_rows = n_batch * seq
    all_rows = lat_rows + n_batch * n_ctx

    layers = _prep_weights(attn_norm_g, ffn_norm_g, w_in, mla_q_norm_g, mla_w_uq, mla_kv_norm_g, mla_w_ukv,
                           mla_q_g, mla_k_g, diff_q_g, diff_k_g, diff_sub_g, w_o_mla, w_o_diff, w_out,
                           moe_w_gate, moe_w_up, moe_w_down, sh_w_gate, sh_w_up, sh_w_down)
    rope_tab = _rope_table(seq, ROW_TILE)
    rw = jnp.pad(router_w, ((0, 0), (1, LANES - 1 - N_EXPERTS)))
    rb = jnp.pad(router_b, (1, LANES - 1 - N_EXPERTS)).reshape(1, LANES)

    pad_rows = (-(n_batch + 1)) % 8
    c_all = jnp.concatenate([c, c_ctx[None, :], jnp.zeros((pad_rows, d), F32)], axis=0)
    mods_all = _mods_call(c_all, ada_w, ada_b).reshape(depth, c_all.shape[0], N_MOD, d)

    xf = jnp.concatenate([x.reshape(lat_rows, d), ctx.reshape(n_batch * n_ctx, d)], axis=0)
    for l in range(depth):
        wl = layers[l]
        mods = mods_all[l]
        k_m, v_m, q_m, k_d, v_d, q_d = _proj_call(xf, mods, wl, rope_tab, n_batch, seq)
        last = l == depth - 1
        o_m = _attn_call(wl["flag_m"], q_m, k_m, v_m, n_batch, seq, n_ctx, MLA_HEADS, 1, [], 0.0, MLA_TQ,
                         not last, "attn_mla")
        o_d = _attn_call(wl["flag_d"], q_d, k_d, v_d, n_batch, seq, n_ctx, DIFF_HEADS, 2,
                         [diff_lam[l], wl["g_sub"]], wl["lam_init"], DIFF_TQ, not last, "attn_diff")
        out_rows = lat_rows if last else all_rows
        xf, f_ext, pos, counts = _merge_call(xf, mods, o_m, o_d, wl, rw, rb, n_batch, seq, out_rows)
        y = _moe_sparse_call(f_ext, _moe_plan(pos, counts, out_rows), wl["w_gu"], wl["w_dn"], out_rows)
        xf = _residual_call(xf, y, mods, n_batch, seq, out_rows)
    return xf.reshape(n_batch, seq, d)
```

```python
import functools
import math

import jax
import jax.numpy as jnp
from jax import lax
from jax.experimental import pallas as pl
from jax.experimental.pallas import tpu as pltpu

F32 = jnp.float32
BF16 = jnp.bfloat16

D_MODEL = 1024
GRID_W = 64
EPS = 1e-6
ROPE_BASE = 10000.0
N_MOD = 6

MLA_HEADS = 8
MLA_NOPE = 64
MLA_ROPE = 32
MLA_V = 64
MLA_Q_RANK = 512
MLA_KV_RANK = 256
MLA_SCALE = 1.0 / math.sqrt(MLA_NOPE + MLA_ROPE)

DIFF_HEADS = 4
DIFF_DH = 64
DIFF_V = 2 * DIFF_DH
DIFF_SCALE = 1.0 / math.sqrt(DIFF_DH)

N_EXPERTS = 16
EXPERTS_PER_GROUP = 4
N_GROUPS = N_EXPERTS // EXPERTS_PER_GROUP
D_EXPERT = 512
PAIR_LO = (0, 0, 0, 1, 1, 2)
PAIR_HI = (1, 2, 3, 2, 3, 3)
N_PAIRS = len(PAIR_LO)
N_CLASSES = N_GROUPS * N_PAIRS

LANES = 128
MXU_DIM = 256
HEAD_PAD = LANES
MLA_W = MLA_HEADS * HEAD_PAD
DIFF_QK_W = DIFF_HEADS * 2 * DIFF_DH
DIFF_K_W = DIFF_HEADS * 2 * LANES
DIFF_V_PAD = 2 * LANES
DIFF_V_W = DIFF_HEADS * DIFF_V_PAD

A_CKV = 0
A_KR = A_CKV + MLA_KV_RANK
A_KD = A_KR + LANES
A_VD = A_KD + DIFF_QK_W
A_CQ = A_VD + DIFF_HEADS * DIFF_V
A_QD = A_CQ + MLA_Q_RANK
A_COLS = A_QD + DIFF_QK_W

ROPE_TAB_W = 6 * LANES
F_EXT_W = D_MODEL + LANES

ROW_TILE = 256
MOE_TILE = 512
SORT_TILE = 256
KV_CHUNK = 768
MLA_TQ = 2048
DIFF_TQ = 512
LOOP_STAGES = 4
LOG2E = math.log2(math.e)
VMEM_LIMIT = 56 * 1024 * 1024


def _sigmoid(x):
    return 1.0 / (1.0 + jnp.exp(-x))


def _dot(a, b):
    return jnp.dot(a, b, preferred_element_type=F32)


def _split_bf16(a):
    hi = a.astype(BF16)
    lo = (a - hi.astype(F32)).astype(BF16)
    return hi, lo


def _dot3(a, b):
    a_hi, a_lo = _split_bf16(a)
    b_hi, b_lo = _split_bf16(b)
    return _dot(a_hi, b_hi) + _dot(a_lo, b_hi) + _dot(a_hi, b_lo)


def _mods_kernel(c_ref, w_ref, b_ref, o_ref):
    c = c_ref[...]
    a = c * _sigmoid(c)
    o_ref[...] = _dot3(a, w_ref[...]) + b_ref[...]


def _mods_call(c_all, ada_w, ada_b):
    depth, d, n = ada_w.shape
    tn = 1536
    rows = c_all.shape[0]
    return pl.pallas_call(
        _mods_kernel,
        grid=(depth, n // tn),
        in_specs=[
            pl.BlockSpec((rows, d), lambda l, j: (0, 0)),
            pl.BlockSpec((None, d, tn), lambda l, j: (l, 0, j)),
            pl.BlockSpec((None, 1, tn), lambda l, j: (l, 0, j)),
        ],
        out_specs=pl.BlockSpec((None, rows, tn), lambda l, j: (l, 0, j)),
        out_shape=jax.ShapeDtypeStruct((depth, rows, n), F32),
        compiler_params=pltpu.CompilerParams(
            dimension_semantics=("parallel", "parallel"), vmem_limit_bytes=VMEM_LIMIT),
        name="adaln_mods",
    )(c_all, ada_w, ada_b.reshape(depth, 1, n))


def _rms(x, g):
    return x * lax.rsqrt(jnp.mean(x * x, axis=-1, keepdims=True) + EPS) * g


def _seg_mean_sq(x, b_ref):
    sq = (x * x).astype(BF16)
    b = b_ref[...]
    parts = [_dot(sq[:, c:c + MXU_DIM], b) for c in range(0, x.shape[1], MXU_DIM)]
    return parts[0] if len(parts) == 1 else jnp.concatenate(parts, axis=1)


def _rope(x, cos, sin_a, sin_b, half):
    return (x * cos + pltpu.roll(x, LANES - half, axis=1) * sin_a
            + pltpu.roll(x, half, axis=1) * sin_b)


def _proj_kernel(x_ref, mod_ref, g_ref, win_ref, wukv_ref, wuq_ref, gkv_ref, gcq_ref, gk_ref, gkr_ref,
                 gq_ref, gkd_ref, gqd_ref, bk_ref, bq_ref, bd_ref, vone_ref, qaug_ref, qdaug_ref, rope_ref,
                 kmla_ref, vmla_ref, qmla_ref, kd_ref, vd_ref, qd_ref):
    x = x_ref[...]
    h = _rms(x, g_ref[...]) * (1.0 + mod_ref[1:2, :]) + mod_ref[0:1, :]
    p = _dot(h.astype(BF16), win_ref[...])

    cos_m = rope_ref[:, 0 * LANES:1 * LANES]
    sin_ma = rope_ref[:, 1 * LANES:2 * LANES]
    sin_mb = rope_ref[:, 2 * LANES:3 * LANES]
    cos_d = rope_ref[:, 3 * LANES:4 * LANES]
    sin_da = rope_ref[:, 4 * LANES:5 * LANES]
    sin_db = rope_ref[:, 5 * LANES:6 * LANES]

    ckv = _rms(p[:, A_CKV:A_KR], gkv_ref[...])
    kv = _dot(ckv.astype(BF16), wukv_ref[...])
    kn = kv[:, :MLA_W]
    kn = kn * lax.rsqrt(_seg_mean_sq(kn, bk_ref) + EPS) * gk_ref[...]
    kr = p[:, A_KR:A_KD]
    kr = kr * lax.rsqrt(jnp.sum(kr * kr, axis=-1, keepdims=True) * (1.0 / MLA_ROPE) + EPS) * gkr_ref[...]
    kr = _rope(kr, cos_m, sin_ma, sin_mb, MLA_ROPE // 4)
    tm = x.shape[0]
    lane_blk = lax.broadcasted_iota(jnp.int32, (tm, LANES), 1)
    kr = kr + (lane_blk == MLA_NOPE + MLA_ROPE).astype(F32)
    for hd in range(MLA_HEADS):
        sl = slice(hd * HEAD_PAD, (hd + 1) * HEAD_PAD)
        kmla_ref[:, sl] = (kn[:, sl] + kr).astype(BF16)
    vmla_ref[...] = (kv[:, MLA_W:] + vone_ref[...]).astype(BF16)

    cq = _rms(p[:, A_CQ:A_QD], gcq_ref[...])
    q = _dot(cq.astype(BF16), wuq_ref[...])
    q = q * lax.rsqrt(_seg_mean_sq(q, bq_ref) + EPS) * gq_ref[...]
    for hd in range(MLA_HEADS):
        sl = slice(hd * HEAD_PAD, (hd + 1) * HEAD_PAD)
        qmla_ref[:, sl] = (_rope(q[:, sl], cos_m, sin_ma, sin_mb, MLA_ROPE // 4) + qaug_ref[:, sl]).astype(BF16)

    kd = p[:, A_KD:A_VD]
    kd = kd * lax.rsqrt(_seg_mean_sq(kd, bd_ref) + EPS) * gkd_ref[...]
    qd = p[:, A_QD:A_COLS]
    qd = qd * lax.rsqrt(_seg_mean_sq(qd, bd_ref) + EPS) * gqd_ref[...]
    ones_blk = (lane_blk == 0).astype(BF16)
    two_ones_blk = (lane_blk < 2).astype(BF16)
    qd_shift = jnp.broadcast_to(qdaug_ref[...], (tm, LANES)).astype(BF16)
    for hd in range(DIFF_HEADS):
        sl = slice(hd * LANES, (hd + 1) * LANES)
        lo = slice(2 * hd * LANES, (2 * hd + 1) * LANES)
        hi = slice((2 * hd + 1) * LANES, (2 * hd + 2) * LANES)
        kd_ref[:, lo] = _rope(kd[:, sl], cos_d, sin_da, sin_db, DIFF_DH // 4).astype(BF16)
        kd_ref[:, hi] = two_ones_blk
        qd_ref[:, lo] = _rope(qd[:, sl], cos_d, sin_da, sin_db, DIFF_DH // 4).astype(BF16)
        qd_ref[:, hi] = qd_shift

    vd = p[:, A_VD:A_CQ]
    for hd in range(DIFF_HEADS):
        vd_ref[:, hd * DIFF_V_PAD:hd * DIFF_V_PAD + DIFF_V] = vd[:, hd * DIFF_V:(hd + 1) * DIFF_V].astype(BF16)
        vd_ref[:, hd * DIFF_V_PAD + DIFF_V:(hd + 1) * DIFF_V_PAD] = ones_blk


def _row_class_maps(n_lat_tiles, tiles_per_batch, n_batch):
    def mod_map(i):
        return (jnp.where(i < n_lat_tiles, i // tiles_per_batch, n_batch), 0, 0)

    def rope_map(i):
        return (jnp.where(i < n_lat_tiles, i % tiles_per_batch, tiles_per_batch), 0)

    return mod_map, rope_map


def _const_spec(shape):
    return pl.BlockSpec(shape, lambda i: tuple(0 for _ in shape))


def _proj_call(x_flat, mods, wl, rope_tab, n_batch, seq):
    rows = x_flat.shape[0]
    tm = ROW_TILE
    n_lat_tiles = n_batch * seq // tm
    mod_map, rope_map = _row_class_maps(n_lat_tiles, seq // tm, n_batch)
    row_spec = lambda w: pl.BlockSpec((tm, w), lambda i: (i, 0))
    tiles_per_batch = seq // tm

    def kv_map(i):
        lat = i + i // tiles_per_batch + 1
        ctx = (i - n_lat_tiles) * (tiles_per_batch + 1)
        return (jnp.where(i < n_lat_tiles, lat, ctx), 0)

    kv_spec = lambda w: pl.BlockSpec((tm, w), kv_map)
    consts = [wl["g_attn"], wl["w_in_a"], wl["w_ukv"], wl["w_uq"], wl["g_kv"], wl["g_cq"], wl["g_k"], wl["g_kr"],
              wl["g_q"], wl["g_kd"], wl["g_qd"], wl["b_k"], wl["b_q"], wl["b_d"], wl["v_one"], wl["q_aug"],
              wl["qd_aug"]]
    out_w = [MLA_W, MLA_W, MLA_W, DIFF_K_W, DIFF_V_W, DIFF_K_W]
    out_specs = [kv_spec(MLA_W), kv_spec(MLA_W), row_spec(MLA_W), kv_spec(DIFF_K_W), kv_spec(DIFF_V_W),
                 row_spec(DIFF_K_W)]
    return pl.pallas_call(
        _proj_kernel,
        grid=(rows // tm,),
        in_specs=[row_spec(D_MODEL), pl.BlockSpec((None, N_MOD, D_MODEL), mod_map)]
        + [_const_spec(c.shape) for c in consts]
        + [pl.BlockSpec((tm, ROPE_TAB_W), rope_map)],
        out_specs=out_specs,
        out_shape=[jax.ShapeDtypeStruct((rows, w), BF16) for w in out_w],
        compiler_params=pltpu.CompilerParams(
            dimension_semantics=("parallel",), vmem_limit_bytes=VMEM_LIMIT),
        name="proj_qkv",
    )(x_flat, mods, *consts, rope_tab)


SAFE_SHIFT = 50.0
SHIFT_MARGIN = 1.02


def _attn_kernel(flag_ref, *refs, n_maps, tq, tk, chunks, lam_init, aliased_out):
    refs = list(refs)
    q_ref, k_ref, v_ref = refs[:3]
    n_in = 3 + (2 if n_maps == 2 else 0) + (1 if aliased_out else 0)
    o_ref = refs[n_in]
    if n_maps == 2:
        lam_ref, subg_ref = refs[3:5]
        qs_ref, s_ref, p_ref, m_ref, a_ref, acc_ref = refs[n_in + 1:]
        qf = q_ref[...].astype(F32)
        lane = lax.broadcasted_iota(jnp.int32, qf.shape, 1)
        keep0 = (lane < DIFF_DH) | (lane == 2 * DIFF_DH)
        keep1 = ((lane >= DIFF_DH) & (lane < 2 * DIFF_DH)) | (lane == 2 * DIFF_DH + 1)
        qs_ref[0:tq, :] = jnp.where(keep0, qf, 0.0).astype(BF16)
        qs_ref[tq:2 * tq, :] = jnp.where(keep1, qf, 0.0).astype(BF16)
        q_src = qs_ref
    else:
        s_ref, p_ref, m_ref, a_ref, acc_ref = refs[n_in + 1:]
        q_src = q_ref
    vw = v_ref.shape[1]
    sum_col = vw // 2

    head, n_mid, tail = chunks
    n_chunks = 1 + n_mid + (1 if tail else 0)

    def geom(j):
        if isinstance(j, int):
            if j == 0:
                return 0, head
            if j <= n_mid:
                return head + (j - 1) * tk, tk
            return head + n_mid * tk, tail
        return pl.multiple_of(head + (j - 1) * tk, math.gcd(head, tk)), tk

    def rows_of(ref, j):
        off, size = geom(j)
        return ref[pl.ds(off, size), :]

    def scores(j, slot):
        size = geom(j)[1]
        s_ref[slot, :, :size] = lax.dot_general(q_src[...], rows_of(k_ref, j), (((1,), (1,)), ((), ())),
                                                preferred_element_type=F32)

    def probs_shifted(j, slot):
        size = geom(j)[1]
        p_ref[slot, :, :size] = jnp.exp2(s_ref[slot, :, :size]).astype(BF16)

    def accumulate_plain(j, slot):
        size = geom(j)[1]
        r = _dot(p_ref[slot, :, :size], rows_of(v_ref, j))
        if isinstance(j, int) and j == 0:
            acc_ref[...] = r
        else:
            acc_ref[...] += r

    def probs_online(j, slot):
        size = geom(j)[1]
        s = s_ref[slot, :, :size]
        mx = jnp.max(s, axis=-1, keepdims=True)
        if isinstance(j, int) and j == 0:
            m_new = mx
        else:
            m_old = m_ref[...]
            m_new = jnp.maximum(m_old, mx)
            a_ref[slot] = jnp.exp2(m_old - m_new)
        m_ref[...] = m_new
        p_ref[slot, :, :size] = jnp.exp2(s - m_new).astype(BF16)

    def accumulate_online(j, slot):
        size = geom(j)[1]
        r = _dot(p_ref[slot, :, :size], rows_of(v_ref, j))
        if isinstance(j, int) and j == 0:
            acc_ref[...] = r
        else:
            acc_ref[...] = acc_ref[...] * a_ref[slot] + r

    def sweep(probs, accumulate):
        def stage(j, slot, with_scores):
            probs(j, slot)
            accumulate(j - 1, 1 - slot)
            if with_scores:
                scores(j + 1, 1 - slot)

        scores(0, 0)
        if n_chunks > 1:
            scores(1, 1)
        probs(0, 0)
        if n_chunks > 1:
            stage(1, 1, n_chunks > 2)
            n_loop = max(n_mid - 2, 0) // LOOP_STAGES

            def group(i, carry):
                j0 = 2 + LOOP_STAGES * i
                for t in range(LOOP_STAGES):
                    stage(j0 + t, t % 2, True)
                return carry

            lax.fori_loop(0, n_loop, group, 0)
            for j in range(2 + LOOP_STAGES * n_loop, n_chunks):
                stage(j, j % 2, j + 1 < n_chunks)
        accumulate(n_chunks - 1, (n_chunks - 1) % 2)

    @pl.when(flag_ref[0] == 1)
    def _():
        sweep(probs_shifted, accumulate_plain)

    @pl.when(flag_ref[0] != 1)
    def _():
        sweep(probs_online, accumulate_online)

    acc = acc_ref[...]
    inv_l = 1.0 / acc[:, sum_col:sum_col + 1]
    if n_maps == 1:
        o_ref[...] = (acc * inv_l).astype(o_ref.dtype)
    else:
        lf = lam_ref[...]
        lam = (jnp.exp(jnp.sum(lf[0:1, :] * lf[1:2, :], axis=-1, keepdims=True))
               - jnp.exp(jnp.sum(lf[2:3, :] * lf[3:4, :], axis=-1, keepdims=True)) + lam_init)
        o = acc[:, :DIFF_V] * inv_l
        o = o[:tq] - lam * o[tq:]
        o_ref[...] = _rms(o, subg_ref[...]).astype(o_ref.dtype)


def _attn_scratch(n_maps, tq, tk, kw, vw):
    r = n_maps * tq
    qs = [pltpu.VMEM((r, kw), BF16)] if n_maps == 2 else []
    return qs + [pltpu.VMEM((2, r, tk), F32), pltpu.VMEM((2, r, tk), BF16), pltpu.VMEM((r, 1), F32),
                 pltpu.VMEM((2, r, 1), F32), pltpu.VMEM((r, vw), F32)]


def _attn_call(flag, q, k, v, n_batch, seq, n_ctx, n_heads, n_maps, extra, lam_init, tq, with_ctx_queries, name):
    rows = q.shape[0]
    vw = v.shape[1] // n_heads
    kw = k.shape[1] // n_heads
    qw = q.shape[1] // n_heads
    keys = seq + n_ctx
    tk = KV_CHUNK
    nq = seq // tq
    assert seq % tq == 0
    chunks = (tk, keys // tk - 1, keys % tk)
    ctx_blk0 = n_batch * seq // n_ctx
    blocks_per_batch = keys // n_ctx
    extra_specs = [pl.BlockSpec(e.shape, lambda *_: (0, 0)) for e in extra]
    out_shape = jax.ShapeDtypeStruct((rows, n_heads * LANES), BF16)

    kern = functools.partial(_attn_kernel, n_maps=n_maps, tq=tq, tk=tk, chunks=chunks, lam_init=lam_init,
                             aliased_out=False)
    out = pl.pallas_call(
        kern,
        grid_spec=pltpu.PrefetchScalarGridSpec(
            num_scalar_prefetch=1,
            grid=(n_batch, n_heads, nq),
            in_specs=[pl.BlockSpec((tq, qw), lambda b, h, qi, f: (b * nq + qi, h)),
                      pl.BlockSpec((keys, kw), lambda b, h, qi, f: (b, h)),
                      pl.BlockSpec((keys, vw), lambda b, h, qi, f: (b, h))] + extra_specs,
            out_specs=pl.BlockSpec((tq, LANES), lambda b, h, qi, f: (b * nq + qi, h)),
            scratch_shapes=_attn_scratch(n_maps, tq, tk, kw, vw)),
        out_shape=out_shape,
        compiler_params=pltpu.CompilerParams(
            dimension_semantics=("parallel", "parallel", "arbitrary"), vmem_limit_bytes=VMEM_LIMIT),
        name=name,
    )(flag, q, k, v, *extra)
    if not with_ctx_queries:
        return out

    kern_c = functools.partial(_attn_kernel, n_maps=n_maps, tq=n_ctx, tk=n_ctx, chunks=(n_ctx, 0, 0),
                               lam_init=lam_init, aliased_out=True)
    return pl.pallas_call(
        kern_c,
        grid_spec=pltpu.PrefetchScalarGridSpec(
            num_scalar_prefetch=1,
            grid=(n_batch, n_heads),
            in_specs=[pl.BlockSpec((n_ctx, qw), lambda b, h, f: (ctx_blk0 + b, h)),
                      pl.BlockSpec((n_ctx, kw), lambda b, h, f: (b * blocks_per_batch, h)),
                      pl.BlockSpec((n_ctx, vw), lambda b, h, f: (b * blocks_per_batch, h))] + extra_specs
            + [pl.BlockSpec(memory_space=pl.ANY)],
            out_specs=pl.BlockSpec((n_ctx, LANES), lambda b, h, f: (ctx_blk0 + b, h)),
            scratch_shapes=_attn_scratch(n_maps, n_ctx, n_ctx, kw, vw)),
        out_shape=out_shape,
        input_output_aliases={4 + len(extra): 0},
        compiler_params=pltpu.CompilerParams(
            dimension_semantics=("parallel", "parallel"), vmem_limit_bytes=VMEM_LIMIT),
        name=name + "_ctx",
    )(flag, q, k, v, *extra, out)


def _route(logits, bias):
    lane = lax.broadcasted_iota(jnp.int32, logits.shape, 1)
    is_exp = (lane >= 1) & (lane <= N_EXPERTS)
    grp = (lane - 1) >> 2
    neg = jnp.float32(-jnp.inf)
    big = jnp.int32(1 << 20)
    aff = _sigmoid(logits)
    sel = jnp.where(is_exp, aff + bias, neg)

    def first_argmax(v):
        mx = jnp.max(v, axis=-1, keepdims=True)
        ix = jnp.min(jnp.where(v == mx, lane, big), axis=-1, keepdims=True)
        return mx, ix

    best = None
    for g in range(N_GROUPS):
        vg = jnp.where(grp == g, sel, neg)
        m1, i1 = first_argmax(vg)
        m2, i2 = first_argmax(jnp.where(lane == i1, neg, vg))
        score = m1 + m2
        if best is None:
            best = (score, i1, i2)
        else:
            upd = score > best[0]
            best = (jnp.where(upd, score, best[0]), jnp.where(upd, i1, best[1]), jnp.where(upd, i2, best[2]))
    _, i1, i2 = best
    a1 = jnp.sum(jnp.where(lane == i1, aff, 0.0), axis=-1, keepdims=True)
    a2 = jnp.sum(jnp.where(lane == i2, aff, 0.0), axis=-1, keepdims=True)
    tot = a1 + a2
    e1 = i1 - 1
    e2 = i2 - 1
    swap = e2 < e1
    lo = jnp.where(swap, e2, e1)
    hi = jnp.where(swap, e1, e2)
    w_lo = jnp.where(swap, a2, a1) / tot
    w_hi = jnp.where(swap, a1, a2) / tot
    l = lo & (EXPERTS_PER_GROUP - 1)
    h = hi & (EXPERTS_PER_GROUP - 1)
    cls = (lo >> 2) * N_PAIRS + ((l * (7 - l)) >> 1) + (h - l - 1)
    return cls, w_lo, w_hi


def _merge_kernel(x_ref, mod_ref, ga_ref, gf_ref, om_ref, od_ref, wgate_ref, wom_ref, wod_ref, wout_ref,
                  rw_ref, rb_ref, tri_ref, xo_ref, f_ref, pos_ref, cnt_ref, *, cap):
    @pl.when(pl.program_id(0) == 0)
    def _():
        cnt_ref[...] = jnp.zeros_like(cnt_ref)

    x = x_ref[...]
    h = _rms(x, ga_ref[...]) * (1.0 + mod_ref[1:2, :]) + mod_ref[0:1, :]
    gates = _sigmoid(_dot(h.astype(BF16), wgate_ref[...]))
    y = (gates[:, :D_MODEL] * _dot(om_ref[...], wom_ref[...])
         + gates[:, D_MODEL:] * _dot(od_ref[...], wod_ref[...]))
    xn = x + mod_ref[2:3, :] * _dot(y.astype(BF16), wout_ref[...])
    xo_ref[...] = xn
    f = _rms(xn, gf_ref[...]) * (1.0 + mod_ref[4:5, :]) + mod_ref[3:4, :]
    cls, w_lo, w_hi = _route(_dot3(f, rw_ref[...]), rb_ref[...])
    lane = lax.broadcasted_iota(jnp.int32, (x.shape[0], LANES), 1)
    f_ref[:, :D_MODEL] = f
    f_ref[:, D_MODEL:] = jnp.where(lane == 0, w_lo, jnp.where(lane == 1, w_hi, 0.0))
    onehot = (lane == cls).astype(F32)
    before = _dot(tri_ref[...], onehot.astype(BF16)) + cnt_ref[...]
    rank = jnp.sum(onehot * before, axis=-1, keepdims=True)
    cnt_ref[...] += jnp.sum(onehot, axis=0, keepdims=True)
    pos = cls.astype(F32) * float(cap) + rank
    pos_ref[...] = jnp.where(lane == 0, pos, 0.0)


def _merge_call(x_flat, mods, o_mla, o_diff, wl, router_w, router_b, n_batch, seq, rows):
    tm = ROW_TILE
    mod_map, _ = _row_class_maps(n_batch * seq // tm, seq // tm, n_batch)
    row_spec = lambda w: pl.BlockSpec((tm, w), lambda i: (i, 0))
    tri = (jnp.arange(tm)[:, None] > jnp.arange(tm)[None, :]).astype(BF16)
    consts = [wl["w_gate"], wl["w_o_mla"], wl["w_o_diff"], wl["w_out"], router_w, router_b, tri]
    return pl.pallas_call(
        functools.partial(_merge_kernel, cap=rows),
        grid=(rows // tm,),
        in_specs=[row_spec(D_MODEL), pl.BlockSpec((None, N_MOD, D_MODEL), mod_map),
                  _const_spec(wl["g_attn"].shape), _const_spec(wl["g_ffn"].shape),
                  row_spec(MLA_W), row_spec(DIFF_HEADS * LANES)]
        + [_const_spec(c.shape) for c in consts],
        out_specs=[row_spec(D_MODEL), row_spec(F_EXT_W), row_spec(LANES), _const_spec((1, LANES))],
        out_shape=[jax.ShapeDtypeStruct((rows, D_MODEL), F32), jax.ShapeDtypeStruct((rows, F_EXT_W), F32),
                   jax.ShapeDtypeStruct((rows, LANES), F32), jax.ShapeDtypeStruct((1, LANES), F32)],
        compiler_params=pltpu.CompilerParams(
            dimension_semantics=("arbitrary",), vmem_limit_bytes=VMEM_LIMIT),
        name="merge_router",
    )(x_flat, mods, wl["g_attn"], wl["g_ffn"], o_mla, o_diff, *consts)


def _moe_plan(pos_rec, counts_rec, n_rows):
    t = SORT_TILE
    tiles_per_bucket = n_rows // t
    counts = counts_rec[0, :N_CLASSES].astype(jnp.int32)
    pos = pos_rec[:, 0].astype(jnp.int32)
    src = jnp.zeros((N_CLASSES * n_rows,), jnp.int32).at[pos].set(jnp.arange(n_rows, dtype=jnp.int32))
    class_tiles = (counts + t - 1) // t
    tile_end = jnp.cumsum(class_tiles)
    n_tiles = tiles_per_bucket + N_CLASSES
    tile = jnp.arange(n_tiles, dtype=jnp.int32)
    tile_cls = jnp.minimum(jnp.searchsorted(tile_end, tile, side="right", method="compare_all"),
                           N_CLASSES - 1).astype(jnp.int32)
    in_class = tile - (tile_end - class_tiles)[tile_cls]
    n_used = tile_end[-1].reshape(1)
    used = tile < n_used[0]
    blk = jnp.where(used, tile_cls * tiles_per_bucket + in_class, 0)
    n_valid = jnp.where(used, jnp.clip(counts[tile_cls] - in_class * t, 0, t), 0)
    base = (tile_cls // N_PAIRS) * EXPERTS_PER_GROUP
    pair = tile_cls % N_PAIRS
    ea = base + jnp.asarray(PAIR_LO, jnp.int32)[pair]
    eb = base + jnp.asarray(PAIR_HI, jnp.int32)[pair]
    return ea, eb, n_used, blk, n_valid, src.reshape(N_CLASSES * tiles_per_bucket, 1, t)


def _moe_sparse_kernel(ea_ref, eb_ref, nu_ref, blk_ref, nv_ref, src0_ref, srcn_ref, srcc_ref, f_hbm,
                       wgu_s, wd_s, wgu_a, wd_a, wgu_b, wd_b, y_hbm, xbuf, ybuf, gsem, ssem, *, n_rows):
    del ea_ref, eb_ref, blk_ref
    t = SORT_TILE
    i = pl.program_id(0)
    n_used = nu_ref[0]
    n_valid = nv_ref[i]
    slot = i % 2

    def start_gather(idx_ref, s):
        for r in range(t):
            pltpu.make_async_copy(f_hbm.at[pl.ds(idx_ref[0, r], 1), :], xbuf.at[s, pl.ds(r, 1), :],
                                  gsem.at[s]).start()

    def gather_done(s):
        return pltpu.make_async_copy(f_hbm.at[pl.ds(0, t), :], xbuf.at[s], gsem.at[s])

    def scatter_done(s):
        return pltpu.make_async_copy(ybuf.at[s], y_hbm.at[pl.ds(0, t), :], ssem.at[s])

    @pl.when(i == 0)
    def _():
        start_gather(src0_ref, 0)

    @pl.when(i < n_used)
    def _():
        gather_done(slot).wait()
        start_gather(srcn_ref, 1 - slot)
        x = xbuf[slot, :, :D_MODEL].astype(BF16)
        w = xbuf[slot, :, D_MODEL:]

        def ffn(wgu_ref, wd_ref, scale):
            gu = _dot(x, wgu_ref[...])
            g = gu[:, :D_EXPERT]
            he = g * _sigmoid(g) * gu[:, D_EXPERT:]
            if scale is not None:
                he = he * scale
            return _dot(he.astype(BF16), wd_ref[...])

        ybuf[slot] = ffn(wgu_s, wd_s, None) + ffn(wgu_a, wd_a, w[:, 0:1]) + ffn(wgu_b, wd_b, w[:, 1:2])
        for r in range(t):
            dst = jnp.where(r < n_valid, srcc_ref[0, r], n_rows + slot * t + r)
            pltpu.make_async_copy(ybuf.at[slot, pl.ds(r, 1), :], y_hbm.at[pl.ds(dst, 1), :],
                                  ssem.at[slot]).start()

        @pl.when(i >= 1)
        def _():
            scatter_done(1 - slot).wait()

        @pl.when(i == n_used - 1)
        def _():
            scatter_done(slot).wait()
            gather_done(1 - slot).wait()


def _moe_sparse_call(f_ext, plan, w_gu, w_down, n_rows):
    ea, eb, n_used, blk, n_valid, src = plan
    t = SORT_TILE
    n_tiles = blk.shape[0]
    smem_spec = lambda imap: pl.BlockSpec((None, 1, t), imap, memory_space=pltpu.SMEM)
    wspec = lambda shape, imap: pl.BlockSpec((None,) + shape, imap)
    gu_shape = (D_MODEL, 2 * D_EXPERT)
    dn_shape = (D_EXPERT, D_MODEL)
    grid_spec = pltpu.PrefetchScalarGridSpec(
        num_scalar_prefetch=5,
        grid=(n_tiles,),
        in_specs=[
            smem_spec(lambda i, ea, eb, nu, blk, nv: (blk[0], 0, 0)),
            smem_spec(lambda i, ea, eb, nu, blk, nv: (blk[jnp.minimum(i + 1, n_tiles - 1)], 0, 0)),
            smem_spec(lambda i, ea, eb, nu, blk, nv: (blk[i], 0, 0)),
            pl.BlockSpec(memory_space=pl.ANY),
            wspec(gu_shape, lambda i, ea, eb, nu, blk, nv: (0, 0, 0)),
            wspec(dn_shape, lambda i, ea, eb, nu, blk, nv: (0, 0, 0)),
            wspec(gu_shape, lambda i, ea, eb, nu, blk, nv: (ea[i] + 1, 0, 0)),
            wspec(dn_shape, lambda i, ea, eb, nu, blk, nv: (ea[i] + 1, 0, 0)),
            wspec(gu_shape, lambda i, ea, eb, nu, blk, nv: (eb[i] + 1, 0, 0)),
            wspec(dn_shape, lambda i, ea, eb, nu, blk, nv: (eb[i] + 1, 0, 0)),
        ],
        out_specs=pl.BlockSpec(memory_space=pl.ANY),
        scratch_shapes=[pltpu.VMEM((2, t, F_EXT_W), F32), pltpu.VMEM((2, t, D_MODEL), F32),
                        pltpu.SemaphoreType.DMA((2,)), pltpu.SemaphoreType.DMA((2,))],
    )
    return pl.pallas_call(
        functools.partial(_moe_sparse_kernel, n_rows=n_rows),
        grid_spec=grid_spec,
        out_shape=jax.ShapeDtypeStruct((n_rows + 2 * t, D_MODEL), F32),
        compiler_params=pltpu.CompilerParams(
            dimension_semantics=("arbitrary",), vmem_limit_bytes=VMEM_LIMIT),
        name="moe_sorted",
    )(ea, eb, n_used, blk, n_valid, src, src, src, f_ext, w_gu, w_down, w_gu, w_down, w_gu, w_down)


def _residual_kernel(x_ref, y_ref, mod_ref, o_ref):
    o_ref[...] = x_ref[...] + mod_ref[5:6, :] * y_ref[...]


def _residual_call(x_flat, y, mods, n_batch, seq, out_rows):
    tm = MOE_TILE
    mod_map, _ = _row_class_maps(n_batch * seq // tm, seq // tm, n_batch)
    row_spec = pl.BlockSpec((tm, D_MODEL), lambda i: (i, 0))
    return pl.pallas_call(
        _residual_kernel,
        grid=(out_rows // tm,),
        in_specs=[row_spec, row_spec, pl.BlockSpec((None, N_MOD, D_MODEL), mod_map)],
        out_specs=row_spec,
        out_shape=jax.ShapeDtypeStruct((out_rows, D_MODEL), F32),
        compiler_params=pltpu.CompilerParams(
            dimension_semantics=("parallel",), vmem_limit_bytes=VMEM_LIMIT),
        name="ffn_residual",
    )(x_flat, y, mods)


def _axial_angles(seq, d_rope):
    d_axis = d_rope // 2
    inv = ROPE_BASE ** (-jnp.arange(0, d_axis, 2, dtype=F32) / d_axis)
    t = jnp.arange(seq, dtype=jnp.int32)
    r = (t // GRID_W).astype(F32)
    col = (t % GRID_W).astype(F32)
    ang_r = r[:, None] * inv
    ang_c = col[:, None] * inv
    cos = jnp.concatenate([jnp.cos(ang_r)] * 2 + [jnp.cos(ang_c)] * 2, axis=1)
    sin = jnp.concatenate([jnp.sin(ang_r)] * 2 + [jnp.sin(ang_c)] * 2, axis=1)
    nf = d_rope // 4
    first = jnp.concatenate([jnp.ones((nf,), F32), jnp.zeros((nf,), F32)] * 2)
    return cos, -sin * first, sin * (1.0 - first)


def _rope_table(seq, tile):
    cm, sma, smb = _axial_angles(seq, MLA_ROPE)
    pad_l = jnp.zeros((seq, MLA_NOPE), F32)
    pad_r = jnp.zeros((seq, HEAD_PAD - MLA_NOPE - MLA_ROPE), F32)
    cos_m = jnp.concatenate([pad_l + 1.0, cm, pad_r + 1.0], axis=1)
    sin_ma = jnp.concatenate([pad_l, sma, pad_r], axis=1)
    sin_mb = jnp.concatenate([pad_l, smb, pad_r], axis=1)
    cd, sda, sdb = _axial_angles(seq, DIFF_DH)
    tab = jnp.concatenate([cos_m, sin_ma, sin_mb] + [jnp.concatenate([a, a], axis=1) for a in (cd, sda, sdb)],
                          axis=1)
    ident = jnp.concatenate([jnp.ones((tile, LANES), F32), jnp.zeros((tile, 2 * LANES), F32)] * 2, axis=1)
    return jnp.concatenate([tab, ident], axis=0)


def _block_diag_mean(seg_sizes):
    m = jnp.zeros((MXU_DIM, MXU_DIM), F32)
    off = 0
    for size, active in seg_sizes:
        if active:
            m = m.at[off:off + size, off:off + size].set(1.0 / size)
        off += size
    assert off == MXU_DIM
    return m.astype(BF16)


def _prep_weights(attn_norm_g, ffn_norm_g, w_in, mla_q_norm_g, mla_w_uq, mla_kv_norm_g, mla_w_ukv, mla_q_g,
                  mla_k_g, diff_q_g, diff_k_g, diff_sub_g, w_o_mla, w_o_diff, w_out, moe_w_gate, moe_w_up,
                  moe_w_down, sh_w_gate, sh_w_up, sh_w_down):
    depth = w_in.shape[0]
    d = D_MODEL
    z = lambda w: jnp.zeros((depth, d, w), F32)
    off_kr = MLA_KV_RANK
    off_kd = off_kr + MLA_ROPE
    off_vd = off_kd + DIFF_QK_W
    off_cq = off_vd + DIFF_HEADS * DIFF_V
    off_qd = off_cq + MLA_Q_RANK
    off_gate = off_qd + DIFF_QK_W
    w_in_a = jnp.concatenate([
        w_in[:, :, :off_kr], z(MLA_NOPE), w_in[:, :, off_kr:off_kd], z(HEAD_PAD - MLA_NOPE - MLA_ROPE),
        w_in[:, :, off_kd:off_gate]], axis=2).astype(BF16)
    w_gate = w_in[:, :, off_gate:].astype(BF16)

    ukv = mla_w_ukv.reshape(depth, MLA_KV_RANK, MLA_HEADS, MLA_NOPE + MLA_V)
    pad_k = jnp.zeros((depth, MLA_KV_RANK, MLA_HEADS, HEAD_PAD - MLA_NOPE), F32)
    pad_v = jnp.zeros((depth, MLA_KV_RANK, MLA_HEADS, HEAD_PAD - MLA_V), F32)
    w_ukv = jnp.concatenate([
        jnp.concatenate([ukv[..., :MLA_NOPE], pad_k], axis=-1).reshape(depth, MLA_KV_RANK, MLA_W),
        jnp.concatenate([ukv[..., MLA_NOPE:], pad_v], axis=-1).reshape(depth, MLA_KV_RANK, MLA_W)],
        axis=-1).astype(BF16)
    uq = mla_w_uq.reshape(depth, MLA_Q_RANK, MLA_HEADS, MLA_NOPE + MLA_ROPE)
    w_uq = jnp.pad(uq, ((0, 0), (0, 0), (0, 0), (0, HEAD_PAD - MLA_NOPE - MLA_ROPE))).reshape(
        depth, MLA_Q_RANK, MLA_W).astype(BF16)
    wom = w_o_mla.reshape(depth, MLA_HEADS, MLA_V, d)
    w_o_mla_p = jnp.pad(wom, ((0, 0), (0, 0), (0, HEAD_PAD - MLA_V), (0, 0))).reshape(depth, MLA_W, d).astype(BF16)

    def per_head(g96, scale):
        g = jnp.pad(g96 * scale, ((0, 0), (0, HEAD_PAD - MLA_NOPE - MLA_ROPE)))
        return jnp.tile(g, (1, MLA_HEADS)).reshape(depth, 1, MLA_W)

    k_nope_only = jnp.concatenate([mla_k_g[:, :MLA_NOPE], jnp.zeros((depth, MLA_ROPE), F32)], axis=1)
    g_k = per_head(k_nope_only, 1.0)
    g_kr = jnp.concatenate([jnp.zeros((depth, MLA_NOPE), F32), mla_k_g[:, MLA_NOPE:],
                            jnp.zeros((depth, HEAD_PAD - MLA_NOPE - MLA_ROPE), F32)], axis=1).reshape(depth, 1, LANES)
    g_q = per_head(mla_q_g, MLA_SCALE * LOG2E)
    g_kd = jnp.tile(diff_k_g.reshape(depth, 2 * DIFF_DH), (1, DIFF_HEADS)).reshape(depth, 1, DIFF_QK_W)
    g_qd = jnp.tile(diff_q_g.reshape(depth, 2 * DIFF_DH) * (DIFF_SCALE * LOG2E),
                    (1, DIFF_HEADS)).reshape(depth, 1, DIFF_QK_W)
    lam_inits = [0.8 - 0.6 * math.exp(-0.3 * l) for l in range(depth)]
    g_sub = diff_sub_g * (1.0 - jnp.asarray(lam_inits, F32))[:, None]

    def seg_norm2(g, n):
        return n * jnp.max(g * g, axis=-1)

    q_n2 = seg_norm2(mla_q_g[:, :MLA_NOPE], MLA_NOPE) + seg_norm2(mla_q_g[:, MLA_NOPE:], MLA_ROPE)
    k_n2 = seg_norm2(mla_k_g[:, :MLA_NOPE], MLA_NOPE) + seg_norm2(mla_k_g[:, MLA_NOPE:], MLA_ROPE)
    shift_m = (jnp.sqrt(q_n2 * k_n2) * (MLA_SCALE * LOG2E * SHIFT_MARGIN)).astype(BF16).astype(F32)
    shift_d = (jnp.sqrt(seg_norm2(diff_q_g, DIFF_DH) * seg_norm2(diff_k_g, DIFF_DH))
               * (DIFF_SCALE * LOG2E * SHIFT_MARGIN)).astype(BF16).astype(F32)
    flag_m = shift_m <= SAFE_SHIFT
    flag_d = jnp.max(shift_d, axis=-1) <= SAFE_SHIFT
    aug_lane = (jnp.arange(HEAD_PAD) == MLA_NOPE + MLA_ROPE).astype(F32)
    q_aug = jnp.tile(aug_lane[None, :] * jnp.where(flag_m, -shift_m, 0.0)[:, None], (1, MLA_HEADS))
    qd_aug = jnp.pad(jnp.where(flag_d[:, None], -shift_d, 0.0), ((0, 0), (0, LANES - 2)))

    v_one = jnp.tile((jnp.arange(HEAD_PAD) == MLA_V).astype(F32), MLA_HEADS).reshape(1, MLA_W)
    b_k = _block_diag_mean([(MLA_NOPE, True), (HEAD_PAD - MLA_NOPE, False)] * 2)
    b_q = _block_diag_mean([(MLA_NOPE, True), (MLA_ROPE, True), (HEAD_PAD - MLA_NOPE - MLA_ROPE, False)] * 2)
    b_d = _block_diag_mean([(DIFF_DH, True)] * 4)

    w_gu = jnp.concatenate([
        jnp.concatenate([sh_w_gate, sh_w_up], axis=-1)[:, None].astype(BF16),
        jnp.concatenate([moe_w_gate.astype(BF16), moe_w_up.astype(BF16)], axis=-1)], axis=1)
    w_dn = jnp.concatenate([sh_w_down[:, None].astype(BF16), moe_w_down.astype(BF16)], axis=1)

    layers = []
    for l in range(depth):
        layers.append({
            "g_attn": attn_norm_g[l].reshape(1, d), "g_ffn": ffn_norm_g[l].reshape(1, d),
            "w_in_a": w_in_a[l], "w_gate": w_gate[l], "w_ukv": w_ukv[l], "w_uq": w_uq[l],
            "g_kv": mla_kv_norm_g[l].reshape(1, MLA_KV_RANK), "g_cq": mla_q_norm_g[l].reshape(1, MLA_Q_RANK),
            "g_k": g_k[l], "g_kr": g_kr[l], "g_q": g_q[l], "g_kd": g_kd[l], "g_qd": g_qd[l],
            "g_sub": g_sub[l].reshape(1, DIFF_V), "b_k": b_k, "b_q": b_q, "b_d": b_d, "v_one": v_one,
            "w_o_mla": w_o_mla_p[l], "w_o_diff": w_o_diff[l].astype(BF16), "w_out": w_out[l].astype(BF16),
            "w_gu": w_gu[l], "w_dn": w_dn[l], "lam_init": lam_inits[l],
            "q_aug": q_aug[l].reshape(1, MLA_W), "qd_aug": qd_aug[l].reshape(1, LANES),
            "flag_m": flag_m[l].astype(jnp.int32).reshape(1), "flag_d": flag_d[l].astype(jnp.int32).reshape(1),
        })
    return layers


def kernel(x, c, ctx, c_ctx, ada_w, ada_b, attn_norm_g, ffn_norm_g, w_in, mla_q_norm_g, mla_w_uq, mla_kv_norm_g, mla_w_ukv, mla_q_g, mla_k_g, diff_q_g, diff_k_g, diff_lam, diff_sub_g, w_o_mla, w_o_diff, w_out, router_w, router_b, moe_w_gate, moe_w_up, moe_w_down, sh_w_gate, sh_w_up, sh_w_down):
    n_batch, seq, d = x.shape
    n_ctx = ctx.shape[1]
    depth = w_in.shape[0]
    assert d == D_MODEL and n_ctx == ROW_TILE and seq % GRID_W == 0
    assert seq % MLA_TQ == 0 and seq % DIFF_TQ == 0 and ((seq + n_ctx) % KV_CHUNK) % 16 == 0
    assert (n_batch * seq) % MOE_TILE == 0 and (n_batch * n_ctx) % MOE_TILE == 0
    lat_rows = n_batch * seq
    all_rows = lat_rows + n_batch * n_ctx

    layers = _prep_weights(attn_norm_g, ffn_norm_g, w_in, mla_q_norm_g, mla_w_uq, mla_kv_norm_g, mla_w_ukv,
                           mla_q_g, mla_k_g, diff_q_g, diff_k_g, diff_sub_g, w_o_mla, w_o_diff, w_out,
                           moe_w_gate, moe_w_up, moe_w_down, sh_w_gate, sh_w_up, sh_w_down)
    rope_tab = _rope_table(seq, ROW_TILE)
    rw = jnp.pad(router_w, ((0, 0), (1, LANES - 1 - N_EXPERTS)))
    rb = jnp.pad(router_b, (1, LANES - 1 - N_EXPERTS)).reshape(1, LANES)

    pad_rows = (-(n_batch + 1)) % 8
    c_all = jnp.concatenate([c, c_ctx[None, :], jnp.zeros((pad_rows, d), F32)], axis=0)
    mods_all = _mods_call(c_all, ada_w, ada_b).reshape(depth, c_all.shape[0], N_MOD, d)

    xf = jnp.concatenate([x.reshape(lat_rows, d), ctx.reshape(n_batch * n_ctx, d)], axis=0)
    for l in range(depth):
        wl = layers[l]
        mods = mods_all[l]
        k_m, v_m, q_m, k_d, v_d, q_d = _proj_call(xf, mods, wl, rope_tab, n_batch, seq)
        last = l == depth - 1
        o_m = _attn_call(wl["flag_m"], q_m, k_m, v_m, n_batch, seq, n_ctx, MLA_HEADS, 1, [], 0.0, MLA_TQ,
                         not last, "attn_mla")
        o_d = _attn_call(wl["flag_d"], q_d, k_d, v_d, n_batch, seq, n_ctx, DIFF_HEADS, 2,
                         [diff_lam[l], wl["g_sub"]], wl["lam_init"], DIFF_TQ, not last, "attn_diff")
        out_rows = lat_rows if last else all_rows
        xf, f_ext, pos, counts = _merge_call(xf, mods, o_m, o_d, wl, rw, rb, n_batch, seq, out_rows)
        y = _moe_sparse_call(f_ext, _moe_plan(pos, counts, out_rows), wl["w_gu"], wl["w_dn"], out_rows)
        xf = _residual_call(xf, y, mods, n_batch, seq, out_rows)
    return xf.reshape(n_batch, seq, d)
```

```python
import functools
import math

import jax
import jax.numpy as jnp
from jax import lax
from jax.experimental import pallas as pl
from jax.experimental.pallas import tpu as pltpu

F32 = jnp.float32
BF16 = jnp.bfloat16

D_MODEL = 1024
GRID_W = 64
EPS = 1e-6
ROPE_BASE = 10000.0
N_MOD = 6

MLA_HEADS = 8
MLA_NOPE = 64
MLA_ROPE = 32
MLA_V = 64
MLA_Q_RANK = 512
MLA_KV_RANK = 256
MLA_SCALE = 1.0 / math.sqrt(MLA_NOPE + MLA_ROPE)

DIFF_HEADS = 4
DIFF_DH = 64
DIFF_V = 2 * DIFF_DH
DIFF_SCALE = 1.0 / math.sqrt(DIFF_DH)

N_EXPERTS = 16
EXPERTS_PER_GROUP = 4
N_GROUPS = N_EXPERTS // EXPERTS_PER_GROUP
D_EXPERT = 512
PAIR_LO = (0, 0, 0, 1, 1, 2)
PAIR_HI = (1, 2, 3, 2, 3, 3)
N_PAIRS = len(PAIR_LO)
N_CLASSES = N_GROUPS * N_PAIRS

LANES = 128
MXU_DIM = 256
HEAD_PAD = LANES
MLA_W = MLA_HEADS * HEAD_PAD
DIFF_QK_W = DIFF_HEADS * 2 * DIFF_DH
DIFF_K_W = DIFF_HEADS * 2 * LANES
DIFF_V_PAD = 2 * LANES
DIFF_V_W = DIFF_HEADS * DIFF_V_PAD

A_CKV = 0
A_KR = A_CKV + MLA_KV_RANK
A_KD = A_KR + LANES
A_VD = A_KD + DIFF_QK_W
A_CQ = A_VD + DIFF_HEADS * DIFF_V
A_QD = A_CQ + MLA_Q_RANK
A_COLS = A_QD + DIFF_QK_W

ROPE_TAB_W = 6 * LANES
F_EXT_W = D_MODEL + LANES

ROW_TILE = 256
MOE_TILE = 512
SORT_TILE = 256
KV_CHUNK = 768
MLA_TQ = 2048
DIFF_TQ = 512
LOOP_STAGES = 4
LOG2E = math.log2(math.e)
VMEM_LIMIT = 56 * 1024 * 1024


def _sigmoid(x):
    return 1.0 / (1.0 + jnp.exp(-x))


def _dot(a, b):
    return jnp.dot(a, b, preferred_element_type=F32)


def _split_bf16(a):
    hi = a.astype(BF16)
    lo = (a - hi.astype(F32)).astype(BF16)
    return hi, lo


def _dot3(a, b):
    a_hi, a_lo = _split_bf16(a)
    b_hi, b_lo = _split_bf16(b)
    return _dot(a_hi, b_hi) + _dot(a_lo, b_hi) + _dot(a_hi, b_lo)


def _mods_kernel(c_ref, w_ref, b_ref, o_ref):
    c = c_ref[...]
    a = c * _sigmoid(c)
    o_ref[...] = _dot3(a, w_ref[...]) + b_ref[...]


def _mods_call(c_all, ada_w, ada_b):
    depth, d, n = ada_w.shape
    tn = 1536
    rows = c_all.shape[0]
    return pl.pallas_call(
        _mods_kernel,
        grid=(depth, n // tn),
        in_specs=[
            pl.BlockSpec((rows, d), lambda l, j: (0, 0)),
            pl.BlockSpec((None, d, tn), lambda l, j: (l, 0, j)),
            pl.BlockSpec((None, 1, tn), lambda l, j: (l, 0, j)),
        ],
        out_specs=pl.BlockSpec((None, rows, tn), lambda l, j: (l, 0, j)),
        out_shape=jax.ShapeDtypeStruct((depth, rows, n), F32),
        compiler_params=pltpu.CompilerParams(
            dimension_semantics=("parallel", "parallel"), vmem_limit_bytes=VMEM_LIMIT),
        name="adaln_mods",
    )(c_all, ada_w, ada_b.reshape(depth, 1, n))


def _rms(x, g):
    return x * lax.rsqrt(jnp.mean(x * x, axis=-1, keepdims=True) + EPS) * g


def _seg_mean_sq(x, b_ref):
    sq = (x * x).astype(BF16)
    b = b_ref[...]
    parts = [_dot(sq[:, c:c + MXU_DIM], b) for c in range(0, x.shape[1], MXU_DIM)]
    return parts[0] if len(parts) == 1 else jnp.concatenate(parts, axis=1)


def _rope(x, cos, sin_a, sin_b, half):
    return (x * cos + pltpu.roll(x, LANES - half, axis=1) * sin_a
            + pltpu.roll(x, half, axis=1) * sin_b)


def _proj_kernel(*refs, fused_residual):
    if fused_residual:
        x_ref, y_ref, modp_ref = refs[:3]
        refs = refs[3:]
        x = x_ref[...] + modp_ref[5:6, :] * y_ref[...]
    else:
        x = refs[0][...]
        refs = refs[1:]
    (mod_ref, g_ref, win_ref, wukv_ref, wuq_ref, gkv_ref, gcq_ref, gk_ref, gkr_ref, gq_ref, gkd_ref, gqd_ref,
     bk_ref, bq_ref, bd_ref, vone_ref, qaug_ref, qdaug_ref, rope_ref) = refs[:19]
    outs = refs[19:]
    if fused_residual:
        outs[0][...] = x
        outs = outs[1:]
    kmla_ref, vmla_ref, qmla_ref, kd_ref, vd_ref, qd_ref = outs
    h = _rms(x, g_ref[...]) * (1.0 + mod_ref[1:2, :]) + mod_ref[0:1, :]
    p = _dot(h.astype(BF16), win_ref[...])

    cos_m = rope_ref[:, 0 * LANES:1 * LANES]
    sin_ma = rope_ref[:, 1 * LANES:2 * LANES]
    sin_mb = rope_ref[:, 2 * LANES:3 * LANES]
    cos_d = rope_ref[:, 3 * LANES:4 * LANES]
    sin_da = rope_ref[:, 4 * LANES:5 * LANES]
    sin_db = rope_ref[:, 5 * LANES:6 * LANES]

    ckv = _rms(p[:, A_CKV:A_KR], gkv_ref[...])
    kv = _dot(ckv.astype(BF16), wukv_ref[...])
    kn = kv[:, :MLA_W]
    kn = kn * lax.rsqrt(_seg_mean_sq(kn, bk_ref) + EPS) * gk_ref[...]
    kr = p[:, A_KR:A_KD]
    kr = kr * lax.rsqrt(jnp.sum(kr * kr, axis=-1, keepdims=True) * (1.0 / MLA_ROPE) + EPS) * gkr_ref[...]
    kr = _rope(kr, cos_m, sin_ma, sin_mb, MLA_ROPE // 4)
    tm = x.shape[0]
    lane_blk = lax.broadcasted_iota(jnp.int32, (tm, LANES), 1)
    kr = kr + (lane_blk == MLA_NOPE + MLA_ROPE).astype(F32)
    for hd in range(MLA_HEADS):
        sl = slice(hd * HEAD_PAD, (hd + 1) * HEAD_PAD)
        kmla_ref[:, sl] = (kn[:, sl] + kr).astype(BF16)
    vmla_ref[...] = (kv[:, MLA_W:] + vone_ref[...]).astype(BF16)

    cq = _rms(p[:, A_CQ:A_QD], gcq_ref[...])
    q = _dot(cq.astype(BF16), wuq_ref[...])
    q = q * lax.rsqrt(_seg_mean_sq(q, bq_ref) + EPS) * gq_ref[...]
    for hd in range(MLA_HEADS):
        sl = slice(hd * HEAD_PAD, (hd + 1) * HEAD_PAD)
        qmla_ref[:, sl] = (_rope(q[:, sl], cos_m, sin_ma, sin_mb, MLA_ROPE // 4) + qaug_ref[:, sl]).astype(BF16)

    kd = p[:, A_KD:A_VD]
    kd = kd * lax.rsqrt(_seg_mean_sq(kd, bd_ref) + EPS) * gkd_ref[...]
    qd = p[:, A_QD:A_COLS]
    qd = qd * lax.rsqrt(_seg_mean_sq(qd, bd_ref) + EPS) * gqd_ref[...]
    ones_blk = (lane_blk == 0).astype(BF16)
    two_ones_blk = (lane_blk < 2).astype(BF16)
    qd_shift = jnp.broadcast_to(qdaug_ref[...], (tm, LANES)).astype(BF16)
    for hd in range(DIFF_HEADS):
        sl = slice(hd * LANES, (hd + 1) * LANES)
        lo = slice(2 * hd * LANES, (2 * hd + 1) * LANES)
        hi = slice((2 * hd + 1) * LANES, (2 * hd + 2) * LANES)
        kd_ref[:, lo] = _rope(kd[:, sl], cos_d, sin_da, sin_db, DIFF_DH // 4).astype(BF16)
        kd_ref[:, hi] = two_ones_blk
        qd_ref[:, lo] = _rope(qd[:, sl], cos_d, sin_da, sin_db, DIFF_DH // 4).astype(BF16)
        qd_ref[:, hi] = qd_shift

    vd = p[:, A_VD:A_CQ]
    for hd in range(DIFF_HEADS):
        vd_ref[:, hd * DIFF_V_PAD:hd * DIFF_V_PAD + DIFF_V] = vd[:, hd * DIFF_V:(hd + 1) * DIFF_V].astype(BF16)
        vd_ref[:, hd * DIFF_V_PAD + DIFF_V:(hd + 1) * DIFF_V_PAD] = ones_blk


def _row_class_maps(n_lat_tiles, tiles_per_batch, n_batch):
    def mod_map(i):
        return (jnp.where(i < n_lat_tiles, i // tiles_per_batch, n_batch), 0, 0)

    def rope_map(i):
        return (jnp.where(i < n_lat_tiles, i % tiles_per_batch, tiles_per_batch), 0)

    return mod_map, rope_map


def _const_spec(shape):
    return pl.BlockSpec(shape, lambda i: tuple(0 for _ in shape))


def _proj_call(x_flat, mods, wl, rope_tab, n_batch, seq, prev=None):
    rows = x_flat.shape[0]
    tm = ROW_TILE
    n_lat_tiles = n_batch * seq // tm
    mod_map, rope_map = _row_class_maps(n_lat_tiles, seq // tm, n_batch)
    row_spec = lambda w: pl.BlockSpec((tm, w), lambda i: (i, 0))
    tiles_per_batch = seq // tm

    def kv_map(i):
        lat = i + i // tiles_per_batch + 1
        ctx = (i - n_lat_tiles) * (tiles_per_batch + 1)
        return (jnp.where(i < n_lat_tiles, lat, ctx), 0)

    kv_spec = lambda w: pl.BlockSpec((tm, w), kv_map)
    consts = [wl["g_attn"], wl["w_in_a"], wl["w_ukv"], wl["w_uq"], wl["g_kv"], wl["g_cq"], wl["g_k"], wl["g_kr"],
              wl["g_q"], wl["g_kd"], wl["g_qd"], wl["b_k"], wl["b_q"], wl["b_d"], wl["v_one"], wl["q_aug"],
              wl["qd_aug"]]
    out_w = [MLA_W, MLA_W, MLA_W, DIFF_K_W, DIFF_V_W, DIFF_K_W]
    out_specs = [kv_spec(MLA_W), kv_spec(MLA_W), row_spec(MLA_W), kv_spec(DIFF_K_W), kv_spec(DIFF_V_W),
                 row_spec(DIFF_K_W)]
    mod_spec = pl.BlockSpec((None, N_MOD, D_MODEL), mod_map)
    lead_specs, lead_args = [row_spec(D_MODEL)], [x_flat]
    out_shape = [jax.ShapeDtypeStruct((rows, w), BF16) for w in out_w]
    if prev is not None:
        lead_specs += [row_spec(D_MODEL), mod_spec]
        lead_args += list(prev)
        out_specs = [row_spec(D_MODEL)] + out_specs
        out_shape = [jax.ShapeDtypeStruct((rows, D_MODEL), F32)] + out_shape
    return pl.pallas_call(
        functools.partial(_proj_kernel, fused_residual=prev is not None),
        grid=(rows // tm,),
        in_specs=lead_specs + [mod_spec] + [_const_spec(c.shape) for c in consts]
        + [pl.BlockSpec((tm, ROPE_TAB_W), rope_map)],
        out_specs=out_specs,
        out_shape=out_shape,
        compiler_params=pltpu.CompilerParams(
            dimension_semantics=("parallel",), vmem_limit_bytes=VMEM_LIMIT),
        name="proj_qkv",
    )(*lead_args, mods, *consts, rope_tab)


SAFE_SHIFT = 50.0
SHIFT_MARGIN = 1.02


def _attn_kernel(flag_ref, *refs, n_maps, tq, tk, chunks, lam_init, aliased_out):
    refs = list(refs)
    q_ref, k_ref, v_ref = refs[:3]
    n_in = 3 + (2 if n_maps == 2 else 0) + (1 if aliased_out else 0)
    o_ref = refs[n_in]
    if n_maps == 2:
        lam_ref, subg_ref = refs[3:5]
        qs_ref, s_ref, p_ref, m_ref, a_ref, acc_ref = refs[n_in + 1:]
        qf = q_ref[...].astype(F32)
        lane = lax.broadcasted_iota(jnp.int32, qf.shape, 1)
        keep0 = (lane < DIFF_DH) | (lane == 2 * DIFF_DH)
        keep1 = ((lane >= DIFF_DH) & (lane < 2 * DIFF_DH)) | (lane == 2 * DIFF_DH + 1)
        qs_ref[0:tq, :] = jnp.where(keep0, qf, 0.0).astype(BF16)
        qs_ref[tq:2 * tq, :] = jnp.where(keep1, qf, 0.0).astype(BF16)
        q_src = qs_ref
    else:
        s_ref, p_ref, m_ref, a_ref, acc_ref = refs[n_in + 1:]
        q_src = q_ref
    vw = v_ref.shape[1]
    sum_col = vw // 2

    head, n_mid, tail = chunks
    n_chunks = 1 + n_mid + (1 if tail else 0)

    def geom(j):
        if isinstance(j, int):
            if j == 0:
                return 0, head
            if j <= n_mid:
                return head + (j - 1) * tk, tk
            return head + n_mid * tk, tail
        return pl.multiple_of(head + (j - 1) * tk, math.gcd(head, tk)), tk

    def rows_of(ref, j):
        off, size = geom(j)
        return ref[pl.ds(off, size), :]

    def scores(j, slot):
        size = geom(j)[1]
        s_ref[slot, :, :size] = lax.dot_general(q_src[...], rows_of(k_ref, j), (((1,), (1,)), ((), ())),
                                                preferred_element_type=F32)

    def probs_shifted(j, slot):
        size = geom(j)[1]
        p_ref[slot, :, :size] = jnp.exp2(s_ref[slot, :, :size]).astype(BF16)

    def accumulate_plain(j, slot):
        size = geom(j)[1]
        r = _dot(p_ref[slot, :, :size], rows_of(v_ref, j))
        if isinstance(j, int) and j == 0:
            acc_ref[...] = r
        else:
            acc_ref[...] += r

    def probs_online(j, slot):
        size = geom(j)[1]
        s = s_ref[slot, :, :size]
        mx = jnp.max(s, axis=-1, keepdims=True)
        if isinstance(j, int) and j == 0:
            m_new = mx
        else:
            m_old = m_ref[...]
            m_new = jnp.maximum(m_old, mx)
            a_ref[slot] = jnp.exp2(m_old - m_new)
        m_ref[...] = m_new
        p_ref[slot, :, :size] = jnp.exp2(s - m_new).astype(BF16)

    def accumulate_online(j, slot):
        size = geom(j)[1]
        r = _dot(p_ref[slot, :, :size], rows_of(v_ref, j))
        if isinstance(j, int) and j == 0:
            acc_ref[...] = r
        else:
            acc_ref[...] = acc_ref[...] * a_ref[slot] + r

    def sweep(probs, accumulate):
        def stage(j, slot, with_scores):
            probs(j, slot)
            accumulate(j - 1, 1 - slot)
            if with_scores:
                scores(j + 1, 1 - slot)

        scores(0, 0)
        if n_chunks > 1:
            scores(1, 1)
        probs(0, 0)
        if n_chunks > 1:
            stage(1, 1, n_chunks > 2)
            n_loop = max(n_mid - 2, 0) // LOOP_STAGES

            def group(i, carry):
                j0 = 2 + LOOP_STAGES * i
                for t in range(LOOP_STAGES):
                    stage(j0 + t, t % 2, True)
                return carry

            lax.fori_loop(0, n_loop, group, 0)
            for j in range(2 + LOOP_STAGES * n_loop, n_chunks):
                stage(j, j % 2, j + 1 < n_chunks)
        accumulate(n_chunks - 1, (n_chunks - 1) % 2)

    @pl.when(flag_ref[0] == 1)
    def _():
        sweep(probs_shifted, accumulate_plain)

    @pl.when(flag_ref[0] != 1)
    def _():
        sweep(probs_online, accumulate_online)

    acc = acc_ref[...]
    inv_l = 1.0 / acc[:, sum_col:sum_col + 1]
    if n_maps == 1:
        o_ref[...] = (acc * inv_l).astype(o_ref.dtype)
    else:
        lf = lam_ref[...]
        lam = (jnp.exp(jnp.sum(lf[0:1, :] * lf[1:2, :], axis=-1, keepdims=True))
               - jnp.exp(jnp.sum(lf[2:3, :] * lf[3:4, :], axis=-1, keepdims=True)) + lam_init)
        o = acc[:, :DIFF_V] * inv_l
        o = o[:tq] - lam * o[tq:]
        o_ref[...] = _rms(o, subg_ref[...]).astype(o_ref.dtype)


def _attn_scratch(n_maps, tq, tk, kw, vw):
    r = n_maps * tq
    qs = [pltpu.VMEM((r, kw), BF16)] if n_maps == 2 else []
    return qs + [pltpu.VMEM((2, r, tk), F32), pltpu.VMEM((2, r, tk), BF16), pltpu.VMEM((r, 1), F32),
                 pltpu.VMEM((2, r, 1), F32), pltpu.VMEM((r, vw), F32)]


def _attn_call(flag, q, k, v, n_batch, seq, n_ctx, n_heads, n_maps, extra, lam_init, tq, with_ctx_queries, name):
    rows = q.shape[0]
    vw = v.shape[1] // n_heads
    kw = k.shape[1] // n_heads
    qw = q.shape[1] // n_heads
    keys = seq + n_ctx
    tk = KV_CHUNK
    nq = seq // tq
    assert seq % tq == 0
    chunks = (tk, keys // tk - 1, keys % tk)
    ctx_blk0 = n_batch * seq // n_ctx
    blocks_per_batch = keys // n_ctx
    extra_specs = [pl.BlockSpec(e.shape, lambda *_: (0, 0)) for e in extra]
    out_shape = jax.ShapeDtypeStruct((rows, n_heads * LANES), BF16)

    kern = functools.partial(_attn_kernel, n_maps=n_maps, tq=tq, tk=tk, chunks=chunks, lam_init=lam_init,
                             aliased_out=False)
    out = pl.pallas_call(
        kern,
        grid_spec=pltpu.PrefetchScalarGridSpec(
            num_scalar_prefetch=1,
            grid=(n_batch, n_heads, nq),
            in_specs=[pl.BlockSpec((tq, qw), lambda b, h, qi, f: (b * nq + qi, h)),
                      pl.BlockSpec((keys, kw), lambda b, h, qi, f: (b, h)),
                      pl.BlockSpec((keys, vw), lambda b, h, qi, f: (b, h))] + extra_specs,
            out_specs=pl.BlockSpec((tq, LANES), lambda b, h, qi, f: (b * nq + qi, h)),
            scratch_shapes=_attn_scratch(n_maps, tq, tk, kw, vw)),
        out_shape=out_shape,
        compiler_params=pltpu.CompilerParams(
            dimension_semantics=("parallel", "parallel", "arbitrary"), vmem_limit_bytes=VMEM_LIMIT),
        name=name,
    )(flag, q, k, v, *extra)
    if not with_ctx_queries:
        return out

    kern_c = functools.partial(_attn_kernel, n_maps=n_maps, tq=n_ctx, tk=n_ctx, chunks=(n_ctx, 0, 0),
                               lam_init=lam_init, aliased_out=True)
    return pl.pallas_call(
        kern_c,
        grid_spec=pltpu.PrefetchScalarGridSpec(
            num_scalar_prefetch=1,
            grid=(n_batch, n_heads),
            in_specs=[pl.BlockSpec((n_ctx, qw), lambda b, h, f: (ctx_blk0 + b, h)),
                      pl.BlockSpec((n_ctx, kw), lambda b, h, f: (b * blocks_per_batch, h)),
                      pl.BlockSpec((n_ctx, vw), lambda b, h, f: (b * blocks_per_batch, h))] + extra_specs
            + [pl.BlockSpec(memory_space=pl.ANY)],
            out_specs=pl.BlockSpec((n_ctx, LANES), lambda b, h, f: (ctx_blk0 + b, h)),
            scratch_shapes=_attn_scratch(n_maps, n_ctx, n_ctx, kw, vw)),
        out_shape=out_shape,
        input_output_aliases={4 + len(extra): 0},
        compiler_params=pltpu.CompilerParams(
            dimension_semantics=("parallel", "parallel"), vmem_limit_bytes=VMEM_LIMIT),
        name=name + "_ctx",
    )(flag, q, k, v, *extra, out)


def _route(logits, bias):
    lane = lax.broadcasted_iota(jnp.int32, logits.shape, 1)
    is_exp = (lane >= 1) & (lane <= N_EXPERTS)
    grp = (lane - 1) >> 2
    neg = jnp.float32(-jnp.inf)
    big = jnp.int32(1 << 20)
    aff = _sigmoid(logits)
    sel = jnp.where(is_exp, aff + bias, neg)

    def first_argmax(v):
        mx = jnp.max(v, axis=-1, keepdims=True)
        ix = jnp.min(jnp.where(v == mx, lane, big), axis=-1, keepdims=True)
        return mx, ix

    best = None
    for g in range(N_GROUPS):
        vg = jnp.where(grp == g, sel, neg)
        m1, i1 = first_argmax(vg)
        m2, i2 = first_argmax(jnp.where(lane == i1, neg, vg))
        score = m1 + m2
        if best is None:
            best = (score, i1, i2)
        else:
            upd = score > best[0]
            best = (jnp.where(upd, score, best[0]), jnp.where(upd, i1, best[1]), jnp.where(upd, i2, best[2]))
    _, i1, i2 = best
    a1 = jnp.sum(jnp.where(lane == i1, aff, 0.0), axis=-1, keepdims=True)
    a2 = jnp.sum(jnp.where(lane == i2, aff, 0.0), axis=-1, keepdims=True)
    tot = a1 + a2
    e1 = i1 - 1
    e2 = i2 - 1
    swap = e2 < e1
    lo = jnp.where(swap, e2, e1)
    hi = jnp.where(swap, e1, e2)
    w_lo = jnp.where(swap, a2, a1) / tot
    w_hi = jnp.where(swap, a1, a2) / tot
    l = lo & (EXPERTS_PER_GROUP - 1)
    h = hi & (EXPERTS_PER_GROUP - 1)
    cls = (lo >> 2) * N_PAIRS + ((l * (7 - l)) >> 1) + (h - l - 1)
    return cls, w_lo, w_hi


def _merge_kernel(x_ref, mod_ref, ga_ref, gf_ref, om_ref, od_ref, wgate_ref, wom_ref, wod_ref, wout_ref,
                  rw_ref, rb_ref, tri_ref, xo_ref, f_ref, pos_ref, cnt_ref, *, cap):
    @pl.when(pl.program_id(0) == 0)
    def _():
        cnt_ref[...] = jnp.zeros_like(cnt_ref)

    x = x_ref[...]
    h = _rms(x, ga_ref[...]) * (1.0 + mod_ref[1:2, :]) + mod_ref[0:1, :]
    gates = _sigmoid(_dot(h.astype(BF16), wgate_ref[...]))
    y = (gates[:, :D_MODEL] * _dot(om_ref[...], wom_ref[...])
         + gates[:, D_MODEL:] * _dot(od_ref[...], wod_ref[...]))
    xn = x + mod_ref[2:3, :] * _dot(y.astype(BF16), wout_ref[...])
    xo_ref[...] = xn
    f = _rms(xn, gf_ref[...]) * (1.0 + mod_ref[4:5, :]) + mod_ref[3:4, :]
    cls, w_lo, w_hi = _route(_dot3(f, rw_ref[...]), rb_ref[...])
    lane = lax.broadcasted_iota(jnp.int32, (x.shape[0], LANES), 1)
    f_ref[:, :D_MODEL] = f
    f_ref[:, D_MODEL:] = jnp.where(lane == 0, w_lo, jnp.where(lane == 1, w_hi, 0.0))
    onehot = (lane == cls).astype(F32)
    before = _dot(tri_ref[...], onehot.astype(BF16)) + cnt_ref[...]
    rank = jnp.sum(onehot * before, axis=-1, keepdims=True)
    cnt_ref[...] += jnp.sum(onehot, axis=0, keepdims=True)
    pos = cls.astype(F32) * float(cap) + rank
    pos_ref[...] = jnp.where(lane == 0, pos, 0.0)


def _merge_call(x_flat, mods, o_mla, o_diff, wl, router_w, router_b, n_batch, seq, rows):
    tm = ROW_TILE
    mod_map, _ = _row_class_maps(n_batch * seq // tm, seq // tm, n_batch)
    row_spec = lambda w: pl.BlockSpec((tm, w), lambda i: (i, 0))
    tri = (jnp.arange(tm)[:, None] > jnp.arange(tm)[None, :]).astype(BF16)
    consts = [wl["w_gate"], wl["w_o_mla"], wl["w_o_diff"], wl["w_out"], router_w, router_b, tri]
    return pl.pallas_call(
        functools.partial(_merge_kernel, cap=rows),
        grid=(rows // tm,),
        in_specs=[row_spec(D_MODEL), pl.BlockSpec((None, N_MOD, D_MODEL), mod_map),
                  _const_spec(wl["g_attn"].shape), _const_spec(wl["g_ffn"].shape),
                  row_spec(MLA_W), row_spec(DIFF_HEADS * LANES)]
        + [_const_spec(c.shape) for c in consts],
        out_specs=[row_spec(D_MODEL), row_spec(F_EXT_W), row_spec(LANES), _const_spec((1, LANES))],
        out_shape=[jax.ShapeDtypeStruct((rows, D_MODEL), F32), jax.ShapeDtypeStruct((rows, F_EXT_W), F32),
                   jax.ShapeDtypeStruct((rows, LANES), F32), jax.ShapeDtypeStruct((1, LANES), F32)],
        compiler_params=pltpu.CompilerParams(
            dimension_semantics=("arbitrary",), vmem_limit_bytes=VMEM_LIMIT),
        name="merge_router",
    )(x_flat, mods, wl["g_attn"], wl["g_ffn"], o_mla, o_diff, *consts)


def _moe_plan(pos_rec, counts_rec, n_rows):
    t = SORT_TILE
    tiles_per_bucket = n_rows // t
    counts = counts_rec[0, :N_CLASSES].astype(jnp.int32)
    pos = pos_rec[:, 0].astype(jnp.int32)
    src = jnp.zeros((N_CLASSES * n_rows,), jnp.int32).at[pos].set(jnp.arange(n_rows, dtype=jnp.int32))
    class_tiles = (counts + t - 1) // t
    tile_end = jnp.cumsum(class_tiles)
    n_tiles = tiles_per_bucket + N_CLASSES
    tile = jnp.arange(n_tiles, dtype=jnp.int32)
    tile_cls = jnp.minimum(jnp.searchsorted(tile_end, tile, side="right", method="compare_all"),
                           N_CLASSES - 1).astype(jnp.int32)
    in_class = tile - (tile_end - class_tiles)[tile_cls]
    n_used = tile_end[-1].reshape(1)
    used = tile < n_used[0]
    blk = jnp.where(used, tile_cls * tiles_per_bucket + in_class, 0)
    n_valid = jnp.where(used, jnp.clip(counts[tile_cls] - in_class * t, 0, t), 0)
    base = (tile_cls // N_PAIRS) * EXPERTS_PER_GROUP
    pair = tile_cls % N_PAIRS
    ea = base + jnp.asarray(PAIR_LO, jnp.int32)[pair]
    eb = base + jnp.asarray(PAIR_HI, jnp.int32)[pair]
    return ea, eb, n_used, blk, n_valid, src.reshape(N_CLASSES * tiles_per_bucket, 1, t)


def _moe_sparse_kernel(ea_ref, eb_ref, nu_ref, blk_ref, nv_ref, src0_ref, srcn_ref, srcc_ref, f_hbm,
                       wgu_s, wd_s, wgu_a, wd_a, wgu_b, wd_b, y_hbm, xbuf, ybuf, gsem, ssem, *, n_rows):
    del ea_ref, eb_ref, blk_ref
    t = SORT_TILE
    i = pl.program_id(0)
    n_used = nu_ref[0]
    n_valid = nv_ref[i]
    slot = i % 2

    def start_gather(idx_ref, s):
        for r in range(t):
            pltpu.make_async_copy(f_hbm.at[pl.ds(idx_ref[0, r], 1), :], xbuf.at[s, pl.ds(r, 1), :],
                                  gsem.at[s]).start()

    def gather_done(s):
        return pltpu.make_async_copy(f_hbm.at[pl.ds(0, t), :], xbuf.at[s], gsem.at[s])

    def scatter_done(s):
        return pltpu.make_async_copy(ybuf.at[s], y_hbm.at[pl.ds(0, t), :], ssem.at[s])

    @pl.when(i == 0)
    def _():
        start_gather(src0_ref, 0)

    @pl.when(i < n_used)
    def _():
        gather_done(slot).wait()
        start_gather(srcn_ref, 1 - slot)
        x = xbuf[slot, :, :D_MODEL].astype(BF16)
        w = xbuf[slot, :, D_MODEL:]

        def ffn(wgu_ref, wd_ref, scale):
            gu = _dot(x, wgu_ref[...])
            g = gu[:, :D_EXPERT]
            he = g * _sigmoid(g) * gu[:, D_EXPERT:]
            if scale is not None:
                he = he * scale
            return _dot(he.astype(BF16), wd_ref[...])

        ybuf[slot] = ffn(wgu_s, wd_s, None) + ffn(wgu_a, wd_a, w[:, 0:1]) + ffn(wgu_b, wd_b, w[:, 1:2])
        for r in range(t):
            dst = jnp.where(r < n_valid, srcc_ref[0, r], n_rows + slot * t + r)
            pltpu.make_async_copy(ybuf.at[slot, pl.ds(r, 1), :], y_hbm.at[pl.ds(dst, 1), :],
                                  ssem.at[slot]).start()

        @pl.when(i >= 1)
        def _():
            scatter_done(1 - slot).wait()

        @pl.when(i == n_used - 1)
        def _():
            scatter_done(slot).wait()
            gather_done(1 - slot).wait()


def _moe_sparse_call(f_ext, plan, w_gu, w_down, n_rows):
    ea, eb, n_used, blk, n_valid, src = plan
    t = SORT_TILE
    n_tiles = blk.shape[0]
    smem_spec = lambda imap: pl.BlockSpec((None, 1, t), imap, memory_space=pltpu.SMEM)
    wspec = lambda shape, imap: pl.BlockSpec((None,) + shape, imap)
    gu_shape = (D_MODEL, 2 * D_EXPERT)
    dn_shape = (D_EXPERT, D_MODEL)
    grid_spec = pltpu.PrefetchScalarGridSpec(
        num_scalar_prefetch=5,
        grid=(n_tiles,),
        in_specs=[
            smem_spec(lambda i, ea, eb, nu, blk, nv: (blk[0], 0, 0)),
            smem_spec(lambda i, ea, eb, nu, blk, nv: (blk[jnp.minimum(i + 1, n_tiles - 1)], 0, 0)),
            smem_spec(lambda i, ea, eb, nu, blk, nv: (blk[i], 0, 0)),
            pl.BlockSpec(memory_space=pl.ANY),
            wspec(gu_shape, lambda i, ea, eb, nu, blk, nv: (0, 0, 0)),
            wspec(dn_shape, lambda i, ea, eb, nu, blk, nv: (0, 0, 0)),
            wspec(gu_shape, lambda i, ea, eb, nu, blk, nv: (ea[i] + 1, 0, 0)),
            wspec(dn_shape, lambda i, ea, eb, nu, blk, nv: (ea[i] + 1, 0, 0)),
            wspec(gu_shape, lambda i, ea, eb, nu, blk, nv: (eb[i] + 1, 0, 0)),
            wspec(dn_shape, lambda i, ea, eb, nu, blk, nv: (eb[i] + 1, 0, 0)),
        ],
        out_specs=pl.BlockSpec(memory_space=pl.ANY),
        scratch_shapes=[pltpu.VMEM((2, t, F_EXT_W), F32), pltpu.VMEM((2, t, D_MODEL), F32),
                        pltpu.SemaphoreType.DMA((2,)), pltpu.SemaphoreType.DMA((2,))],
    )
    return pl.pallas_call(
        functools.partial(_moe_sparse_kernel, n_rows=n_rows),
        grid_spec=grid_spec,
        out_shape=jax.ShapeDtypeStruct((n_rows + 2 * t, D_MODEL), F32),
        compiler_params=pltpu.CompilerParams(
            dimension_semantics=("arbitrary",), vmem_limit_bytes=VMEM_LIMIT),
        name="moe_sorted",
    )(ea, eb, n_used, blk, n_valid, src, src, src, f_ext, w_gu, w_down, w_gu, w_down, w_gu, w_down)


def _residual_kernel(x_ref, y_ref, mod_ref, o_ref):
    o_ref[...] = x_ref[...] + mod_ref[5:6, :] * y_ref[...]


def _residual_call(x_flat, y, mods, n_batch, seq, out_rows):
    tm = MOE_TILE
    mod_map, _ = _row_class_maps(n_batch * seq // tm, seq // tm, n_batch)
    row_spec = pl.BlockSpec((tm, D_MODEL), lambda i: (i, 0))
    return pl.pallas_call(
        _residual_kernel,
        grid=(out_rows // tm,),
        in_specs=[row_spec, row_spec, pl.BlockSpec((None, N_MOD, D_MODEL), mod_map)],
        out_specs=row_spec,
        out_shape=jax.ShapeDtypeStruct((out_rows, D_MODEL), F32),
        compiler_params=pltpu.CompilerParams(
            dimension_semantics=("parallel",), vmem_limit_bytes=VMEM_LIMIT),
        name="ffn_residual",
    )(x_flat, y, mods)


def _axial_angles(seq, d_rope):
    d_axis = d_rope // 2
    inv = ROPE_BASE ** (-jnp.arange(0, d_axis, 2, dtype=F32) / d_axis)
    t = jnp.arange(seq, dtype=jnp.int32)
    r = (t // GRID_W).astype(F32)
    col = (t % GRID_W).astype(F32)
    ang_r = r[:, None] * inv
    ang_c = col[:, None] * inv
    cos = jnp.concatenate([jnp.cos(ang_r)] * 2 + [jnp.cos(ang_c)] * 2, axis=1)
    sin = jnp.concatenate([jnp.sin(ang_r)] * 2 + [jnp.sin(ang_c)] * 2, axis=1)
    nf = d_rope // 4
    first = jnp.concatenate([jnp.ones((nf,), F32), jnp.zeros((nf,), F32)] * 2)
    return cos, -sin * first, sin * (1.0 - first)


def _rope_table(seq, tile):
    cm, sma, smb = _axial_angles(seq, MLA_ROPE)
    pad_l = jnp.zeros((seq, MLA_NOPE), F32)
    pad_r = jnp.zeros((seq, HEAD_PAD - MLA_NOPE - MLA_ROPE), F32)
    cos_m = jnp.concatenate([pad_l + 1.0, cm, pad_r + 1.0], axis=1)
    sin_ma = jnp.concatenate([pad_l, sma, pad_r], axis=1)
    sin_mb = jnp.concatenate([pad_l, smb, pad_r], axis=1)
    cd, sda, sdb = _axial_angles(seq, DIFF_DH)
    tab = jnp.concatenate([cos_m, sin_ma, sin_mb] + [jnp.concatenate([a, a], axis=1) for a in (cd, sda, sdb)],
                          axis=1)
    ident = jnp.concatenate([jnp.ones((tile, LANES), F32), jnp.zeros((tile, 2 * LANES), F32)] * 2, axis=1)
    return jnp.concatenate([tab, ident], axis=0)


def _block_diag_mean(seg_sizes):
    m = jnp.zeros((MXU_DIM, MXU_DIM), F32)
    off = 0
    for size, active in seg_sizes:
        if active:
            m = m.at[off:off + size, off:off + size].set(1.0 / size)
        off += size
    assert off == MXU_DIM
    return m.astype(BF16)


def _prep_weights(attn_norm_g, ffn_norm_g, w_in, mla_q_norm_g, mla_w_uq, mla_kv_norm_g, mla_w_ukv, mla_q_g,
                  mla_k_g, diff_q_g, diff_k_g, diff_sub_g, w_o_mla, w_o_diff, w_out, moe_w_gate, moe_w_up,
                  moe_w_down, sh_w_gate, sh_w_up, sh_w_down):
    depth = w_in.shape[0]
    d = D_MODEL
    z = lambda w: jnp.zeros((depth, d, w), F32)
    off_kr = MLA_KV_RANK
    off_kd = off_kr + MLA_ROPE
    off_vd = off_kd + DIFF_QK_W
    off_cq = off_vd + DIFF_HEADS * DIFF_V
    off_qd = off_cq + MLA_Q_RANK
    off_gate = off_qd + DIFF_QK_W
    w_in_a = jnp.concatenate([
        w_in[:, :, :off_kr], z(MLA_NOPE), w_in[:, :, off_kr:off_kd], z(HEAD_PAD - MLA_NOPE - MLA_ROPE),
        w_in[:, :, off_kd:off_gate]], axis=2).astype(BF16)
    w_gate = w_in[:, :, off_gate:].astype(BF16)

    ukv = mla_w_ukv.reshape(depth, MLA_KV_RANK, MLA_HEADS, MLA_NOPE + MLA_V)
    pad_k = jnp.zeros((depth, MLA_KV_RANK, MLA_HEADS, HEAD_PAD - MLA_NOPE), F32)
    pad_v = jnp.zeros((depth, MLA_KV_RANK, MLA_HEADS, HEAD_PAD - MLA_V), F32)
    w_ukv = jnp.concatenate([
        jnp.concatenate([ukv[..., :MLA_NOPE], pad_k], axis=-1).reshape(depth, MLA_KV_RANK, MLA_W),
        jnp.concatenate([ukv[..., MLA_NOPE:], pad_v], axis=-1).reshape(depth, MLA_KV_RANK, MLA_W)],
        axis=-1).astype(BF16)
    uq = mla_w_uq.reshape(depth, MLA_Q_RANK, MLA_HEADS, MLA_NOPE + MLA_ROPE)
    w_uq = jnp.pad(uq, ((0, 0), (0, 0), (0, 0), (0, HEAD_PAD - MLA_NOPE - MLA_ROPE))).reshape(
        depth, MLA_Q_RANK, MLA_W).astype(BF16)
    wom = w_o_mla.reshape(depth, MLA_HEADS, MLA_V, d)
    w_o_mla_p = jnp.pad(wom, ((0, 0), (0, 0), (0, HEAD_PAD - MLA_V), (0, 0))).reshape(depth, MLA_W, d).astype(BF16)

    def per_head(g96, scale):
        g = jnp.pad(g96 * scale, ((0, 0), (0, HEAD_PAD - MLA_NOPE - MLA_ROPE)))
        return jnp.tile(g, (1, MLA_HEADS)).reshape(depth, 1, MLA_W)

    k_nope_only = jnp.concatenate([mla_k_g[:, :MLA_NOPE], jnp.zeros((depth, MLA_ROPE), F32)], axis=1)
    g_k = per_head(k_nope_only, 1.0)
    g_kr = jnp.concatenate([jnp.zeros((depth, MLA_NOPE), F32), mla_k_g[:, MLA_NOPE:],
                            jnp.zeros((depth, HEAD_PAD - MLA_NOPE - MLA_ROPE), F32)], axis=1).reshape(depth, 1, LANES)
    g_q = per_head(mla_q_g, MLA_SCALE * LOG2E)
    g_kd = jnp.tile(diff_k_g.reshape(depth, 2 * DIFF_DH), (1, DIFF_HEADS)).reshape(depth, 1, DIFF_QK_W)
    g_qd = jnp.tile(diff_q_g.reshape(depth, 2 * DIFF_DH) * (DIFF_SCALE * LOG2E),
                    (1, DIFF_HEADS)).reshape(depth, 1, DIFF_QK_W)
    lam_inits = [0.8 - 0.6 * math.exp(-0.3 * l) for l in range(depth)]
    g_sub = diff_sub_g * (1.0 - jnp.asarray(lam_inits, F32))[:, None]

    def seg_norm2(g, n):
        return n * jnp.max(g * g, axis=-1)

    q_n2 = seg_norm2(mla_q_g[:, :MLA_NOPE], MLA_NOPE) + seg_norm2(mla_q_g[:, MLA_NOPE:], MLA_ROPE)
    k_n2 = seg_norm2(mla_k_g[:, :MLA_NOPE], MLA_NOPE) + seg_norm2(mla_k_g[:, MLA_NOPE:], MLA_ROPE)
    shift_m = (jnp.sqrt(q_n2 * k_n2) * (MLA_SCALE * LOG2E * SHIFT_MARGIN)).astype(BF16).astype(F32)
    shift_d = (jnp.sqrt(seg_norm2(diff_q_g, DIFF_DH) * seg_norm2(diff_k_g, DIFF_DH))
               * (DIFF_SCALE * LOG2E * SHIFT_MARGIN)).astype(BF16).astype(F32)
    flag_m = shift_m <= SAFE_SHIFT
    flag_d = jnp.max(shift_d, axis=-1) <= SAFE_SHIFT
    aug_lane = (jnp.arange(HEAD_PAD) == MLA_NOPE + MLA_ROPE).astype(F32)
    q_aug = jnp.tile(aug_lane[None, :] * jnp.where(flag_m, -shift_m, 0.0)[:, None], (1, MLA_HEADS))
    qd_aug = jnp.pad(jnp.where(flag_d[:, None], -shift_d, 0.0), ((0, 0), (0, LANES - 2)))

    v_one = jnp.tile((jnp.arange(HEAD_PAD) == MLA_V).astype(F32), MLA_HEADS).reshape(1, MLA_W)
    b_k = _block_diag_mean([(MLA_NOPE, True), (HEAD_PAD - MLA_NOPE, False)] * 2)
    b_q = _block_diag_mean([(MLA_NOPE, True), (MLA_ROPE, True), (HEAD_PAD - MLA_NOPE - MLA_ROPE, False)] * 2)
    b_d = _block_diag_mean([(DIFF_DH, True)] * 4)

    w_gu = jnp.concatenate([
        jnp.concatenate([sh_w_gate, sh_w_up], axis=-1)[:, None].astype(BF16),
        jnp.concatenate([moe_w_gate.astype(BF16), moe_w_up.astype(BF16)], axis=-1)], axis=1)
    w_dn = jnp.concatenate([sh_w_down[:, None].astype(BF16), moe_w_down.astype(BF16)], axis=1)

    layers = []
    for l in range(depth):
        layers.append({
            "g_attn": attn_norm_g[l].reshape(1, d), "g_ffn": ffn_norm_g[l].reshape(1, d),
            "w_in_a": w_in_a[l], "w_gate": w_gate[l], "w_ukv": w_ukv[l], "w_uq": w_uq[l],
            "g_kv": mla_kv_norm_g[l].reshape(1, MLA_KV_RANK), "g_cq": mla_q_norm_g[l].reshape(1, MLA_Q_RANK),
            "g_k": g_k[l], "g_kr": g_kr[l], "g_q": g_q[l], "g_kd": g_kd[l], "g_qd": g_qd[l],
            "g_sub": g_sub[l].reshape(1, DIFF_V), "b_k": b_k, "b_q": b_q, "b_d": b_d, "v_one": v_one,
            "w_o_mla": w_o_mla_p[l], "w_o_diff": w_o_diff[l].astype(BF16), "w_out": w_out[l].astype(BF16),
            "w_gu": w_gu[l], "w_dn": w_dn[l], "lam_init": lam_inits[l],
            "q_aug": q_aug[l].reshape(1, MLA_W), "qd_aug": qd_aug[l].reshape(1, LANES),
            "flag_m": flag_m[l].astype(jnp.int32).reshape(1), "flag_d": flag_d[l].astype(jnp.int32).reshape(1),
        })
    return layers


def kernel(x, c, ctx, c_ctx, ada_w, ada_b, attn_norm_g, ffn_norm_g, w_in, mla_q_norm_g, mla_w_uq, mla_kv_norm_g, mla_w_ukv, mla_q_g, mla_k_g, diff_q_g, diff_k_g, diff_lam, diff_sub_g, w_o_mla, w_o_diff, w_out, router_w, router_b, moe_w_gate, moe_w_up, moe_w_down, sh_w_gate, sh_w_up, sh_w_down):
    n_batch, seq, d = x.shape
    n_ctx = ctx.shape[1]
    depth = w_in.shape[0]
    assert d == D_MODEL and n_ctx == ROW_TILE and seq % GRID_W == 0
    assert seq % MLA_TQ == 0 and seq % DIFF_TQ == 0 and ((seq + n_ctx) % KV_CHUNK) % 16 == 0
    assert (n_batch * seq) % MOE_TILE == 0 and (n_batch * n_ctx) % MOE_TILE == 0
    lat_rows = n_batch * seq
    all_rows = lat_rows + n_batch * n_ctx

    layers = _prep_weights(attn_norm_g, ffn_norm_g, w_in, mla_q_norm_g, mla_w_uq, mla_kv_norm_g, mla_w_ukv,
                           mla_q_g, mla_k_g, diff_q_g, diff_k_g, diff_sub_g, w_o_mla, w_o_diff, w_out,
                           moe_w_gate, moe_w_up, moe_w_down, sh_w_gate, sh_w_up, sh_w_down)
    rope_tab = _rope_table(seq, ROW_TILE)
    rw = jnp.pad(router_w, ((0, 0), (1, LANES - 1 - N_EXPERTS)))
    rb = jnp.pad(router_b, (1, LANES - 1 - N_EXPERTS)).reshape(1, LANES)

    pad_rows = (-(n_batch + 1)) % 8
    c_all = jnp.concatenate([c, c_ctx[None, :], jnp.zeros((pad_rows, d), F32)], axis=0)
    mods_all = _mods_call(c_all, ada_w, ada_b).reshape(depth, c_all.shape[0], N_MOD, d)

    xf = jnp.concatenate([x.reshape(lat_rows, d), ctx.reshape(n_batch * n_ctx, d)], axis=0)
    pending = None
    for l in range(depth):
        wl = layers[l]
        mods = mods_all[l]
        if pending is None:
            k_m, v_m, q_m, k_d, v_d, q_d = _proj_call(xf, mods, wl, rope_tab, n_batch, seq)
        else:
            xf, k_m, v_m, q_m, k_d, v_d, q_d = _proj_call(xf, mods, wl, rope_tab, n_batch, seq, prev=pending)
        last = l == depth - 1
        o_m = _attn_call(wl["flag_m"], q_m, k_m, v_m, n_batch, seq, n_ctx, MLA_HEADS, 1, [], 0.0, MLA_TQ,
                         not last, "attn_mla")
        o_d = _attn_call(wl["flag_d"], q_d, k_d, v_d, n_batch, seq, n_ctx, DIFF_HEADS, 2,
                         [diff_lam[l], wl["g_sub"]], wl["lam_init"], DIFF_TQ, not last, "attn_diff")
        out_rows = lat_rows if last else all_rows
        xf, f_ext, pos, counts = _merge_call(xf, mods, o_m, o_d, wl, rw, rb, n_batch, seq, out_rows)
        y = _moe_sparse_call(f_ext, _moe_plan(pos, counts, out_rows), wl["w_gu"], wl["w_dn"], out_rows)
        pending = (y, mods)
    xf = _residual_call(xf, pending[0], pending[1], n_batch, seq, lat_rows)
    return xf.reshape(n_batch, seq, d)
```

```python
import functools
import math

import jax
import jax.numpy as jnp
from jax import lax
from jax.experimental import pallas as pl
from jax.experimental.pallas import tpu as pltpu

F32 = jnp.float32
BF16 = jnp.bfloat16

D_MODEL = 1024
GRID_W = 64
EPS = 1e-6
ROPE_BASE = 10000.0
N_MOD = 6

MLA_HEADS = 8
MLA_NOPE = 64
MLA_ROPE = 32
MLA_V = 64
MLA_Q_RANK = 512
MLA_KV_RANK = 256
MLA_SCALE = 1.0 / math.sqrt(MLA_NOPE + MLA_ROPE)

DIFF_HEADS = 4
DIFF_DH = 64
DIFF_V = 2 * DIFF_DH
DIFF_SCALE = 1.0 / math.sqrt(DIFF_DH)

N_EXPERTS = 16
EXPERTS_PER_GROUP = 4
N_GROUPS = N_EXPERTS // EXPERTS_PER_GROUP
D_EXPERT = 512
PAIR_LO = (0, 0, 0, 1, 1, 2)
PAIR_HI = (1, 2, 3, 2, 3, 3)
N_PAIRS = len(PAIR_LO)
N_CLASSES = N_GROUPS * N_PAIRS

LANES = 128
MXU_DIM = 256
HEAD_PAD = LANES
MLA_W = MLA_HEADS * HEAD_PAD
DIFF_QK_W = DIFF_HEADS * 2 * DIFF_DH
DIFF_K_W = DIFF_HEADS * 2 * LANES
DIFF_V_PAD = 2 * LANES
DIFF_V_W = DIFF_HEADS * DIFF_V_PAD

A_CKV = 0
A_KR = A_CKV + MLA_KV_RANK
A_KD = A_KR + LANES
A_VD = A_KD + DIFF_QK_W
A_CQ = A_VD + DIFF_HEADS * DIFF_V
A_QD = A_CQ + MLA_Q_RANK
A_COLS = A_QD + DIFF_QK_W

ROPE_TAB_W = 6 * LANES
F_EXT_W = D_MODEL + LANES

ROW_TILE = 256
MOE_TILE = 512
SORT_TILE = 256
KV_CHUNK = 768
MLA_TQ = 2048
DIFF_TQ = 512
LOOP_STAGES = 4
SHIFTED_LOOP_STAGES = 8
LOG2E = math.log2(math.e)
VMEM_LIMIT = 56 * 1024 * 1024


def _sigmoid(x):
    return 1.0 / (1.0 + jnp.exp(-x))


def _dot(a, b):
    return jnp.dot(a, b, preferred_element_type=F32)


def _split_bf16(a):
    hi = a.astype(BF16)
    lo = (a - hi.astype(F32)).astype(BF16)
    return hi, lo


def _dot3(a, b):
    a_hi, a_lo = _split_bf16(a)
    b_hi, b_lo = _split_bf16(b)
    return _dot(a_hi, b_hi) + _dot(a_lo, b_hi) + _dot(a_hi, b_lo)


def _mods_kernel(c_ref, w_ref, b_ref, o_ref):
    c = c_ref[...]
    a = c * _sigmoid(c)
    o_ref[...] = _dot3(a, w_ref[...]) + b_ref[...]


def _mods_call(c_all, ada_w, ada_b):
    depth, d, n = ada_w.shape
    tn = 1536
    rows = c_all.shape[0]
    return pl.pallas_call(
        _mods_kernel,
        grid=(depth, n // tn),
        in_specs=[
            pl.BlockSpec((rows, d), lambda l, j: (0, 0)),
            pl.BlockSpec((None, d, tn), lambda l, j: (l, 0, j)),
            pl.BlockSpec((None, 1, tn), lambda l, j: (l, 0, j)),
        ],
        out_specs=pl.BlockSpec((None, rows, tn), lambda l, j: (l, 0, j)),
        out_shape=jax.ShapeDtypeStruct((depth, rows, n), F32),
        compiler_params=pltpu.CompilerParams(
            dimension_semantics=("parallel", "parallel"), vmem_limit_bytes=VMEM_LIMIT),
        name="adaln_mods",
    )(c_all, ada_w, ada_b.reshape(depth, 1, n))


def _rms(x, g):
    return x * lax.rsqrt(jnp.mean(x * x, axis=-1, keepdims=True) + EPS) * g


def _seg_mean_sq(x, b_ref):
    sq = (x * x).astype(BF16)
    b = b_ref[...]
    parts = [_dot(sq[:, c:c + MXU_DIM], b) for c in range(0, x.shape[1], MXU_DIM)]
    return parts[0] if len(parts) == 1 else jnp.concatenate(parts, axis=1)


def _rope(x, cos, sin_a, sin_b, half):
    return (x * cos + pltpu.roll(x, LANES - half, axis=1) * sin_a
            + pltpu.roll(x, half, axis=1) * sin_b)


def _proj_kernel(*refs, fused_residual):
    if fused_residual:
        x_ref, y_ref, modp_ref = refs[:3]
        refs = refs[3:]
        x = x_ref[...] + modp_ref[5:6, :] * y_ref[...]
    else:
        x = refs[0][...]
        refs = refs[1:]
    (mod_ref, g_ref, win_ref, wukv_ref, wuq_ref, gkv_ref, gcq_ref, gk_ref, gkr_ref, gq_ref, gkd_ref, gqd_ref,
     bk_ref, bq_ref, bd_ref, vone_ref, qaug_ref, qdaug_ref, rope_ref) = refs[:19]
    outs = refs[19:]
    if fused_residual:
        outs[0][...] = x
        outs = outs[1:]
    kmla_ref, vmla_ref, qmla_ref, kd_ref, vd_ref, qd_ref = outs
    h = _rms(x, g_ref[...]) * (1.0 + mod_ref[1:2, :]) + mod_ref[0:1, :]
    p = _dot(h.astype(BF16), win_ref[...])

    cos_m = rope_ref[:, 0 * LANES:1 * LANES]
    sin_ma = rope_ref[:, 1 * LANES:2 * LANES]
    sin_mb = rope_ref[:, 2 * LANES:3 * LANES]
    cos_d = rope_ref[:, 3 * LANES:4 * LANES]
    sin_da = rope_ref[:, 4 * LANES:5 * LANES]
    sin_db = rope_ref[:, 5 * LANES:6 * LANES]

    ckv = _rms(p[:, A_CKV:A_KR], gkv_ref[...])
    kv = _dot(ckv.astype(BF16), wukv_ref[...])
    kn = kv[:, :MLA_W]
    kn = kn * lax.rsqrt(_seg_mean_sq(kn, bk_ref) + EPS) * gk_ref[...]
    kr = p[:, A_KR:A_KD]
    kr = kr * lax.rsqrt(jnp.sum(kr * kr, axis=-1, keepdims=True) * (1.0 / MLA_ROPE) + EPS) * gkr_ref[...]
    kr = _rope(kr, cos_m, sin_ma, sin_mb, MLA_ROPE // 4)
    tm = x.shape[0]
    lane_blk = lax.broadcasted_iota(jnp.int32, (tm, LANES), 1)
    kr = kr + (lane_blk == MLA_NOPE + MLA_ROPE).astype(F32)
    for hd in range(MLA_HEADS):
        sl = slice(hd * HEAD_PAD, (hd + 1) * HEAD_PAD)
        kmla_ref[:, sl] = (kn[:, sl] + kr).astype(BF16)
    vmla_ref[...] = (kv[:, MLA_W:] + vone_ref[...]).astype(BF16)

    cq = _rms(p[:, A_CQ:A_QD], gcq_ref[...])
    q = _dot(cq.astype(BF16), wuq_ref[...])
    q = q * lax.rsqrt(_seg_mean_sq(q, bq_ref) + EPS) * gq_ref[...]
    for hd in range(MLA_HEADS):
        sl = slice(hd * HEAD_PAD, (hd + 1) * HEAD_PAD)
        qmla_ref[:, sl] = (_rope(q[:, sl], cos_m, sin_ma, sin_mb, MLA_ROPE // 4) + qaug_ref[:, sl]).astype(BF16)

    kd = p[:, A_KD:A_VD]
    kd = kd * lax.rsqrt(_seg_mean_sq(kd, bd_ref) + EPS) * gkd_ref[...]
    qd = p[:, A_QD:A_COLS]
    qd = qd * lax.rsqrt(_seg_mean_sq(qd, bd_ref) + EPS) * gqd_ref[...]
    ones_blk = (lane_blk == 0).astype(BF16)
    two_ones_blk = (lane_blk < 2).astype(BF16)
    qd_shift = jnp.broadcast_to(qdaug_ref[...], (tm, LANES)).astype(BF16)
    for hd in range(DIFF_HEADS):
        sl = slice(hd * LANES, (hd + 1) * LANES)
        lo = slice(2 * hd * LANES, (2 * hd + 1) * LANES)
        hi = slice((2 * hd + 1) * LANES, (2 * hd + 2) * LANES)
        kd_ref[:, lo] = _rope(kd[:, sl], cos_d, sin_da, sin_db, DIFF_DH // 4).astype(BF16)
        kd_ref[:, hi] = two_ones_blk
        qd_ref[:, lo] = _rope(qd[:, sl], cos_d, sin_da, sin_db, DIFF_DH // 4).astype(BF16)
        qd_ref[:, hi] = qd_shift

    vd = p[:, A_VD:A_CQ]
    for hd in range(DIFF_HEADS):
        vd_ref[:, hd * DIFF_V_PAD:hd * DIFF_V_PAD + DIFF_V] = vd[:, hd * DIFF_V:(hd + 1) * DIFF_V].astype(BF16)
        vd_ref[:, hd * DIFF_V_PAD + DIFF_V:(hd + 1) * DIFF_V_PAD] = ones_blk


def _row_class_maps(n_lat_tiles, tiles_per_batch, n_batch):
    def mod_map(i):
        return (jnp.where(i < n_lat_tiles, i // tiles_per_batch, n_batch), 0, 0)

    def rope_map(i):
        return (jnp.where(i < n_lat_tiles, i % tiles_per_batch, tiles_per_batch), 0)

    return mod_map, rope_map


def _const_spec(shape):
    return pl.BlockSpec(shape, lambda i: tuple(0 for _ in shape))


def _proj_call(x_flat, mods, wl, rope_tab, n_batch, seq, prev=None):
    rows = x_flat.shape[0]
    tm = ROW_TILE
    n_lat_tiles = n_batch * seq // tm
    mod_map, rope_map = _row_class_maps(n_lat_tiles, seq // tm, n_batch)
    row_spec = lambda w: pl.BlockSpec((tm, w), lambda i: (i, 0))
    tiles_per_batch = seq // tm

    def kv_map(i):
        lat = i + i // tiles_per_batch + 1
        ctx = (i - n_lat_tiles) * (tiles_per_batch + 1)
        return (jnp.where(i < n_lat_tiles, lat, ctx), 0)

    kv_spec = lambda w: pl.BlockSpec((tm, w), kv_map)
    consts = [wl["g_attn"], wl["w_in_a"], wl["w_ukv"], wl["w_uq"], wl["g_kv"], wl["g_cq"], wl["g_k"], wl["g_kr"],
              wl["g_q"], wl["g_kd"], wl["g_qd"], wl["b_k"], wl["b_q"], wl["b_d"], wl["v_one"], wl["q_aug"],
              wl["qd_aug"]]
    out_w = [MLA_W, MLA_W, MLA_W, DIFF_K_W, DIFF_V_W, DIFF_K_W]
    out_specs = [kv_spec(MLA_W), kv_spec(MLA_W), row_spec(MLA_W), kv_spec(DIFF_K_W), kv_spec(DIFF_V_W),
                 row_spec(DIFF_K_W)]
    mod_spec = pl.BlockSpec((None, N_MOD, D_MODEL), mod_map)
    lead_specs, lead_args = [row_spec(D_MODEL)], [x_flat]
    out_shape = [jax.ShapeDtypeStruct((rows, w), BF16) for w in out_w]
    if prev is not None:
        lead_specs += [row_spec(D_MODEL), mod_spec]
        lead_args += list(prev)
        out_specs = [row_spec(D_MODEL)] + out_specs
        out_shape = [jax.ShapeDtypeStruct((rows, D_MODEL), F32)] + out_shape
    return pl.pallas_call(
        functools.partial(_proj_kernel, fused_residual=prev is not None),
        grid=(rows // tm,),
        in_specs=lead_specs + [mod_spec] + [_const_spec(c.shape) for c in consts]
        + [pl.BlockSpec((tm, ROPE_TAB_W), rope_map)],
        out_specs=out_specs,
        out_shape=out_shape,
        compiler_params=pltpu.CompilerParams(
            dimension_semantics=("parallel",), vmem_limit_bytes=VMEM_LIMIT),
        name="proj_qkv",
    )(*lead_args, mods, *consts, rope_tab)


SAFE_SHIFT = 50.0
SHIFT_MARGIN = 1.02


def _attn_kernel(flag_ref, *refs, n_maps, tq, tk, chunks, lam_init, aliased_out):
    refs = list(refs)
    q_ref, k_ref, v_ref = refs[:3]
    n_in = 3 + (2 if n_maps == 2 else 0) + (1 if aliased_out else 0)
    o_ref = refs[n_in]
    if n_maps == 2:
        lam_ref, subg_ref = refs[3:5]
        qs_ref, s_ref, p_ref, m_ref, a_ref, acc_ref = refs[n_in + 1:]
        qf = q_ref[...].astype(F32)
        lane = lax.broadcasted_iota(jnp.int32, qf.shape, 1)
        keep0 = (lane < DIFF_DH) | (lane == 2 * DIFF_DH)
        keep1 = ((lane >= DIFF_DH) & (lane < 2 * DIFF_DH)) | (lane == 2 * DIFF_DH + 1)
        qs_ref[0:tq, :] = jnp.where(keep0, qf, 0.0).astype(BF16)
        qs_ref[tq:2 * tq, :] = jnp.where(keep1, qf, 0.0).astype(BF16)
        q_src = qs_ref
    else:
        s_ref, p_ref, m_ref, a_ref, acc_ref = refs[n_in + 1:]
        q_src = q_ref
    vw = v_ref.shape[1]
    sum_col = vw // 2

    head, n_mid, tail = chunks
    n_chunks = 1 + n_mid + (1 if tail else 0)

    def geom(j):
        if isinstance(j, int):
            if j == 0:
                return 0, head
            if j <= n_mid:
                return head + (j - 1) * tk, tk
            return head + n_mid * tk, tail
        return pl.multiple_of(head + (j - 1) * tk, math.gcd(head, tk)), tk

    def rows_of(ref, j):
        off, size = geom(j)
        return ref[pl.ds(off, size), :]

    def scores(j, slot):
        size = geom(j)[1]
        s_ref[slot, :, :size] = lax.dot_general(q_src[...], rows_of(k_ref, j), (((1,), (1,)), ((), ())),
                                                preferred_element_type=F32)

    def probs_shifted(j, slot):
        size = geom(j)[1]
        p_ref[slot, :, :size] = jnp.exp2(s_ref[slot, :, :size]).astype(BF16)

    def accumulate_plain(j, slot):
        size = geom(j)[1]
        r = _dot(p_ref[slot, :, :size], rows_of(v_ref, j))
        if isinstance(j, int) and j == 0:
            acc_ref[...] = r
        else:
            acc_ref[...] += r

    def probs_online(j, slot):
        size = geom(j)[1]
        s = s_ref[slot, :, :size]
        mx = jnp.max(s, axis=-1, keepdims=True)
        if isinstance(j, int) and j == 0:
            m_new = mx
        else:
            m_old = m_ref[...]
            m_new = jnp.maximum(m_old, mx)
            a_ref[slot] = jnp.exp2(m_old - m_new)
        m_ref[...] = m_new
        p_ref[slot, :, :size] = jnp.exp2(s - m_new).astype(BF16)

    def accumulate_online(j, slot):
        size = geom(j)[1]
        r = _dot(p_ref[slot, :, :size], rows_of(v_ref, j))
        if isinstance(j, int) and j == 0:
            acc_ref[...] = r
        else:
            acc_ref[...] = acc_ref[...] * a_ref[slot] + r

    def sweep(probs, accumulate, loop_stages):
        def stage(j, slot, with_scores):
            probs(j, slot)
            accumulate(j - 1, 1 - slot)
            if with_scores:
                scores(j + 1, 1 - slot)

        scores(0, 0)
        if n_chunks > 1:
            scores(1, 1)
        probs(0, 0)
        if n_chunks > 1:
            stage(1, 1, n_chunks > 2)
            n_loop = max(n_mid - 2, 0) // loop_stages

            def group(i, carry):
                j0 = 2 + loop_stages * i
                for t in range(loop_stages):
                    stage(j0 + t, t % 2, True)
                return carry

            lax.fori_loop(0, n_loop, group, 0)
            for j in range(2 + loop_stages * n_loop, n_chunks):
                stage(j, j % 2, j + 1 < n_chunks)
        accumulate(n_chunks - 1, (n_chunks - 1) % 2)

    @pl.when(flag_ref[0] == 1)
    def _():
        sweep(probs_shifted, accumulate_plain, SHIFTED_LOOP_STAGES)

    @pl.when(flag_ref[0] != 1)
    def _():
        sweep(probs_online, accumulate_online, LOOP_STAGES)

    acc = acc_ref[...]
    inv_l = 1.0 / acc[:, sum_col:sum_col + 1]
    if n_maps == 1:
        o_ref[...] = (acc * inv_l).astype(o_ref.dtype)
    else:
        lf = lam_ref[...]
        lam = (jnp.exp(jnp.sum(lf[0:1, :] * lf[1:2, :], axis=-1, keepdims=True))
               - jnp.exp(jnp.sum(lf[2:3, :] * lf[3:4, :], axis=-1, keepdims=True)) + lam_init)
        o = acc[:, :DIFF_V] * inv_l
        o = o[:tq] - lam * o[tq:]
        o_ref[...] = _rms(o, subg_ref[...]).astype(o_ref.dtype)


def _attn_scratch(n_maps, tq, tk, kw, vw):
    r = n_maps * tq
    qs = [pltpu.VMEM((r, kw), BF16)] if n_maps == 2 else []
    return qs + [pltpu.VMEM((2, r, tk), F32), pltpu.VMEM((2, r, tk), BF16), pltpu.VMEM((r, 1), F32),
                 pltpu.VMEM((2, r, 1), F32), pltpu.VMEM((r, vw), F32)]


def _attn_call(flag, q, k, v, n_batch, seq, n_ctx, n_heads, n_maps, extra, lam_init, tq, with_ctx_queries, name):
    rows = q.shape[0]
    vw = v.shape[1] // n_heads
    kw = k.shape[1] // n_heads
    qw = q.shape[1] // n_heads
    keys = seq + n_ctx
    tk = KV_CHUNK
    nq = seq // tq
    assert seq % tq == 0
    chunks = (tk, keys // tk - 1, keys % tk)
    ctx_blk0 = n_batch * seq // n_ctx
    blocks_per_batch = keys // n_ctx
    extra_specs = [pl.BlockSpec(e.shape, lambda *_: (0, 0)) for e in extra]
    out_shape = jax.ShapeDtypeStruct((rows, n_heads * LANES), BF16)

    kern = functools.partial(_attn_kernel, n_maps=n_maps, tq=tq, tk=tk, chunks=chunks, lam_init=lam_init,
                             aliased_out=False)
    out = pl.pallas_call(
        kern,
        grid_spec=pltpu.PrefetchScalarGridSpec(
            num_scalar_prefetch=1,
            grid=(n_batch, n_heads, nq),
            in_specs=[pl.BlockSpec((tq, qw), lambda b, h, qi, f: (b * nq + qi, h)),
                      pl.BlockSpec((keys, kw), lambda b, h, qi, f: (b, h)),
                      pl.BlockSpec((keys, vw), lambda b, h, qi, f: (b, h))] + extra_specs,
            out_specs=pl.BlockSpec((tq, LANES), lambda b, h, qi, f: (b * nq + qi, h)),
            scratch_shapes=_attn_scratch(n_maps, tq, tk, kw, vw)),
        out_shape=out_shape,
        compiler_params=pltpu.CompilerParams(
            dimension_semantics=("parallel", "parallel", "arbitrary"), vmem_limit_bytes=VMEM_LIMIT),
        name=name,
    )(flag, q, k, v, *extra)
    if not with_ctx_queries:
        return out

    kern_c = functools.partial(_attn_kernel, n_maps=n_maps, tq=n_ctx, tk=n_ctx, chunks=(n_ctx, 0, 0),
                               lam_init=lam_init, aliased_out=True)
    return pl.pallas_call(
        kern_c,
        grid_spec=pltpu.PrefetchScalarGridSpec(
            num_scalar_prefetch=1,
            grid=(n_batch, n_heads),
            in_specs=[pl.BlockSpec((n_ctx, qw), lambda b, h, f: (ctx_blk0 + b, h)),
                      pl.BlockSpec((n_ctx, kw), lambda b, h, f: (b * blocks_per_batch, h)),
                      pl.BlockSpec((n_ctx, vw), lambda b, h, f: (b * blocks_per_batch, h))] + extra_specs
            + [pl.BlockSpec(memory_space=pl.ANY)],
            out_specs=pl.BlockSpec((n_ctx, LANES), lambda b, h, f: (ctx_blk0 + b, h)),
            scratch_shapes=_attn_scratch(n_maps, n_ctx, n_ctx, kw, vw)),
        out_shape=out_shape,
        input_output_aliases={4 + len(extra): 0},
        compiler_params=pltpu.CompilerParams(
            dimension_semantics=("parallel", "parallel"), vmem_limit_bytes=VMEM_LIMIT),
        name=name + "_ctx",
    )(flag, q, k, v, *extra, out)


def _route(logits, bias):
    lane = lax.broadcasted_iota(jnp.int32, logits.shape, 1)
    is_exp = (lane >= 1) & (lane <= N_EXPERTS)
    grp = (lane - 1) >> 2
    neg = jnp.float32(-jnp.inf)
    big = jnp.int32(1 << 20)
    aff = _sigmoid(logits)
    sel = jnp.where(is_exp, aff + bias, neg)

    def first_argmax(v):
        mx = jnp.max(v, axis=-1, keepdims=True)
        ix = jnp.min(jnp.where(v == mx, lane, big), axis=-1, keepdims=True)
        return mx, ix

    best = None
    for g in range(N_GROUPS):
        vg = jnp.where(grp == g, sel, neg)
        m1, i1 = first_argmax(vg)
        m2, i2 = first_argmax(jnp.where(lane == i1, neg, vg))
        score = m1 + m2
        if best is None:
            best = (score, i1, i2)
        else:
            upd = score > best[0]
            best = (jnp.where(upd, score, best[0]), jnp.where(upd, i1, best[1]), jnp.where(upd, i2, best[2]))
    _, i1, i2 = best
    a1 = jnp.sum(jnp.where(lane == i1, aff, 0.0), axis=-1, keepdims=True)
    a2 = jnp.sum(jnp.where(lane == i2, aff, 0.0), axis=-1, keepdims=True)
    tot = a1 + a2
    e1 = i1 - 1
    e2 = i2 - 1
    swap = e2 < e1
    lo = jnp.where(swap, e2, e1)
    hi = jnp.where(swap, e1, e2)
    w_lo = jnp.where(swap, a2, a1) / tot
    w_hi = jnp.where(swap, a1, a2) / tot
    l = lo & (EXPERTS_PER_GROUP - 1)
    h = hi & (EXPERTS_PER_GROUP - 1)
    cls = (lo >> 2) * N_PAIRS + ((l * (7 - l)) >> 1) + (h - l - 1)
    return cls, w_lo, w_hi


def _merge_kernel(x_ref, mod_ref, ga_ref, gf_ref, om_ref, od_ref, wgate_ref, wom_ref, wod_ref, wout_ref,
                  rw_ref, rb_ref, tri_ref, xo_ref, f_ref, pos_ref, cnt_ref, *, cap):
    @pl.when(pl.program_id(0) == 0)
    def _():
        cnt_ref[...] = jnp.zeros_like(cnt_ref)

    x = x_ref[...]
    h = _rms(x, ga_ref[...]) * (1.0 + mod_ref[1:2, :]) + mod_ref[0:1, :]
    gates = _sigmoid(_dot(h.astype(BF16), wgate_ref[...]))
    y = (gates[:, :D_MODEL] * _dot(om_ref[...], wom_ref[...])
         + gates[:, D_MODEL:] * _dot(od_ref[...], wod_ref[...]))
    xn = x + mod_ref[2:3, :] * _dot(y.astype(BF16), wout_ref[...])
    xo_ref[...] = xn
    f = _rms(xn, gf_ref[...]) * (1.0 + mod_ref[4:5, :]) + mod_ref[3:4, :]
    cls, w_lo, w_hi = _route(_dot3(f, rw_ref[...]), rb_ref[...])
    lane = lax.broadcasted_iota(jnp.int32, (x.shape[0], LANES), 1)
    f_ref[:, :D_MODEL] = f
    f_ref[:, D_MODEL:] = jnp.where(lane == 0, w_lo, jnp.where(lane == 1, w_hi, 0.0))
    onehot = (lane == cls).astype(F32)
    before = _dot(tri_ref[...], onehot.astype(BF16)) + cnt_ref[...]
    rank = jnp.sum(onehot * before, axis=-1, keepdims=True)
    cnt_ref[...] += jnp.sum(onehot, axis=0, keepdims=True)
    pos = cls.astype(F32) * float(cap) + rank
    pos_ref[...] = jnp.where(lane == 0, pos, 0.0)


def _merge_call(x_flat, mods, o_mla, o_diff, wl, router_w, router_b, n_batch, seq, rows):
    tm = ROW_TILE
    mod_map, _ = _row_class_maps(n_batch * seq // tm, seq // tm, n_batch)
    row_spec = lambda w: pl.BlockSpec((tm, w), lambda i: (i, 0))
    tri = (jnp.arange(tm)[:, None] > jnp.arange(tm)[None, :]).astype(BF16)
    consts = [wl["w_gate"], wl["w_o_mla"], wl["w_o_diff"], wl["w_out"], router_w, router_b, tri]
    return pl.pallas_call(
        functools.partial(_merge_kernel, cap=rows),
        grid=(rows // tm,),
        in_specs=[row_spec(D_MODEL), pl.BlockSpec((None, N_MOD, D_MODEL), mod_map),
                  _const_spec(wl["g_attn"].shape), _const_spec(wl["g_ffn"].shape),
                  row_spec(MLA_W), row_spec(DIFF_HEADS * LANES)]
        + [_const_spec(c.shape) for c in consts],
        out_specs=[row_spec(D_MODEL), row_spec(F_EXT_W), row_spec(LANES), _const_spec((1, LANES))],
        out_shape=[jax.ShapeDtypeStruct((rows, D_MODEL), F32), jax.ShapeDtypeStruct((rows, F_EXT_W), F32),
                   jax.ShapeDtypeStruct((rows, LANES), F32), jax.ShapeDtypeStruct((1, LANES), F32)],
        compiler_params=pltpu.CompilerParams(
            dimension_semantics=("arbitrary",), vmem_limit_bytes=VMEM_LIMIT),
        name="merge_router",
    )(x_flat, mods, wl["g_attn"], wl["g_ffn"], o_mla, o_diff, *consts)


def _moe_plan(pos_rec, counts_rec, n_rows):
    t = SORT_TILE
    tiles_per_bucket = n_rows // t
    counts = counts_rec[0, :N_CLASSES].astype(jnp.int32)
    pos = pos_rec[:, 0].astype(jnp.int32)
    src = jnp.zeros((N_CLASSES * n_rows,), jnp.int32).at[pos].set(jnp.arange(n_rows, dtype=jnp.int32))
    class_tiles = (counts + t - 1) // t
    tile_end = jnp.cumsum(class_tiles)
    n_tiles = tiles_per_bucket + N_CLASSES
    tile = jnp.arange(n_tiles, dtype=jnp.int32)
    tile_cls = jnp.minimum(jnp.searchsorted(tile_end, tile, side="right", method="compare_all"),
                           N_CLASSES - 1).astype(jnp.int32)
    in_class = tile - (tile_end - class_tiles)[tile_cls]
    n_used = tile_end[-1].reshape(1)
    used = tile < n_used[0]
    blk = jnp.where(used, tile_cls * tiles_per_bucket + in_class, 0)
    n_valid = jnp.where(used, jnp.clip(counts[tile_cls] - in_class * t, 0, t), 0)
    base = (tile_cls // N_PAIRS) * EXPERTS_PER_GROUP
    pair = tile_cls % N_PAIRS
    ea = base + jnp.asarray(PAIR_LO, jnp.int32)[pair]
    eb = base + jnp.asarray(PAIR_HI, jnp.int32)[pair]
    return ea, eb, n_used, blk, n_valid, src.reshape(N_CLASSES * tiles_per_bucket, 1, t)


def _moe_sparse_kernel(ea_ref, eb_ref, nu_ref, blk_ref, nv_ref, src0_ref, srcn_ref, srcc_ref, f_hbm,
                       wgu_s, wd_s, wgu_a, wd_a, wgu_b, wd_b, y_hbm, xbuf, ybuf, gsem, ssem, *, n_rows):
    del ea_ref, eb_ref, blk_ref
    t = SORT_TILE
    i = pl.program_id(0)
    n_used = nu_ref[0]
    n_valid = nv_ref[i]
    slot = i % 2

    def start_gather(idx_ref, s):
        for r in range(t):
            pltpu.make_async_copy(f_hbm.at[pl.ds(idx_ref[0, r], 1), :], xbuf.at[s, pl.ds(r, 1), :],
                                  gsem.at[s]).start()

    def gather_done(s):
        return pltpu.make_async_copy(f_hbm.at[pl.ds(0, t), :], xbuf.at[s], gsem.at[s])

    def scatter_done(s):
        return pltpu.make_async_copy(ybuf.at[s], y_hbm.at[pl.ds(0, t), :], ssem.at[s])

    @pl.when(i == 0)
    def _():
        start_gather(src0_ref, 0)

    @pl.when(i < n_used)
    def _():
        gather_done(slot).wait()
        start_gather(srcn_ref, 1 - slot)
        x = xbuf[slot, :, :D_MODEL].astype(BF16)
        w = xbuf[slot, :, D_MODEL:]

        def ffn(wgu_ref, wd_ref, scale):
            gu = _dot(x, wgu_ref[...])
            g = gu[:, :D_EXPERT]
            he = g * _sigmoid(g) * gu[:, D_EXPERT:]
            if scale is not None:
                he = he * scale
            return _dot(he.astype(BF16), wd_ref[...])

        ybuf[slot] = ffn(wgu_s, wd_s, None) + ffn(wgu_a, wd_a, w[:, 0:1]) + ffn(wgu_b, wd_b, w[:, 1:2])
        for r in range(t):
            dst = jnp.where(r < n_valid, srcc_ref[0, r], n_rows + slot * t + r)
            pltpu.make_async_copy(ybuf.at[slot, pl.ds(r, 1), :], y_hbm.at[pl.ds(dst, 1), :],
                                  ssem.at[slot]).start()

        @pl.when(i >= 1)
        def _():
            scatter_done(1 - slot).wait()

        @pl.when(i == n_used - 1)
        def _():
            scatter_done(slot).wait()
            gather_done(1 - slot).wait()


def _moe_sparse_call(f_ext, plan, w_gu, w_down, n_rows):
    ea, eb, n_used, blk, n_valid, src = plan
    t = SORT_TILE
    n_tiles = blk.shape[0]
    smem_spec = lambda imap: pl.BlockSpec((None, 1, t), imap, memory_space=pltpu.SMEM)
    wspec = lambda shape, imap: pl.BlockSpec((None,) + shape, imap)
    gu_shape = (D_MODEL, 2 * D_EXPERT)
    dn_shape = (D_EXPERT, D_MODEL)
    grid_spec = pltpu.PrefetchScalarGridSpec(
        num_scalar_prefetch=5,
        grid=(n_tiles,),
        in_specs=[
            smem_spec(lambda i, ea, eb, nu, blk, nv: (blk[0], 0, 0)),
            smem_spec(lambda i, ea, eb, nu, blk, nv: (blk[jnp.minimum(i + 1, n_tiles - 1)], 0, 0)),
            smem_spec(lambda i, ea, eb, nu, blk, nv: (blk[i], 0, 0)),
            pl.BlockSpec(memory_space=pl.ANY),
            wspec(gu_shape, lambda i, ea, eb, nu, blk, nv: (0, 0, 0)),
            wspec(dn_shape, lambda i, ea, eb, nu, blk, nv: (0, 0, 0)),
            wspec(gu_shape, lambda i, ea, eb, nu, blk, nv: (ea[i] + 1, 0, 0)),
            wspec(dn_shape, lambda i, ea, eb, nu, blk, nv: (ea[i] + 1, 0, 0)),
            wspec(gu_shape, lambda i, ea, eb, nu, blk, nv: (eb[i] + 1, 0, 0)),
            wspec(dn_shape, lambda i, ea, eb, nu, blk, nv: (eb[i] + 1, 0, 0)),
        ],
        out_specs=pl.BlockSpec(memory_space=pl.ANY),
        scratch_shapes=[pltpu.VMEM((2, t, F_EXT_W), F32), pltpu.VMEM((2, t, D_MODEL), F32),
                        pltpu.SemaphoreType.DMA((2,)), pltpu.SemaphoreType.DMA((2,))],
    )
    return pl.pallas_call(
        functools.partial(_moe_sparse_kernel, n_rows=n_rows),
        grid_spec=grid_spec,
        out_shape=jax.ShapeDtypeStruct((n_rows + 2 * t, D_MODEL), F32),
        compiler_params=pltpu.CompilerParams(
            dimension_semantics=("arbitrary",), vmem_limit_bytes=VMEM_LIMIT),
        name="moe_sorted",
    )(ea, eb, n_used, blk, n_valid, src, src, src, f_ext, w_gu, w_down, w_gu, w_down, w_gu, w_down)


def _residual_kernel(x_ref, y_ref, mod_ref, o_ref):
    o_ref[...] = x_ref[...] + mod_ref[5:6, :] * y_ref[...]


def _residual_call(x_flat, y, mods, n_batch, seq, out_rows):
    tm = MOE_TILE
    mod_map, _ = _row_class_maps(n_batch * seq // tm, seq // tm, n_batch)
    row_spec = pl.BlockSpec((tm, D_MODEL), lambda i: (i, 0))
    return pl.pallas_call(
        _residual_kernel,
        grid=(out_rows // tm,),
        in_specs=[row_spec, row_spec, pl.BlockSpec((None, N_MOD, D_MODEL), mod_map)],
        out_specs=row_spec,
        out_shape=jax.ShapeDtypeStruct((out_rows, D_MODEL), F32),
        compiler_params=pltpu.CompilerParams(
            dimension_semantics=("parallel",), vmem_limit_bytes=VMEM_LIMIT),
        name="ffn_residual",
    )(x_flat, y, mods)


def _axial_angles(seq, d_rope):
    d_axis = d_rope // 2
    inv = ROPE_BASE ** (-jnp.arange(0, d_axis, 2, dtype=F32) / d_axis)
    t = jnp.arange(seq, dtype=jnp.int32)
    r = (t // GRID_W).astype(F32)
    col = (t % GRID_W).astype(F32)
    ang_r = r[:, None] * inv
    ang_c = col[:, None] * inv
    cos = jnp.concatenate([jnp.cos(ang_r)] * 2 + [jnp.cos(ang_c)] * 2, axis=1)
    sin = jnp.concatenate([jnp.sin(ang_r)] * 2 + [jnp.sin(ang_c)] * 2, axis=1)
    nf = d_rope // 4
    first = jnp.concatenate([jnp.ones((nf,), F32), jnp.zeros((nf,), F32)] * 2)
    return cos, -sin * first, sin * (1.0 - first)


def _rope_table(seq, tile):
    cm, sma, smb = _axial_angles(seq, MLA_ROPE)
    pad_l = jnp.zeros((seq, MLA_NOPE), F32)
    pad_r = jnp.zeros((seq, HEAD_PAD - MLA_NOPE - MLA_ROPE), F32)
    cos_m = jnp.concatenate([pad_l + 1.0, cm, pad_r + 1.0], axis=1)
    sin_ma = jnp.concatenate([pad_l, sma, pad_r], axis=1)
    sin_mb = jnp.concatenate([pad_l, smb, pad_r], axis=1)
    cd, sda, sdb = _axial_angles(seq, DIFF_DH)
    tab = jnp.concatenate([cos_m, sin_ma, sin_mb] + [jnp.concatenate([a, a], axis=1) for a in (cd, sda, sdb)],
                          axis=1)
    ident = jnp.concatenate([jnp.ones((tile, LANES), F32), jnp.zeros((tile, 2 * LANES), F32)] * 2, axis=1)
    return jnp.concatenate([tab, ident], axis=0)


def _block_diag_mean(seg_sizes):
    m = jnp.zeros((MXU_DIM, MXU_DIM), F32)
    off = 0
    for size, active in seg_sizes:
        if active:
            m = m.at[off:off + size, off:off + size].set(1.0 / size)
        off += size
    assert off == MXU_DIM
    return m.astype(BF16)


def _prep_weights(attn_norm_g, ffn_norm_g, w_in, mla_q_norm_g, mla_w_uq, mla_kv_norm_g, mla_w_ukv, mla_q_g,
                  mla_k_g, diff_q_g, diff_k_g, diff_sub_g, w_o_mla, w_o_diff, w_out, moe_w_gate, moe_w_up,
                  moe_w_down, sh_w_gate, sh_w_up, sh_w_down):
    depth = w_in.shape[0]
    d = D_MODEL
    z = lambda w: jnp.zeros((depth, d, w), F32)
    off_kr = MLA_KV_RANK
    off_kd = off_kr + MLA_ROPE
    off_vd = off_kd + DIFF_QK_W
    off_cq = off_vd + DIFF_HEADS * DIFF_V
    off_qd = off_cq + MLA_Q_RANK
    off_gate = off_qd + DIFF_QK_W
    w_in_a = jnp.concatenate([
        w_in[:, :, :off_kr], z(MLA_NOPE), w_in[:, :, off_kr:off_kd], z(HEAD_PAD - MLA_NOPE - MLA_ROPE),
        w_in[:, :, off_kd:off_gate]], axis=2).astype(BF16)
    w_gate = w_in[:, :, off_gate:].astype(BF16)

    ukv = mla_w_ukv.reshape(depth, MLA_KV_RANK, MLA_HEADS, MLA_NOPE + MLA_V)
    pad_k = jnp.zeros((depth, MLA_KV_RANK, MLA_HEADS, HEAD_PAD - MLA_NOPE), F32)
    pad_v = jnp.zeros((depth, MLA_KV_RANK, MLA_HEADS, HEAD_PAD - MLA_V), F32)
    w_ukv = jnp.concatenate([
        jnp.concatenate([ukv[..., :MLA_NOPE], pad_k], axis=-1).reshape(depth, MLA_KV_RANK, MLA_W),
        jnp.concatenate([ukv[..., MLA_NOPE:], pad_v], axis=-1).reshape(depth, MLA_KV_RANK, MLA_W)],
        axis=-1).astype(BF16)
    uq = mla_w_uq.reshape(depth, MLA_Q_RANK, MLA_HEADS, MLA_NOPE + MLA_ROPE)
    w_uq = jnp.pad(uq, ((0, 0), (0, 0), (0, 0), (0, HEAD_PAD - MLA_NOPE - MLA_ROPE))).reshape(
        depth, MLA_Q_RANK, MLA_W).astype(BF16)
    wom = w_o_mla.reshape(depth, MLA_HEADS, MLA_V, d)
    w_o_mla_p = jnp.pad(wom, ((0, 0), (0, 0), (0, HEAD_PAD - MLA_V), (0, 0))).reshape(depth, MLA_W, d).astype(BF16)

    def per_head(g96, scale):
        g = jnp.pad(g96 * scale, ((0, 0), (0, HEAD_PAD - MLA_NOPE - MLA_ROPE)))
        return jnp.tile(g, (1, MLA_HEADS)).reshape(depth, 1, MLA_W)

    k_nope_only = jnp.concatenate([mla_k_g[:, :MLA_NOPE], jnp.zeros((depth, MLA_ROPE), F32)], axis=1)
    g_k = per_head(k_nope_only, 1.0)
    g_kr = jnp.concatenate([jnp.zeros((depth, MLA_NOPE), F32), mla_k_g[:, MLA_NOPE:],
                            jnp.zeros((depth, HEAD_PAD - MLA_NOPE - MLA_ROPE), F32)], axis=1).reshape(depth, 1, LANES)
    g_q = per_head(mla_q_g, MLA_SCALE * LOG2E)
    g_kd = jnp.tile(diff_k_g.reshape(depth, 2 * DIFF_DH), (1, DIFF_HEADS)).reshape(depth, 1, DIFF_QK_W)
    g_qd = jnp.tile(diff_q_g.reshape(depth, 2 * DIFF_DH) * (DIFF_SCALE * LOG2E),
                    (1, DIFF_HEADS)).reshape(depth, 1, DIFF_QK_W)
    lam_inits = [0.8 - 0.6 * math.exp(-0.3 * l) for l in range(depth)]
    g_sub = diff_sub_g * (1.0 - jnp.asarray(lam_inits, F32))[:, None]

    def seg_norm2(g, n):
        return n * jnp.max(g * g, axis=-1)

    q_n2 = seg_norm2(mla_q_g[:, :MLA_NOPE], MLA_NOPE) + seg_norm2(mla_q_g[:, MLA_NOPE:], MLA_ROPE)
    k_n2 = seg_norm2(mla_k_g[:, :MLA_NOPE], MLA_NOPE) + seg_norm2(mla_k_g[:, MLA_NOPE:], MLA_ROPE)
    shift_m = (jnp.sqrt(q_n2 * k_n2) * (MLA_SCALE * LOG2E * SHIFT_MARGIN)).astype(BF16).astype(F32)
    shift_d = (jnp.sqrt(seg_norm2(diff_q_g, DIFF_DH) * seg_norm2(diff_k_g, DIFF_DH))
               * (DIFF_SCALE * LOG2E * SHIFT_MARGIN)).astype(BF16).astype(F32)
    flag_m = shift_m <= SAFE_SHIFT
    flag_d = jnp.max(shift_d, axis=-1) <= SAFE_SHIFT
    aug_lane = (jnp.arange(HEAD_PAD) == MLA_NOPE + MLA_ROPE).astype(F32)
    q_aug = jnp.tile(aug_lane[None, :] * jnp.where(flag_m, -shift_m, 0.0)[:, None], (1, MLA_HEADS))
    qd_aug = jnp.pad(jnp.where(flag_d[:, None], -shift_d, 0.0), ((0, 0), (0, LANES - 2)))

    v_one = jnp.tile((jnp.arange(HEAD_PAD) == MLA_V).astype(F32), MLA_HEADS).reshape(1, MLA_W)
    b_k = _block_diag_mean([(MLA_NOPE, True), (HEAD_PAD - MLA_NOPE, False)] * 2)
    b_q = _block_diag_mean([(MLA_NOPE, True), (MLA_ROPE, True), (HEAD_PAD - MLA_NOPE - MLA_ROPE, False)] * 2)
    b_d = _block_diag_mean([(DIFF_DH, True)] * 4)

    w_gu = jnp.concatenate([
        jnp.concatenate([sh_w_gate, sh_w_up], axis=-1)[:, None].astype(BF16),
        jnp.concatenate([moe_w_gate.astype(BF16), moe_w_up.astype(BF16)], axis=-1)], axis=1)
    w_dn = jnp.concatenate([sh_w_down[:, None].astype(BF16), moe_w_down.astype(BF16)], axis=1)

    layers = []
    for l in range(depth):
        layers.append({
            "g_attn": attn_norm_g[l].reshape(1, d), "g_ffn": ffn_norm_g[l].reshape(1, d),
            "w_in_a": w_in_a[l], "w_gate": w_gate[l], "w_ukv": w_ukv[l], "w_uq": w_uq[l],
            "g_kv": mla_kv_norm_g[l].reshape(1, MLA_KV_RANK), "g_cq": mla_q_norm_g[l].reshape(1, MLA_Q_RANK),
            "g_k": g_k[l], "g_kr": g_kr[l], "g_q": g_q[l], "g_kd": g_kd[l], "g_qd": g_qd[l],
            "g_sub": g_sub[l].reshape(1, DIFF_V), "b_k": b_k, "b_q": b_q, "b_d": b_d, "v_one": v_one,
            "w_o_mla": w_o_mla_p[l], "w_o_diff": w_o_diff[l].astype(BF16), "w_out": w_out[l].astype(BF16),
            "w_gu": w_gu[l], "w_dn": w_dn[l], "lam_init": lam_inits[l],
            "q_aug": q_aug[l].reshape(1, MLA_W), "qd_aug": qd_aug[l].reshape(1, LANES),
            "flag_m": flag_m[l].astype(jnp.int32).reshape(1), "flag_d": flag_d[l].astype(jnp.int32).reshape(1),
        })
    return layers


def kernel(x, c, ctx, c_ctx, ada_w, ada_b, attn_norm_g, ffn_norm_g, w_in, mla_q_norm_g, mla_w_uq, mla_kv_norm_g, mla_w_ukv, mla_q_g, mla_k_g, diff_q_g, diff_k_g, diff_lam, diff_sub_g, w_o_mla, w_o_diff, w_out, router_w, router_b, moe_w_gate, moe_w_up, moe_w_down, sh_w_gate, sh_w_up, sh_w_down):
    n_batch, seq, d = x.shape
    n_ctx = ctx.shape[1]
    depth = w_in.shape[0]
    assert d == D_MODEL and n_ctx == ROW_TILE and seq % GRID_W == 0
    assert seq % MLA_TQ == 0 and seq % DIFF_TQ == 0 and ((seq + n_ctx) % KV_CHUNK) % 16 == 0
    assert (n_batch * seq) % MOE_TILE == 0 and (n_batch * n_ctx) % MOE_TILE == 0
    lat_rows = n_batch * seq
    all_rows = lat_rows + n_batch * n_ctx

    layers = _prep_weights(attn_norm_g, ffn_norm_g, w_in, mla_q_norm_g, mla_w_uq, mla_kv_norm_g, mla_w_ukv,
                           mla_q_g, mla_k_g, diff_q_g, diff_k_g, diff_sub_g, w_o_mla, w_o_diff, w_out,
                           moe_w_gate, moe_w_up, moe_w_down, sh_w_gate, sh_w_up, sh_w_down)
    rope_tab = _rope_table(seq, ROW_TILE)
    rw = jnp.pad(router_w, ((0, 0), (1, LANES - 1 - N_EXPERTS)))
    rb = jnp.pad(router_b, (1, LANES - 1 - N_EXPERTS)).reshape(1, LANES)

    pad_rows = (-(n_batch + 1)) % 8
    c_all = jnp.concatenate([c, c_ctx[None, :], jnp.zeros((pad_rows, d), F32)], axis=0)
    mods_all = _mods_call(c_all, ada_w, ada_b).reshape(depth, c_all.shape[0], N_MOD, d)

    xf = jnp.concatenate([x.reshape(lat_rows, d), ctx.reshape(n_batch * n_ctx, d)], axis=0)
    pending = None
    for l in range(depth):
        wl = layers[l]
        mods = mods_all[l]
        if pending is None:
            k_m, v_m, q_m, k_d, v_d, q_d = _proj_call(xf, mods, wl, rope_tab, n_batch, seq)
        else:
            xf, k_m, v_m, q_m, k_d, v_d, q_d = _proj_call(xf, mods, wl, rope_tab, n_batch, seq, prev=pending)
        last = l == depth - 1
        o_m = _attn_call(wl["flag_m"], q_m, k_m, v_m, n_batch, seq, n_ctx, MLA_HEADS, 1, [], 0.0, MLA_TQ,
                         not last, "attn_mla")
        o_d = _attn_call(wl["flag_d"], q_d, k_d, v_d, n_batch, seq, n_ctx, DIFF_HEADS, 2,
                         [diff_lam[l], wl["g_sub"]], wl["lam_init"], DIFF_TQ, not last, "attn_diff")
        out_rows = lat_rows if last else all_rows
        xf, f_ext, pos, counts = _merge_call(xf, mods, o_m, o_d, wl, rw, rb, n_batch, seq, out_rows)
        y = _moe_sparse_call(f_ext, _moe_plan(pos, counts, out_rows), wl["w_gu"], wl["w_dn"], out_rows)
        pending = (y, mods)
    xf = _residual_call(xf, pending[0], pending[1], n_batch, seq, lat_rows)
    return xf.reshape(n_batch, seq, d)
```

```python
import functools
import math

import jax
import jax.numpy as jnp
from jax import lax
from jax.experimental import pallas as pl
from jax.experimental.pallas import tpu as pltpu

F32 = jnp.float32
BF16 = jnp.bfloat16

D_MODEL = 1024
GRID_W = 64
EPS = 1e-6
ROPE_BASE = 10000.0
N_MOD = 6

MLA_HEADS = 8
MLA_NOPE = 64
MLA_ROPE = 32
MLA_V = 64
MLA_Q_RANK = 512
MLA_KV_RANK = 256
MLA_SCALE = 1.0 / math.sqrt(MLA_NOPE + MLA_ROPE)

DIFF_HEADS = 4
DIFF_DH = 64
DIFF_V = 2 * DIFF_DH
DIFF_SCALE = 1.0 / math.sqrt(DIFF_DH)

N_EXPERTS = 16
EXPERTS_PER_GROUP = 4
N_GROUPS = N_EXPERTS // EXPERTS_PER_GROUP
D_EXPERT = 512
PAIR_LO = (0, 0, 0, 1, 1, 2)
PAIR_HI = (1, 2, 3, 2, 3, 3)
N_PAIRS = len(PAIR_LO)
N_CLASSES = N_GROUPS * N_PAIRS

LANES = 128
MXU_DIM = 256
HEAD_PAD = LANES
MLA_W = MLA_HEADS * HEAD_PAD
DIFF_QK_W = DIFF_HEADS * 2 * DIFF_DH
DIFF_K_W = DIFF_HEADS * 2 * LANES
DIFF_V_PAD = 2 * LANES
DIFF_V_W = DIFF_HEADS * DIFF_V_PAD

A_CKV = 0
A_KR = A_CKV + MLA_KV_RANK
A_KD = A_KR + LANES
A_VD = A_KD + DIFF_QK_W
A_CQ = A_VD + DIFF_HEADS * DIFF_V
A_QD = A_CQ + MLA_Q_RANK
A_COLS = A_QD + DIFF_QK_W

ROPE_TAB_W = 6 * LANES
F_EXT_W = D_MODEL + LANES

ROW_TILE = 256
MOE_TILE = 512
SORT_TILE = 256
KV_CHUNK = 768
MLA_TQ = 2048
DIFF_TQ = 512
LOOP_STAGES = 4
SHIFTED_LOOP_STAGES = {1: 4, 2: 8}
LOG2E = math.log2(math.e)
VMEM_LIMIT = 56 * 1024 * 1024


def _sigmoid(x):
    return 1.0 / (1.0 + jnp.exp(-x))


def _dot(a, b):
    return jnp.dot(a, b, preferred_element_type=F32)


def _split_bf16(a):
    hi = a.astype(BF16)
    lo = (a - hi.astype(F32)).astype(BF16)
    return hi, lo


def _dot3(a, b):
    a_hi, a_lo = _split_bf16(a)
    b_hi, b_lo = _split_bf16(b)
    return _dot(a_hi, b_hi) + _dot(a_lo, b_hi) + _dot(a_hi, b_lo)


def _mods_kernel(c_ref, w_ref, b_ref, o_ref):
    c = c_ref[...]
    a = c * _sigmoid(c)
    o_ref[...] = _dot3(a, w_ref[...]) + b_ref[...]


def _mods_call(c_all, ada_w, ada_b):
    depth, d, n = ada_w.shape
    tn = 1536
    rows = c_all.shape[0]
    return pl.pallas_call(
        _mods_kernel,
        grid=(depth, n // tn),
        in_specs=[
            pl.BlockSpec((rows, d), lambda l, j: (0, 0)),
            pl.BlockSpec((None, d, tn), lambda l, j: (l, 0, j)),
            pl.BlockSpec((None, 1, tn), lambda l, j: (l, 0, j)),
        ],
        out_specs=pl.BlockSpec((None, rows, tn), lambda l, j: (l, 0, j)),
        out_shape=jax.ShapeDtypeStruct((depth, rows, n), F32),
        compiler_params=pltpu.CompilerParams(
            dimension_semantics=("parallel", "parallel"), vmem_limit_bytes=VMEM_LIMIT),
        name="adaln_mods",
    )(c_all, ada_w, ada_b.reshape(depth, 1, n))


def _rms(x, g):
    return x * lax.rsqrt(jnp.mean(x * x, axis=-1, keepdims=True) + EPS) * g


def _seg_mean_sq(x, b_ref):
    sq = (x * x).astype(BF16)
    b = b_ref[...]
    parts = [_dot(sq[:, c:c + MXU_DIM], b) for c in range(0, x.shape[1], MXU_DIM)]
    return parts[0] if len(parts) == 1 else jnp.concatenate(parts, axis=1)


def _rope(x, cos, sin_a, sin_b, half):
    return (x * cos + pltpu.roll(x, LANES - half, axis=1) * sin_a
            + pltpu.roll(x, half, axis=1) * sin_b)


def _proj_kernel(*refs, fused_residual):
    if fused_residual:
        x_ref, y_ref, modp_ref = refs[:3]
        refs = refs[3:]
        x = x_ref[...] + modp_ref[5:6, :] * y_ref[...]
    else:
        x = refs[0][...]
        refs = refs[1:]
    (mod_ref, g_ref, win_ref, wukv_ref, wuq_ref, gkv_ref, gcq_ref, gk_ref, gkr_ref, gq_ref, gkd_ref, gqd_ref,
     bk_ref, bq_ref, bd_ref, vone_ref, qaug_ref, qdaug_ref, rope_ref) = refs[:19]
    outs = refs[19:]
    if fused_residual:
        outs[0][...] = x
        outs = outs[1:]
    kmla_ref, vmla_ref, qmla_ref, kd_ref, vd_ref, qd_ref = outs
    h = _rms(x, g_ref[...]) * (1.0 + mod_ref[1:2, :]) + mod_ref[0:1, :]
    p = _dot(h.astype(BF16), win_ref[...])

    cos_m = rope_ref[:, 0 * LANES:1 * LANES]
    sin_ma = rope_ref[:, 1 * LANES:2 * LANES]
    sin_mb = rope_ref[:, 2 * LANES:3 * LANES]
    cos_d = rope_ref[:, 3 * LANES:4 * LANES]
    sin_da = rope_ref[:, 4 * LANES:5 * LANES]
    sin_db = rope_ref[:, 5 * LANES:6 * LANES]

    ckv = _rms(p[:, A_CKV:A_KR], gkv_ref[...])
    kv = _dot(ckv.astype(BF16), wukv_ref[...])
    kn = kv[:, :MLA_W]
    kn = kn * lax.rsqrt(_seg_mean_sq(kn, bk_ref) + EPS) * gk_ref[...]
    kr = p[:, A_KR:A_KD]
    kr = kr * lax.rsqrt(jnp.sum(kr * kr, axis=-1, keepdims=True) * (1.0 / MLA_ROPE) + EPS) * gkr_ref[...]
    kr = _rope(kr, cos_m, sin_ma, sin_mb, MLA_ROPE // 4)
    tm = x.shape[0]
    lane_blk = lax.broadcasted_iota(jnp.int32, (tm, LANES), 1)
    kr = kr + (lane_blk == MLA_NOPE + MLA_ROPE).astype(F32)
    for hd in range(MLA_HEADS):
        sl = slice(hd * HEAD_PAD, (hd + 1) * HEAD_PAD)
        kmla_ref[:, sl] = (kn[:, sl] + kr).astype(BF16)
    vmla_ref[...] = (kv[:, MLA_W:] + vone_ref[...]).astype(BF16)

    cq = _rms(p[:, A_CQ:A_QD], gcq_ref[...])
    q = _dot(cq.astype(BF16), wuq_ref[...])
    q = q * lax.rsqrt(_seg_mean_sq(q, bq_ref) + EPS) * gq_ref[...]
    for hd in range(MLA_HEADS):
        sl = slice(hd * HEAD_PAD, (hd + 1) * HEAD_PAD)
        qmla_ref[:, sl] = (_rope(q[:, sl], cos_m, sin_ma, sin_mb, MLA_ROPE // 4) + qaug_ref[:, sl]).astype(BF16)

    kd = p[:, A_KD:A_VD]
    kd = kd * lax.rsqrt(_seg_mean_sq(kd, bd_ref) + EPS) * gkd_ref[...]
    qd = p[:, A_QD:A_COLS]
    qd = qd * lax.rsqrt(_seg_mean_sq(qd, bd_ref) + EPS) * gqd_ref[...]
    ones_blk = (lane_blk == 0).astype(BF16)
    two_ones_blk = (lane_blk < 2).astype(BF16)
    qd_shift = jnp.broadcast_to(qdaug_ref[...], (tm, LANES)).astype(BF16)
    for hd in range(DIFF_HEADS):
        sl = slice(hd * LANES, (hd + 1) * LANES)
        lo = slice(2 * hd * LANES, (2 * hd + 1) * LANES)
        hi = slice((2 * hd + 1) * LANES, (2 * hd + 2) * LANES)
        kd_ref[:, lo] = _rope(kd[:, sl], cos_d, sin_da, sin_db, DIFF_DH // 4).astype(BF16)
        kd_ref[:, hi] = two_ones_blk
        qd_ref[:, lo] = _rope(qd[:, sl], cos_d, sin_da, sin_db, DIFF_DH // 4).astype(BF16)
        qd_ref[:, hi] = qd_shift

    vd = p[:, A_VD:A_CQ]
    for hd in range(DIFF_HEADS):
        vd_ref[:, hd * DIFF_V_PAD:hd * DIFF_V_PAD + DIFF_V] = vd[:, hd * DIFF_V:(hd + 1) * DIFF_V].astype(BF16)
        vd_ref[:, hd * DIFF_V_PAD + DIFF_V:(hd + 1) * DIFF_V_PAD] = ones_blk


def _row_class_maps(n_lat_tiles, tiles_per_batch, n_batch):
    def mod_map(i):
        return (jnp.where(i < n_lat_tiles, i // tiles_per_batch, n_batch), 0, 0)

    def rope_map(i):
        return (jnp.where(i < n_lat_tiles, i % tiles_per_batch, tiles_per_batch), 0)

    return mod_map, rope_map


def _const_spec(shape):
    return pl.BlockSpec(shape, lambda i: tuple(0 for _ in shape))


def _proj_call(x_flat, mods, wl, rope_tab, n_batch, seq, prev=None):
    rows = x_flat.shape[0]
    tm = ROW_TILE
    n_lat_tiles = n_batch * seq // tm
    mod_map, rope_map = _row_class_maps(n_lat_tiles, seq // tm, n_batch)
    row_spec = lambda w: pl.BlockSpec((tm, w), lambda i: (i, 0))
    tiles_per_batch = seq // tm

    def kv_map(i):
        lat = i + i // tiles_per_batch + 1
        ctx = (i - n_lat_tiles) * (tiles_per_batch + 1)
        return (jnp.where(i < n_lat_tiles, lat, ctx), 0)

    kv_spec = lambda w: pl.BlockSpec((tm, w), kv_map)
    consts = [wl["g_attn"], wl["w_in_a"], wl["w_ukv"], wl["w_uq"], wl["g_kv"], wl["g_cq"], wl["g_k"], wl["g_kr"],
              wl["g_q"], wl["g_kd"], wl["g_qd"], wl["b_k"], wl["b_q"], wl["b_d"], wl["v_one"], wl["q_aug"],
              wl["qd_aug"]]
    out_w = [MLA_W, MLA_W, MLA_W, DIFF_K_W, DIFF_V_W, DIFF_K_W]
    out_specs = [kv_spec(MLA_W), kv_spec(MLA_W), row_spec(MLA_W), kv_spec(DIFF_K_W), kv_spec(DIFF_V_W),
                 row_spec(DIFF_K_W)]
    mod_spec = pl.BlockSpec((None, N_MOD, D_MODEL), mod_map)
    lead_specs, lead_args = [row_spec(D_MODEL)], [x_flat]
    out_shape = [jax.ShapeDtypeStruct((rows, w), BF16) for w in out_w]
    if prev is not None:
        lead_specs += [row_spec(D_MODEL), mod_spec]
        lead_args += list(prev)
        out_specs = [row_spec(D_MODEL)] + out_specs
        out_shape = [jax.ShapeDtypeStruct((rows, D_MODEL), F32)] + out_shape
    return pl.pallas_call(
        functools.partial(_proj_kernel, fused_residual=prev is not None),
        grid=(rows // tm,),
        in_specs=lead_specs + [mod_spec] + [_const_spec(c.shape) for c in consts]
        + [pl.BlockSpec((tm, ROPE_TAB_W), rope_map)],
        out_specs=out_specs,
        out_shape=out_shape,
        compiler_params=pltpu.CompilerParams(
            dimension_semantics=("parallel",), vmem_limit_bytes=VMEM_LIMIT),
        name="proj_qkv",
    )(*lead_args, mods, *consts, rope_tab)


SAFE_SHIFT = 50.0
SHIFT_MARGIN = 1.02


def _attn_kernel(flag_ref, *refs, n_maps, tq, tk, chunks, lam_init, aliased_out):
    refs = list(refs)
    q_ref, k_ref, v_ref = refs[:3]
    n_in = 3 + (2 if n_maps == 2 else 0) + (1 if aliased_out else 0)
    o_ref = refs[n_in]
    if n_maps == 2:
        lam_ref, subg_ref = refs[3:5]
        qs_ref, s_ref, p_ref, m_ref, a_ref, acc_ref = refs[n_in + 1:]
        qf = q_ref[...].astype(F32)
        lane = lax.broadcasted_iota(jnp.int32, qf.shape, 1)
        keep0 = (lane < DIFF_DH) | (lane == 2 * DIFF_DH)
        keep1 = ((lane >= DIFF_DH) & (lane < 2 * DIFF_DH)) | (lane == 2 * DIFF_DH + 1)
        qs_ref[0:tq, :] = jnp.where(keep0, qf, 0.0).astype(BF16)
        qs_ref[tq:2 * tq, :] = jnp.where(keep1, qf, 0.0).astype(BF16)
        q_src = qs_ref
    else:
        s_ref, p_ref, m_ref, a_ref, acc_ref = refs[n_in + 1:]
        q_src = q_ref
    vw = v_ref.shape[1]
    sum_col = vw // 2

    head, n_mid, tail = chunks
    n_chunks = 1 + n_mid + (1 if tail else 0)

    def geom(j):
        if isinstance(j, int):
            if j == 0:
                return 0, head
            if j <= n_mid:
                return head + (j - 1) * tk, tk
            return head + n_mid * tk, tail
        return pl.multiple_of(head + (j - 1) * tk, math.gcd(head, tk)), tk

    def rows_of(ref, j):
        off, size = geom(j)
        return ref[pl.ds(off, size), :]

    def scores(j, slot):
        size = geom(j)[1]
        s_ref[slot, :, :size] = lax.dot_general(q_src[...], rows_of(k_ref, j), (((1,), (1,)), ((), ())),
                                                preferred_element_type=F32)

    def probs_shifted(j, slot):
        size = geom(j)[1]
        p_ref[slot, :, :size] = jnp.exp2(s_ref[slot, :, :size]).astype(BF16)

    def accumulate_plain(j, slot):
        size = geom(j)[1]
        r = _dot(p_ref[slot, :, :size], rows_of(v_ref, j))
        if isinstance(j, int) and j == 0:
            acc_ref[...] = r
        else:
            acc_ref[...] += r

    def probs_online(j, slot):
        size = geom(j)[1]
        s = s_ref[slot, :, :size]
        mx = jnp.max(s, axis=-1, keepdims=True)
        if isinstance(j, int) and j == 0:
            m_new = mx
        else:
            m_old = m_ref[...]
            m_new = jnp.maximum(m_old, mx)
            a_ref[slot] = jnp.exp2(m_old - m_new)
        m_ref[...] = m_new
        p_ref[slot, :, :size] = jnp.exp2(s - m_new).astype(BF16)

    def accumulate_online(j, slot):
        size = geom(j)[1]
        r = _dot(p_ref[slot, :, :size], rows_of(v_ref, j))
        if isinstance(j, int) and j == 0:
            acc_ref[...] = r
        else:
            acc_ref[...] = acc_ref[...] * a_ref[slot] + r

    def sweep(probs, accumulate, loop_stages):
        def stage(j, slot, with_scores):
            probs(j, slot)
            accumulate(j - 1, 1 - slot)
            if with_scores:
                scores(j + 1, 1 - slot)

        scores(0, 0)
        if n_chunks > 1:
            scores(1, 1)
        probs(0, 0)
        if n_chunks > 1:
            stage(1, 1, n_chunks > 2)
            n_loop = max(n_mid - 2, 0) // loop_stages

            def group(i, carry):
                j0 = 2 + loop_stages * i
                for t in range(loop_stages):
                    stage(j0 + t, t % 2, True)
                return carry

            lax.fori_loop(0, n_loop, group, 0)
            for j in range(2 + loop_stages * n_loop, n_chunks):
                stage(j, j % 2, j + 1 < n_chunks)
        accumulate(n_chunks - 1, (n_chunks - 1) % 2)

    @pl.when(flag_ref[0] == 1)
    def _():
        sweep(probs_shifted, accumulate_plain, SHIFTED_LOOP_STAGES[n_maps])

    @pl.when(flag_ref[0] != 1)
    def _():
        sweep(probs_online, accumulate_online, LOOP_STAGES)

    acc = acc_ref[...]
    inv_l = 1.0 / acc[:, sum_col:sum_col + 1]
    if n_maps == 1:
        o_ref[...] = (acc * inv_l).astype(o_ref.dtype)
    else:
        lf = lam_ref[...]
        lam = (jnp.exp(jnp.sum(lf[0:1, :] * lf[1:2, :], axis=-1, keepdims=True))
               - jnp.exp(jnp.sum(lf[2:3, :] * lf[3:4, :], axis=-1, keepdims=True)) + lam_init)
        o = acc[:, :DIFF_V] * inv_l
        o = o[:tq] - lam * o[tq:]
        o_ref[...] = _rms(o, subg_ref[...]).astype(o_ref.dtype)


def _attn_scratch(n_maps, tq, tk, kw, vw):
    r = n_maps * tq
    qs = [pltpu.VMEM((r, kw), BF16)] if n_maps == 2 else []
    return qs + [pltpu.VMEM((2, r, tk), F32), pltpu.VMEM((2, r, tk), BF16), pltpu.VMEM((r, 1), F32),
                 pltpu.VMEM((2, r, 1), F32), pltpu.VMEM((r, vw), F32)]


def _attn_call(flag, q, k, v, n_batch, seq, n_ctx, n_heads, n_maps, extra, lam_init, tq, with_ctx_queries, name):
    rows = q.shape[0]
    vw = v.shape[1] // n_heads
    kw = k.shape[1] // n_heads
    qw = q.shape[1] // n_heads
    keys = seq + n_ctx
    tk = KV_CHUNK
    nq = seq // tq
    assert seq % tq == 0
    chunks = (tk, keys // tk - 1, keys % tk)
    ctx_blk0 = n_batch * seq // n_ctx
    blocks_per_batch = keys // n_ctx
    extra_specs = [pl.BlockSpec(e.shape, lambda *_: (0, 0)) for e in extra]
    out_shape = jax.ShapeDtypeStruct((rows, n_heads * LANES), BF16)

    kern = functools.partial(_attn_kernel, n_maps=n_maps, tq=tq, tk=tk, chunks=chunks, lam_init=lam_init,
                             aliased_out=False)
    out = pl.pallas_call(
        kern,
        grid_spec=pltpu.PrefetchScalarGridSpec(
            num_scalar_prefetch=1,
            grid=(n_batch, n_heads, nq),
            in_specs=[pl.BlockSpec((tq, qw), lambda b, h, qi, f: (b * nq + qi, h)),
                      pl.BlockSpec((keys, kw), lambda b, h, qi, f: (b, h)),
                      pl.BlockSpec((keys, vw), lambda b, h, qi, f: (b, h))] + extra_specs,
            out_specs=pl.BlockSpec((tq, LANES), lambda b, h, qi, f: (b * nq + qi, h)),
            scratch_shapes=_attn_scratch(n_maps, tq, tk, kw, vw)),
        out_shape=out_shape,
        compiler_params=pltpu.CompilerParams(
            dimension_semantics=("parallel", "parallel", "arbitrary"), vmem_limit_bytes=VMEM_LIMIT),
        name=name,
    )(flag, q, k, v, *extra)
    if not with_ctx_queries:
        return out

    kern_c = functools.partial(_attn_kernel, n_maps=n_maps, tq=n_ctx, tk=n_ctx, chunks=(n_ctx, 0, 0),
                               lam_init=lam_init, aliased_out=True)
    return pl.pallas_call(
        kern_c,
        grid_spec=pltpu.PrefetchScalarGridSpec(
            num_scalar_prefetch=1,
            grid=(n_batch, n_heads),
            in_specs=[pl.BlockSpec((n_ctx, qw), lambda b, h, f: (ctx_blk0 + b, h)),
                      pl.BlockSpec((n_ctx, kw), lambda b, h, f: (b * blocks_per_batch, h)),
                      pl.BlockSpec((n_ctx, vw), lambda b, h, f: (b * blocks_per_batch, h))] + extra_specs
            + [pl.BlockSpec(memory_space=pl.ANY)],
            out_specs=pl.BlockSpec((n_ctx, LANES), lambda b, h, f: (ctx_blk0 + b, h)),
            scratch_shapes=_attn_scratch(n_maps, n_ctx, n_ctx, kw, vw)),
        out_shape=out_shape,
        input_output_aliases={4 + len(extra): 0},
        compiler_params=pltpu.CompilerParams(
            dimension_semantics=("parallel", "parallel"), vmem_limit_bytes=VMEM_LIMIT),
        name=name + "_ctx",
    )(flag, q, k, v, *extra, out)


def _route(logits, bias):
    lane = lax.broadcasted_iota(jnp.int32, logits.shape, 1)
    is_exp = (lane >= 1) & (lane <= N_EXPERTS)
    grp = (lane - 1) >> 2
    neg = jnp.float32(-jnp.inf)
    big = jnp.int32(1 << 20)
    aff = _sigmoid(logits)
    sel = jnp.where(is_exp, aff + bias, neg)

    def first_argmax(v):
        mx = jnp.max(v, axis=-1, keepdims=True)
        ix = jnp.min(jnp.where(v == mx, lane, big), axis=-1, keepdims=True)
        return mx, ix

    best = None
    for g in range(N_GROUPS):
        vg = jnp.where(grp == g, sel, neg)
        m1, i1 = first_argmax(vg)
        m2, i2 = first_argmax(jnp.where(lane == i1, neg, vg))
        score = m1 + m2
        if best is None:
            best = (score, i1, i2)
        else:
            upd = score > best[0]
            best = (jnp.where(upd, score, best[0]), jnp.where(upd, i1, best[1]), jnp.where(upd, i2, best[2]))
    _, i1, i2 = best
    a1 = jnp.sum(jnp.where(lane == i1, aff, 0.0), axis=-1, keepdims=True)
    a2 = jnp.sum(jnp.where(lane == i2, aff, 0.0), axis=-1, keepdims=True)
    tot = a1 + a2
    e1 = i1 - 1
    e2 = i2 - 1
    swap = e2 < e1
    lo = jnp.where(swap, e2, e1)
    hi = jnp.where(swap, e1, e2)
    w_lo = jnp.where(swap, a2, a1) / tot
    w_hi = jnp.where(swap, a1, a2) / tot
    l = lo & (EXPERTS_PER_GROUP - 1)
    h = hi & (EXPERTS_PER_GROUP - 1)
    cls = (lo >> 2) * N_PAIRS + ((l * (7 - l)) >> 1) + (h - l - 1)
    return cls, w_lo, w_hi


def _merge_kernel(x_ref, mod_ref, ga_ref, gf_ref, om_ref, od_ref, wgate_ref, wom_ref, wod_ref, wout_ref,
                  rw_ref, rb_ref, tri_ref, xo_ref, f_ref, pos_ref, cnt_ref, *, cap):
    @pl.when(pl.program_id(0) == 0)
    def _():
        cnt_ref[...] = jnp.zeros_like(cnt_ref)

    x = x_ref[...]
    h = _rms(x, ga_ref[...]) * (1.0 + mod_ref[1:2, :]) + mod_ref[0:1, :]
    gates = _sigmoid(_dot(h.astype(BF16), wgate_ref[...]))
    y = (gates[:, :D_MODEL] * _dot(om_ref[...], wom_ref[...])
         + gates[:, D_MODEL:] * _dot(od_ref[...], wod_ref[...]))
    xn = x + mod_ref[2:3, :] * _dot(y.astype(BF16), wout_ref[...])
    xo_ref[...] = xn
    f = _rms(xn, gf_ref[...]) * (1.0 + mod_ref[4:5, :]) + mod_ref[3:4, :]
    cls, w_lo, w_hi = _route(_dot3(f, rw_ref[...]), rb_ref[...])
    lane = lax.broadcasted_iota(jnp.int32, (x.shape[0], LANES), 1)
    f_ref[:, :D_MODEL] = f
    f_ref[:, D_MODEL:] = jnp.where(lane == 0, w_lo, jnp.where(lane == 1, w_hi, 0.0))
    onehot = (lane == cls).astype(F32)
    before = _dot(tri_ref[...], onehot.astype(BF16)) + cnt_ref[...]
    rank = jnp.sum(onehot * before, axis=-1, keepdims=True)
    cnt_ref[...] += jnp.sum(onehot, axis=0, keepdims=True)
    pos = cls.astype(F32) * float(cap) + rank
    pos_ref[...] = jnp.where(lane == 0, pos, 0.0)


def _merge_call(x_flat, mods, o_mla, o_diff, wl, router_w, router_b, n_batch, seq, rows):
    tm = ROW_TILE
    mod_map, _ = _row_class_maps(n_batch * seq // tm, seq // tm, n_batch)
    row_spec = lambda w: pl.BlockSpec((tm, w), lambda i: (i, 0))
    tri = (jnp.arange(tm)[:, None] > jnp.arange(tm)[None, :]).astype(BF16)
    consts = [wl["w_gate"], wl["w_o_mla"], wl["w_o_diff"], wl["w_out"], router_w, router_b, tri]
    return pl.pallas_call(
        functools.partial(_merge_kernel, cap=rows),
        grid=(rows // tm,),
        in_specs=[row_spec(D_MODEL), pl.BlockSpec((None, N_MOD, D_MODEL), mod_map),
                  _const_spec(wl["g_attn"].shape), _const_spec(wl["g_ffn"].shape),
                  row_spec(MLA_W), row_spec(DIFF_HEADS * LANES)]
        + [_const_spec(c.shape) for c in consts],
        out_specs=[row_spec(D_MODEL), row_spec(F_EXT_W), row_spec(LANES), _const_spec((1, LANES))],
        out_shape=[jax.ShapeDtypeStruct((rows, D_MODEL), F32), jax.ShapeDtypeStruct((rows, F_EXT_W), F32),
                   jax.ShapeDtypeStruct((rows, LANES), F32), jax.ShapeDtypeStruct((1, LANES), F32)],
        compiler_params=pltpu.CompilerParams(
            dimension_semantics=("arbitrary",), vmem_limit_bytes=VMEM_LIMIT),
        name="merge_router",
    )(x_flat, mods, wl["g_attn"], wl["g_ffn"], o_mla, o_diff, *consts)


def _moe_plan(pos_rec, counts_rec, n_rows):
    t = SORT_TILE
    tiles_per_bucket = n_rows // t
    counts = counts_rec[0, :N_CLASSES].astype(jnp.int32)
    pos = pos_rec[:, 0].astype(jnp.int32)
    src = jnp.zeros((N_CLASSES * n_rows,), jnp.int32).at[pos].set(jnp.arange(n_rows, dtype=jnp.int32))
    class_tiles = (counts + t - 1) // t
    tile_end = jnp.cumsum(class_tiles)
    n_tiles = tiles_per_bucket + N_CLASSES
    tile = jnp.arange(n_tiles, dtype=jnp.int32)
    tile_cls = jnp.minimum(jnp.searchsorted(tile_end, tile, side="right", method="compare_all"),
                           N_CLASSES - 1).astype(jnp.int32)
    in_class = tile - (tile_end - class_tiles)[tile_cls]
    n_used = tile_end[-1].reshape(1)
    used = tile < n_used[0]
    blk = jnp.where(used, tile_cls * tiles_per_bucket + in_class, 0)
    n_valid = jnp.where(used, jnp.clip(counts[tile_cls] - in_class * t, 0, t), 0)
    base = (tile_cls // N_PAIRS) * EXPERTS_PER_GROUP
    pair = tile_cls % N_PAIRS
    ea = base + jnp.asarray(PAIR_LO, jnp.int32)[pair]
    eb = base + jnp.asarray(PAIR_HI, jnp.int32)[pair]
    return ea, eb, n_used, blk, n_valid, src.reshape(N_CLASSES * tiles_per_bucket, 1, t)


def _moe_sparse_kernel(ea_ref, eb_ref, nu_ref, blk_ref, nv_ref, src0_ref, srcn_ref, srcc_ref, f_hbm,
                       wgu_s, wd_s, wgu_a, wd_a, wgu_b, wd_b, y_hbm, xbuf, ybuf, gsem, ssem, *, n_rows):
    del ea_ref, eb_ref, blk_ref
    t = SORT_TILE
    i = pl.program_id(0)
    n_used = nu_ref[0]
    n_valid = nv_ref[i]
    slot = i % 2

    def start_gather(idx_ref, s):
        for r in range(t):
            pltpu.make_async_copy(f_hbm.at[pl.ds(idx_ref[0, r], 1), :], xbuf.at[s, pl.ds(r, 1), :],
                                  gsem.at[s]).start()

    def gather_done(s):
        return pltpu.make_async_copy(f_hbm.at[pl.ds(0, t), :], xbuf.at[s], gsem.at[s])

    def scatter_done(s):
        return pltpu.make_async_copy(ybuf.at[s], y_hbm.at[pl.ds(0, t), :], ssem.at[s])

    @pl.when(i == 0)
    def _():
        start_gather(src0_ref, 0)

    @pl.when(i < n_used)
    def _():
        gather_done(slot).wait()
        start_gather(srcn_ref, 1 - slot)
        x = xbuf[slot, :, :D_MODEL].astype(BF16)
        w = xbuf[slot, :, D_MODEL:]

        def ffn(wgu_ref, wd_ref, scale):
            gu = _dot(x, wgu_ref[...])
            g = gu[:, :D_EXPERT]
            he = g * _sigmoid(g) * gu[:, D_EXPERT:]
            if scale is not None:
                he = he * scale
            return _dot(he.astype(BF16), wd_ref[...])

        ybuf[slot] = ffn(wgu_s, wd_s, None) + ffn(wgu_a, wd_a, w[:, 0:1]) + ffn(wgu_b, wd_b, w[:, 1:2])
        for r in range(t):
            dst = jnp.where(r < n_valid, srcc_ref[0, r], n_rows + slot * t + r)
            pltpu.make_async_copy(ybuf.at[slot, pl.ds(r, 1), :], y_hbm.at[pl.ds(dst, 1), :],
                                  ssem.at[slot]).start()

        @pl.when(i >= 1)
        def _():
            scatter_done(1 - slot).wait()

        @pl.when(i == n_used - 1)
        def _():
            scatter_done(slot).wait()
            gather_done(1 - slot).wait()


def _moe_sparse_call(f_ext, plan, w_gu, w_down, n_rows):
    ea, eb, n_used, blk, n_valid, src = plan
    t = SORT_TILE
    n_tiles = blk.shape[0]
    smem_spec = lambda imap: pl.BlockSpec((None, 1, t), imap, memory_space=pltpu.SMEM)
    wspec = lambda shape, imap: pl.BlockSpec((None,) + shape, imap)
    gu_shape = (D_MODEL, 2 * D_EXPERT)
    dn_shape = (D_EXPERT, D_MODEL)
    grid_spec = pltpu.PrefetchScalarGridSpec(
        num_scalar_prefetch=5,
        grid=(n_tiles,),
        in_specs=[
            smem_spec(lambda i, ea, eb, nu, blk, nv: (blk[0], 0, 0)),
            smem_spec(lambda i, ea, eb, nu, blk, nv: (blk[jnp.minimum(i + 1, n_tiles - 1)], 0, 0)),
            smem_spec(lambda i, ea, eb, nu, blk, nv: (blk[i], 0, 0)),
            pl.BlockSpec(memory_space=pl.ANY),
            wspec(gu_shape, lambda i, ea, eb, nu, blk, nv: (0, 0, 0)),
            wspec(dn_shape, lambda i, ea, eb, nu, blk, nv: (0, 0, 0)),
            wspec(gu_shape, lambda i, ea, eb, nu, blk, nv: (ea[i] + 1, 0, 0)),
            wspec(dn_shape, lambda i, ea, eb, nu, blk, nv: (ea[i] + 1, 0, 0)),
            wspec(gu_shape, lambda i, ea, eb, nu, blk, nv: (eb[i] + 1, 0, 0)),
            wspec(dn_shape, lambda i, ea, eb, nu, blk, nv: (eb[i] + 1, 0, 0)),
        ],
        out_specs=pl.BlockSpec(memory_space=pl.ANY),
        scratch_shapes=[pltpu.VMEM((2, t, F_EXT_W), F32), pltpu.VMEM((2, t, D_MODEL), F32),
                        pltpu.SemaphoreType.DMA((2,)), pltpu.SemaphoreType.DMA((2,))],
    )
    return pl.pallas_call(
        functools.partial(_moe_sparse_kernel, n_rows=n_rows),
        grid_spec=grid_spec,
        out_shape=jax.ShapeDtypeStruct((n_rows + 2 * t, D_MODEL), F32),
        compiler_params=pltpu.CompilerParams(
            dimension_semantics=("arbitrary",), vmem_limit_bytes=VMEM_LIMIT),
        name="moe_sorted",
    )(ea, eb, n_used, blk, n_valid, src, src, src, f_ext, w_gu, w_down, w_gu, w_down, w_gu, w_down)


def _residual_kernel(x_ref, y_ref, mod_ref, o_ref):
    o_ref[...] = x_ref[...] + mod_ref[5:6, :] * y_ref[...]


def _residual_call(x_flat, y, mods, n_batch, seq, out_rows):
    tm = MOE_TILE
    mod_map, _ = _row_class_maps(n_batch * seq // tm, seq // tm, n_batch)
    row_spec = pl.BlockSpec((tm, D_MODEL), lambda i: (i, 0))
    return pl.pallas_call(
        _residual_kernel,
        grid=(out_rows // tm,),
        in_specs=[row_spec, row_spec, pl.BlockSpec((None, N_MOD, D_MODEL), mod_map)],
        out_specs=row_spec,
        out_shape=jax.ShapeDtypeStruct((out_rows, D_MODEL), F32),
        compiler_params=pltpu.CompilerParams(
            dimension_semantics=("parallel",), vmem_limit_bytes=VMEM_LIMIT),
        name="ffn_residual",
    )(x_flat, y, mods)


def _axial_angles(seq, d_rope):
    d_axis = d_rope // 2
    inv = ROPE_BASE ** (-jnp.arange(0, d_axis, 2, dtype=F32) / d_axis)
    t = jnp.arange(seq, dtype=jnp.int32)
    r = (t // GRID_W).astype(F32)
    col = (t % GRID_W).astype(F32)
    ang_r = r[:, None] * inv
    ang_c = col[:, None] * inv
    cos = jnp.concatenate([jnp.cos(ang_r)] * 2 + [jnp.cos(ang_c)] * 2, axis=1)
    sin = jnp.concatenate([jnp.sin(ang_r)] * 2 + [jnp.sin(ang_c)] * 2, axis=1)
    nf = d_rope // 4
    first = jnp.concatenate([jnp.ones((nf,), F32), jnp.zeros((nf,), F32)] * 2)
    return cos, -sin * first, sin * (1.0 - first)


def _rope_table(seq, tile):
    cm, sma, smb = _axial_angles(seq, MLA_ROPE)
    pad_l = jnp.zeros((seq, MLA_NOPE), F32)
    pad_r = jnp.zeros((seq, HEAD_PAD - MLA_NOPE - MLA_ROPE), F32)
    cos_m = jnp.concatenate([pad_l + 1.0, cm, pad_r + 1.0], axis=1)
    sin_ma = jnp.concatenate([pad_l, sma, pad_r], axis=1)
    sin_mb = jnp.concatenate([pad_l, smb, pad_r], axis=1)
    cd, sda, sdb = _axial_angles(seq, DIFF_DH)
    tab = jnp.concatenate([cos_m, sin_ma, sin_mb] + [jnp.concatenate([a, a], axis=1) for a in (cd, sda, sdb)],
                          axis=1)
    ident = jnp.concatenate([jnp.ones((tile, LANES), F32), jnp.zeros((tile, 2 * LANES), F32)] * 2, axis=1)
    return jnp.concatenate([tab, ident], axis=0)


def _block_diag_mean(seg_sizes):
    m = jnp.zeros((MXU_DIM, MXU_DIM), F32)
    off = 0
    for size, active in seg_sizes:
        if active:
            m = m.at[off:off + size, off:off + size].set(1.0 / size)
        off += size
    assert off == MXU_DIM
    return m.astype(BF16)


def _prep_weights(attn_norm_g, ffn_norm_g, w_in, mla_q_norm_g, mla_w_uq, mla_kv_norm_g, mla_w_ukv, mla_q_g,
                  mla_k_g, diff_q_g, diff_k_g, diff_sub_g, w_o_mla, w_o_diff, w_out, moe_w_gate, moe_w_up,
                  moe_w_down, sh_w_gate, sh_w_up, sh_w_down):
    depth = w_in.shape[0]
    d = D_MODEL
    z = lambda w: jnp.zeros((depth, d, w), F32)
    off_kr = MLA_KV_RANK
    off_kd = off_kr + MLA_ROPE
    off_vd = off_kd + DIFF_QK_W
    off_cq = off_vd + DIFF_HEADS * DIFF_V
    off_qd = off_cq + MLA_Q_RANK
    off_gate = off_qd + DIFF_QK_W
    w_in_a = jnp.concatenate([
        w_in[:, :, :off_kr], z(MLA_NOPE), w_in[:, :, off_kr:off_kd], z(HEAD_PAD - MLA_NOPE - MLA_ROPE),
        w_in[:, :, off_kd:off_gate]], axis=2).astype(BF16)
    w_gate = w_in[:, :, off_gate:].astype(BF16)

    ukv = mla_w_ukv.reshape(depth, MLA_KV_RANK, MLA_HEADS, MLA_NOPE + MLA_V)
    pad_k = jnp.zeros((depth, MLA_KV_RANK, MLA_HEADS, HEAD_PAD - MLA_NOPE), F32)
    pad_v = jnp.zeros((depth, MLA_KV_RANK, MLA_HEADS, HEAD_PAD - MLA_V), F32)
    w_ukv = jnp.concatenate([
        jnp.concatenate([ukv[..., :MLA_NOPE], pad_k], axis=-1).reshape(depth, MLA_KV_RANK, MLA_W),
        jnp.concatenate([ukv[..., MLA_NOPE:], pad_v], axis=-1).reshape(depth, MLA_KV_RANK, MLA_W)],
        axis=-1).astype(BF16)
    uq = mla_w_uq.reshape(depth, MLA_Q_RANK, MLA_HEADS, MLA_NOPE + MLA_ROPE)
    w_uq = jnp.pad(uq, ((0, 0), (0, 0), (0, 0), (0, HEAD_PAD - MLA_NOPE - MLA_ROPE))).reshape(
        depth, MLA_Q_RANK, MLA_W).astype(BF16)
    wom = w_o_mla.reshape(depth, MLA_HEADS, MLA_V, d)
    w_o_mla_p = jnp.pad(wom, ((0, 0), (0, 0), (0, HEAD_PAD - MLA_V), (0, 0))).reshape(depth, MLA_W, d).astype(BF16)

    def per_head(g96, scale):
        g = jnp.pad(g96 * scale, ((0, 0), (0, HEAD_PAD - MLA_NOPE - MLA_ROPE)))
        return jnp.tile(g, (1, MLA_HEADS)).reshape(depth, 1, MLA_W)

    k_nope_only = jnp.concatenate([mla_k_g[:, :MLA_NOPE], jnp.zeros((depth, MLA_ROPE), F32)], axis=1)
    g_k = per_head(k_nope_only, 1.0)
    g_kr = jnp.concatenate([jnp.zeros((depth, MLA_NOPE), F32), mla_k_g[:, MLA_NOPE:],
                            jnp.zeros((depth, HEAD_PAD - MLA_NOPE - MLA_ROPE), F32)], axis=1).reshape(depth, 1, LANES)
    g_q = per_head(mla_q_g, MLA_SCALE * LOG2E)
    g_kd = jnp.tile(diff_k_g.reshape(depth, 2 * DIFF_DH), (1, DIFF_HEADS)).reshape(depth, 1, DIFF_QK_W)
    g_qd = jnp.tile(diff_q_g.reshape(depth, 2 * DIFF_DH) * (DIFF_SCALE * LOG2E),
                    (1, DIFF_HEADS)).reshape(depth, 1, DIFF_QK_W)
    lam_inits = [0.8 - 0.6 * math.exp(-0.3 * l) for l in range(depth)]
    g_sub = diff_sub_g * (1.0 - jnp.asarray(lam_inits, F32))[:, None]

    def seg_norm2(g, n):
        return n * jnp.max(g * g, axis=-1)

    q_n2 = seg_norm2(mla_q_g[:, :MLA_NOPE], MLA_NOPE) + seg_norm2(mla_q_g[:, MLA_NOPE:], MLA_ROPE)
    k_n2 = seg_norm2(mla_k_g[:, :MLA_NOPE], MLA_NOPE) + seg_norm2(mla_k_g[:, MLA_NOPE:], MLA_ROPE)
    shift_m = (jnp.sqrt(q_n2 * k_n2) * (MLA_SCALE * LOG2E * SHIFT_MARGIN)).astype(BF16).astype(F32)
    shift_d = (jnp.sqrt(seg_norm2(diff_q_g, DIFF_DH) * seg_norm2(diff_k_g, DIFF_DH))
               * (DIFF_SCALE * LOG2E * SHIFT_MARGIN)).astype(BF16).astype(F32)
    flag_m = shift_m <= SAFE_SHIFT
    flag_d = jnp.max(shift_d, axis=-1) <= SAFE_SHIFT
    aug_lane = (jnp.arange(HEAD_PAD) == MLA_NOPE + MLA_ROPE).astype(F32)
    q_aug = jnp.tile(aug_lane[None, :] * jnp.where(flag_m, -shift_m, 0.0)[:, None], (1, MLA_HEADS))
    qd_aug = jnp.pad(jnp.where(flag_d[:, None], -shift_d, 0.0), ((0, 0), (0, LANES - 2)))

    v_one = jnp.tile((jnp.arange(HEAD_PAD) == MLA_V).astype(F32), MLA_HEADS).reshape(1, MLA_W)
    b_k = _block_diag_mean([(MLA_NOPE, True), (HEAD_PAD - MLA_NOPE, False)] * 2)
    b_q = _block_diag_mean([(MLA_NOPE, True), (MLA_ROPE, True), (HEAD_PAD - MLA_NOPE - MLA_ROPE, False)] * 2)
    b_d = _block_diag_mean([(DIFF_DH, True)] * 4)

    w_gu = jnp.concatenate([
        jnp.concatenate([sh_w_gate, sh_w_up], axis=-1)[:, None].astype(BF16),
        jnp.concatenate([moe_w_gate.astype(BF16), moe_w_up.astype(BF16)], axis=-1)], axis=1)
    w_dn = jnp.concatenate([sh_w_down[:, None].astype(BF16), moe_w_down.astype(BF16)], axis=1)

    layers = []
    for l in range(depth):
        layers.append({
            "g_attn": attn_norm_g[l].reshape(1, d), "g_ffn": ffn_norm_g[l].reshape(1, d),
            "w_in_a": w_in_a[l], "w_gate": w_gate[l], "w_ukv": w_ukv[l], "w_uq": w_uq[l],
            "g_kv": mla_kv_norm_g[l].reshape(1, MLA_KV_RANK), "g_cq": mla_q_norm_g[l].reshape(1, MLA_Q_RANK),
            "g_k": g_k[l], "g_kr": g_kr[l], "g_q": g_q[l], "g_kd": g_kd[l], "g_qd": g_qd[l],
            "g_sub": g_sub[l].reshape(1, DIFF_V), "b_k": b_k, "b_q": b_q, "b_d": b_d, "v_one": v_one,
            "w_o_mla": w_o_mla_p[l], "w_o_diff": w_o_diff[l].astype(BF16), "w_out": w_out[l].astype(BF16),
            "w_gu": w_gu[l], "w_dn": w_dn[l], "lam_init": lam_inits[l],
            "q_aug": q_aug[l].reshape(1, MLA_W), "qd_aug": qd_aug[l].reshape(1, LANES),
            "flag_m": flag_m[l].astype(jnp.int32).reshape(1), "flag_d": flag_d[l].astype(jnp.int32).reshape(1),
        })
    return layers


def kernel(x, c, ctx, c_ctx, ada_w, ada_b, attn_norm_g, ffn_norm_g, w_in, mla_q_norm_g, mla_w_uq, mla_kv_norm_g, mla_w_ukv, mla_q_g, mla_k_g, diff_q_g, diff_k_g, diff_lam, diff_sub_g, w_o_mla, w_o_diff, w_out, router_w, router_b, moe_w_gate, moe_w_up, moe_w_down, sh_w_gate, sh_w_up, sh_w_down):
    n_batch, seq, d = x.shape
    n_ctx = ctx.shape[1]
    depth = w_in.shape[0]
    assert d == D_MODEL and n_ctx == ROW_TILE and seq % GRID_W == 0
    assert seq % MLA_TQ == 0 and seq % DIFF_TQ == 0 and ((seq + n_ctx) % KV_CHUNK) % 16 == 0
    assert (n_batch * seq) % MOE_TILE == 0 and (n_batch * n_ctx) % MOE_TILE == 0
    lat_rows = n_batch * seq
    all_rows = lat_rows + n_batch * n_ctx

    layers = _prep_weights(attn_norm_g, ffn_norm_g, w_in, mla_q_norm_g, mla_w_uq, mla_kv_norm_g, mla_w_ukv,
                           mla_q_g, mla_k_g, diff_q_g, diff_k_g, diff_sub_g, w_o_mla, w_o_diff, w_out,
                           moe_w_gate, moe_w_up, moe_w_down, sh_w_gate, sh_w_up, sh_w_down)
    rope_tab = _rope_table(seq, ROW_TILE)
    rw = jnp.pad(router_w, ((0, 0), (1, LANES - 1 - N_EXPERTS)))
    rb = jnp.pad(router_b, (1, LANES - 1 - N_EXPERTS)).reshape(1, LANES)

    pad_rows = (-(n_batch + 1)) % 8
    c_all = jnp.concatenate([c, c_ctx[None, :], jnp.zeros((pad_rows, d), F32)], axis=0)
    mods_all = _mods_call(c_all, ada_w, ada_b).reshape(depth, c_all.shape[0], N_MOD, d)

    xf = jnp.concatenate([x.reshape(lat_rows, d), ctx.reshape(n_batch * n_ctx, d)], axis=0)
    pending = None
    for l in range(depth):
        wl = layers[l]
        mods = mods_all[l]
        if pending is None:
            k_m, v_m, q_m, k_d, v_d, q_d = _proj_call(xf, mods, wl, rope_tab, n_batch, seq)
        else:
            xf, k_m, v_m, q_m, k_d, v_d, q_d = _proj_call(xf, mods, wl, rope_tab, n_batch, seq, prev=pending)
        last = l == depth - 1
        o_m = _attn_call(wl["flag_m"], q_m, k_m, v_m, n_batch, seq, n_ctx, MLA_HEADS, 1, [], 0.0, MLA_TQ,
                         not last, "attn_mla")
        o_d = _attn_call(wl["flag_d"], q_d, k_d, v_d, n_batch, seq, n_ctx, DIFF_HEADS, 2,
                         [diff_lam[l], wl["g_sub"]], wl["lam_init"], DIFF_TQ, not last, "attn_diff")
        out_rows = lat_rows if last else all_rows
        xf, f_ext, pos, counts = _merge_call(xf, mods, o_m, o_d, wl, rw, rb, n_batch, seq, out_rows)
        y = _moe_sparse_call(f_ext, _moe_plan(pos, counts, out_rows), wl["w_gu"], wl["w_dn"], out_rows)
        pending = (y, mods)
    xf = _residual_call(xf, pending[0], pending[1], n_batch, seq, lat_rows)
    return xf.reshape(n_batch, seq, d)
```

```python
import functools
import math

import jax
import jax.numpy as jnp
from jax import lax
from jax.experimental import pallas as pl
from jax.experimental.pallas import tpu as pltpu

F32 = jnp.float32
BF16 = jnp.bfloat16

D_MODEL = 1024
GRID_W = 64
EPS = 1e-6
ROPE_BASE = 10000.0
N_MOD = 6

MLA_HEADS = 8
MLA_NOPE = 64
MLA_ROPE = 32
MLA_V = 64
MLA_Q_RANK = 512
MLA_KV_RANK = 256
MLA_SCALE = 1.0 / math.sqrt(MLA_NOPE + MLA_ROPE)

DIFF_HEADS = 4
DIFF_DH = 64
DIFF_V = 2 * DIFF_DH
DIFF_SCALE = 1.0 / math.sqrt(DIFF_DH)

N_EXPERTS = 16
EXPERTS_PER_GROUP = 4
N_GROUPS = N_EXPERTS // EXPERTS_PER_GROUP
D_EXPERT = 512
PAIR_LO = (0, 0, 0, 1, 1, 2)
PAIR_HI = (1, 2, 3, 2, 3, 3)
N_PAIRS = len(PAIR_LO)
N_CLASSES = N_GROUPS * N_PAIRS

LANES = 128
MXU_DIM = 256
HEAD_PAD = LANES
MLA_W = MLA_HEADS * HEAD_PAD
DIFF_QK_W = DIFF_HEADS * 2 * DIFF_DH
DIFF_K_W = DIFF_HEADS * 2 * LANES
DIFF_V_PAD = 2 * LANES
DIFF_V_W = DIFF_HEADS * DIFF_V_PAD

A_CKV = 0
A_KR = A_CKV + MLA_KV_RANK
A_KD = A_KR + LANES
A_VD = A_KD + DIFF_QK_W
A_CQ = A_VD + DIFF_HEADS * DIFF_V
A_QD = A_CQ + MLA_Q_RANK
A_COLS = A_QD + DIFF_QK_W

ROPE_TAB_W = 6 * LANES
F_EXT_W = D_MODEL + LANES

ROW_TILE = 256
MOE_TILE = 512
SORT_TILE = 256
KV_CHUNK = 768
MLA_TQ = 2048
DIFF_TQ = 512
LOOP_STAGES = 4
SHIFTED_LOOP_STAGES = {1: 4, 2: 8}
LOG2E = math.log2(math.e)
VMEM_LIMIT = 56 * 1024 * 1024


def _sigmoid(x):
    return 1.0 / (1.0 + jnp.exp(-x))


def _dot(a, b):
    return jnp.dot(a, b, preferred_element_type=F32)


def _split_bf16(a):
    hi = a.astype(BF16)
    lo = (a - hi.astype(F32)).astype(BF16)
    return hi, lo


def _dot3(a, b):
    a_hi, a_lo = _split_bf16(a)
    b_hi, b_lo = _split_bf16(b)
    return _dot(a_hi, b_hi) + _dot(a_lo, b_hi) + _dot(a_hi, b_lo)


def _mods_kernel(c_ref, w_ref, b_ref, o_ref):
    c = c_ref[...]
    a = c * _sigmoid(c)
    o_ref[...] = _dot3(a, w_ref[...]) + b_ref[...]


def _mods_call(c_all, ada_w, ada_b):
    depth, d, n = ada_w.shape
    tn = 1536
    rows = c_all.shape[0]
    return pl.pallas_call(
        _mods_kernel,
        grid=(depth, n // tn),
        in_specs=[
            pl.BlockSpec((rows, d), lambda l, j: (0, 0)),
            pl.BlockSpec((None, d, tn), lambda l, j: (l, 0, j)),
            pl.BlockSpec((None, 1, tn), lambda l, j: (l, 0, j)),
        ],
        out_specs=pl.BlockSpec((None, rows, tn), lambda l, j: (l, 0, j)),
        out_shape=jax.ShapeDtypeStruct((depth, rows, n), F32),
        compiler_params=pltpu.CompilerParams(
            dimension_semantics=("parallel", "parallel"), vmem_limit_bytes=VMEM_LIMIT),
        name="adaln_mods",
    )(c_all, ada_w, ada_b.reshape(depth, 1, n))


def _rms(x, g):
    return x * lax.rsqrt(jnp.mean(x * x, axis=-1, keepdims=True) + EPS) * g


def _seg_mean_sq(x, b_ref):
    sq = (x * x).astype(BF16)
    b = b_ref[...]
    parts = [_dot(sq[:, c:c + MXU_DIM], b) for c in range(0, x.shape[1], MXU_DIM)]
    return parts[0] if len(parts) == 1 else jnp.concatenate(parts, axis=1)


def _rope(x, cos, sin_a, sin_b, half):
    return (x * cos + pltpu.roll(x, LANES - half, axis=1) * sin_a
            + pltpu.roll(x, half, axis=1) * sin_b)


def _proj_kernel(*refs, fused_residual):
    if fused_residual:
        x_ref, y_ref, modp_ref = refs[:3]
        refs = refs[3:]
        x = x_ref[...] + modp_ref[5:6, :] * y_ref[...]
    else:
        x = refs[0][...]
        refs = refs[1:]
    (mod_ref, g_ref, win_ref, wukv_ref, wuq_ref, gkv_ref, gcq_ref, gk_ref, gkr_ref, gq_ref, gkd_ref, gqd_ref,
     bk_ref, bq_ref, bd_ref, vone_ref, qaug_ref, qdaug_ref, rope_ref) = refs[:19]
    outs = refs[19:]
    if fused_residual:
        outs[0][...] = x
        outs = outs[1:]
    kmla_ref, vmla_ref, qmla_ref, kd_ref, vd_ref, qd_ref = outs
    h = _rms(x, g_ref[...]) * (1.0 + mod_ref[1:2, :]) + mod_ref[0:1, :]
    p = _dot(h.astype(BF16), win_ref[...])

    cos_m = rope_ref[:, 0 * LANES:1 * LANES]
    sin_ma = rope_ref[:, 1 * LANES:2 * LANES]
    sin_mb = rope_ref[:, 2 * LANES:3 * LANES]
    cos_d = rope_ref[:, 3 * LANES:4 * LANES]
    sin_da = rope_ref[:, 4 * LANES:5 * LANES]
    sin_db = rope_ref[:, 5 * LANES:6 * LANES]

    ckv = _rms(p[:, A_CKV:A_KR], gkv_ref[...])
    kv = _dot(ckv.astype(BF16), wukv_ref[...])
    kn = kv[:, :MLA_W]
    kn = kn * lax.rsqrt(_seg_mean_sq(kn, bk_ref) + EPS) * gk_ref[...]
    kr = p[:, A_KR:A_KD]
    kr = kr * lax.rsqrt(jnp.sum(kr * kr, axis=-1, keepdims=True) * (1.0 / MLA_ROPE) + EPS) * gkr_ref[...]
    kr = _rope(kr, cos_m, sin_ma, sin_mb, MLA_ROPE // 4)
    tm = x.shape[0]
    lane_blk = lax.broadcasted_iota(jnp.int32, (tm, LANES), 1)
    kr = kr + (lane_blk == MLA_NOPE + MLA_ROPE).astype(F32)
    for hd in range(MLA_HEADS):
        sl = slice(hd * HEAD_PAD, (hd + 1) * HEAD_PAD)
        kmla_ref[:, sl] = (kn[:, sl] + kr).astype(BF16)
    vmla_ref[...] = (kv[:, MLA_W:] + vone_ref[...]).astype(BF16)

    cq = _rms(p[:, A_CQ:A_QD], gcq_ref[...])
    q = _dot(cq.astype(BF16), wuq_ref[...])
    q = q * lax.rsqrt(_seg_mean_sq(q, bq_ref) + EPS) * gq_ref[...]
    for hd in range(MLA_HEADS):
        sl = slice(hd * HEAD_PAD, (hd + 1) * HEAD_PAD)
        qmla_ref[:, sl] = (_rope(q[:, sl], cos_m, sin_ma, sin_mb, MLA_ROPE // 4) + qaug_ref[:, sl]).astype(BF16)

    kd = p[:, A_KD:A_VD]
    kd = kd * lax.rsqrt(_seg_mean_sq(kd, bd_ref) + EPS) * gkd_ref[...]
    qd = p[:, A_QD:A_COLS]
    qd = qd * lax.rsqrt(_seg_mean_sq(qd, bd_ref) + EPS) * gqd_ref[...]
    ones_blk = (lane_blk == 0).astype(BF16)
    two_ones_blk = (lane_blk < 2).astype(BF16)
    qd_shift = jnp.broadcast_to(qdaug_ref[...], (tm, LANES)).astype(BF16)
    for hd in range(DIFF_HEADS):
        sl = slice(hd * LANES, (hd + 1) * LANES)
        lo = slice(2 * hd * LANES, (2 * hd + 1) * LANES)
        hi = slice((2 * hd + 1) * LANES, (2 * hd + 2) * LANES)
        kd_ref[:, lo] = _rope(kd[:, sl], cos_d, sin_da, sin_db, DIFF_DH // 4).astype(BF16)
        kd_ref[:, hi] = two_ones_blk
        qd_ref[:, lo] = _rope(qd[:, sl], cos_d, sin_da, sin_db, DIFF_DH // 4).astype(BF16)
        qd_ref[:, hi] = qd_shift

    vd = p[:, A_VD:A_CQ]
    for hd in range(DIFF_HEADS):
        vd_ref[:, hd * DIFF_V_PAD:hd * DIFF_V_PAD + DIFF_V] = vd[:, hd * DIFF_V:(hd + 1) * DIFF_V].astype(BF16)
        vd_ref[:, hd * DIFF_V_PAD + DIFF_V:(hd + 1) * DIFF_V_PAD] = ones_blk


def _row_class_maps(n_lat_tiles, tiles_per_batch, n_batch):
    def mod_map(i):
        return (jnp.where(i < n_lat_tiles, i // tiles_per_batch, n_batch), 0, 0)

    def rope_map(i):
        return (jnp.where(i < n_lat_tiles, i % tiles_per_batch, tiles_per_batch), 0)

    return mod_map, rope_map


def _const_spec(shape):
    return pl.BlockSpec(shape, lambda i: tuple(0 for _ in shape))


def _proj_call(x_flat, mods, wl, rope_tab, n_batch, seq, prev=None):
    rows = x_flat.shape[0]
    tm = ROW_TILE
    n_lat_tiles = n_batch * seq // tm
    mod_map, rope_map = _row_class_maps(n_lat_tiles, seq // tm, n_batch)
    row_spec = lambda w: pl.BlockSpec((tm, w), lambda i: (i, 0))
    tiles_per_batch = seq // tm

    def kv_map(i):
        lat = i + i // tiles_per_batch + 1
        ctx = (i - n_lat_tiles) * (tiles_per_batch + 1)
        return (jnp.where(i < n_lat_tiles, lat, ctx), 0)

    kv_spec = lambda w: pl.BlockSpec((tm, w), kv_map)
    consts = [wl["g_attn"], wl["w_in_a"], wl["w_ukv"], wl["w_uq"], wl["g_kv"], wl["g_cq"], wl["g_k"], wl["g_kr"],
              wl["g_q"], wl["g_kd"], wl["g_qd"], wl["b_k"], wl["b_q"], wl["b_d"], wl["v_one"], wl["q_aug"],
              wl["qd_aug"]]
    out_w = [MLA_W, MLA_W, MLA_W, DIFF_K_W, DIFF_V_W, DIFF_K_W]
    out_specs = [kv_spec(MLA_W), kv_spec(MLA_W), row_spec(MLA_W), kv_spec(DIFF_K_W), kv_spec(DIFF_V_W),
                 row_spec(DIFF_K_W)]
    mod_spec = pl.BlockSpec((None, N_MOD, D_MODEL), mod_map)
    lead_specs, lead_args = [row_spec(D_MODEL)], [x_flat]
    out_shape = [jax.ShapeDtypeStruct((rows, w), BF16) for w in out_w]
    if prev is not None:
        lead_specs += [row_spec(D_MODEL), mod_spec]
        lead_args += list(prev)
        out_specs = [row_spec(D_MODEL)] + out_specs
        out_shape = [jax.ShapeDtypeStruct((rows, D_MODEL), F32)] + out_shape
    return pl.pallas_call(
        functools.partial(_proj_kernel, fused_residual=prev is not None),
        grid=(rows // tm,),
        in_specs=lead_specs + [mod_spec] + [_const_spec(c.shape) for c in consts]
        + [pl.BlockSpec((tm, ROPE_TAB_W), rope_map)],
        out_specs=out_specs,
        out_shape=out_shape,
        compiler_params=pltpu.CompilerParams(
            dimension_semantics=("parallel",), vmem_limit_bytes=VMEM_LIMIT),
        name="proj_qkv",
    )(*lead_args, mods, *consts, rope_tab)


SAFE_SHIFT = 50.0
SHIFT_MARGIN = 1.02


def _attn_kernel(flag_ref, *refs, n_maps, tq, tk, chunks, lam_init, aliased_out):
    refs = list(refs)
    q_ref, k_ref, v_ref = refs[:3]
    n_in = 3 + (2 if n_maps == 2 else 0) + (1 if aliased_out else 0)
    o_ref = refs[n_in]
    if n_maps == 2:
        lam_ref, subg_ref = refs[3:5]
        qs_ref, s_ref, p_ref, m_ref, a_ref, acc_ref = refs[n_in + 1:]
        qf = q_ref[...].astype(F32)
        lane = lax.broadcasted_iota(jnp.int32, qf.shape, 1)
        keep0 = (lane < DIFF_DH) | (lane == 2 * DIFF_DH)
        keep1 = ((lane >= DIFF_DH) & (lane < 2 * DIFF_DH)) | (lane == 2 * DIFF_DH + 1)
        qs_ref[0:tq, :] = jnp.where(keep0, qf, 0.0).astype(BF16)
        qs_ref[tq:2 * tq, :] = jnp.where(keep1, qf, 0.0).astype(BF16)
        q_src = qs_ref
    else:
        s_ref, p_ref, m_ref, a_ref, acc_ref = refs[n_in + 1:]
        q_src = q_ref
    vw = v_ref.shape[1]
    sum_col = vw // 2

    head, n_mid, tail = chunks
    n_chunks = 1 + n_mid + (1 if tail else 0)

    def geom(j):
        if isinstance(j, int):
            if j == 0:
                return 0, head
            if j <= n_mid:
                return head + (j - 1) * tk, tk
            return head + n_mid * tk, tail
        return pl.multiple_of(head + (j - 1) * tk, math.gcd(head, tk)), tk

    def rows_of(ref, j):
        off, size = geom(j)
        return ref[pl.ds(off, size), :]

    def scores(j, slot):
        size = geom(j)[1]
        s_ref[slot, :, :size] = lax.dot_general(q_src[...], rows_of(k_ref, j), (((1,), (1,)), ((), ())),
                                                preferred_element_type=F32)

    def probs_shifted(j, slot):
        size = geom(j)[1]
        p_ref[slot, :, :size] = jnp.exp2(s_ref[slot, :, :size]).astype(BF16)

    def accumulate_plain(j, slot):
        size = geom(j)[1]
        r = _dot(p_ref[slot, :, :size], rows_of(v_ref, j))
        if isinstance(j, int) and j == 0:
            acc_ref[...] = r
        else:
            acc_ref[...] += r

    def probs_online(j, slot):
        size = geom(j)[1]
        s = s_ref[slot, :, :size]
        mx = jnp.max(s, axis=-1, keepdims=True)
        if isinstance(j, int) and j == 0:
            m_new = mx
        else:
            m_old = m_ref[...]
            m_new = jnp.maximum(m_old, mx)
            a_ref[slot] = jnp.exp2(m_old - m_new)
        m_ref[...] = m_new
        p_ref[slot, :, :size] = jnp.exp2(s - m_new).astype(BF16)

    def accumulate_online(j, slot):
        size = geom(j)[1]
        r = _dot(p_ref[slot, :, :size], rows_of(v_ref, j))
        if isinstance(j, int) and j == 0:
            acc_ref[...] = r
        else:
            acc_ref[...] = acc_ref[...] * a_ref[slot] + r

    def sweep(probs, accumulate, loop_stages):
        def stage(j, slot, with_scores):
            probs(j, slot)
            accumulate(j - 1, 1 - slot)
            if with_scores:
                scores(j + 1, 1 - slot)

        scores(0, 0)
        if n_chunks > 1:
            scores(1, 1)
        probs(0, 0)
        if n_chunks > 1:
            stage(1, 1, n_chunks > 2)
            n_loop = max(n_mid - 2, 0) // loop_stages

            def group(i, carry):
                j0 = 2 + loop_stages * i
                for t in range(loop_stages):
                    stage(j0 + t, t % 2, True)
                return carry

            lax.fori_loop(0, n_loop, group, 0)
            for j in range(2 + loop_stages * n_loop, n_chunks):
                stage(j, j % 2, j + 1 < n_chunks)
        accumulate(n_chunks - 1, (n_chunks - 1) % 2)

    @pl.when(flag_ref[0] == 1)
    def _():
        sweep(probs_shifted, accumulate_plain, SHIFTED_LOOP_STAGES[n_maps])

    @pl.when(flag_ref[0] != 1)
    def _():
        sweep(probs_online, accumulate_online, LOOP_STAGES)

    acc = acc_ref[...]
    inv_l = 1.0 / acc[:, sum_col:sum_col + 1]
    if n_maps == 1:
        o_ref[...] = (acc * inv_l).astype(o_ref.dtype)
    else:
        lf = lam_ref[...]
        lam = (jnp.exp(jnp.sum(lf[0:1, :] * lf[1:2, :], axis=-1, keepdims=True))
               - jnp.exp(jnp.sum(lf[2:3, :] * lf[3:4, :], axis=-1, keepdims=True)) + lam_init)
        o = acc[:, :DIFF_V] * inv_l
        o = o[:tq] - lam * o[tq:]
        o_ref[...] = _rms(o, subg_ref[...]).astype(o_ref.dtype)


def _attn_scratch(n_maps, tq, tk, kw, vw):
    r = n_maps * tq
    qs = [pltpu.VMEM((r, kw), BF16)] if n_maps == 2 else []
    return qs + [pltpu.VMEM((2, r, tk), F32), pltpu.VMEM((2, r, tk), BF16), pltpu.VMEM((r, 1), F32),
                 pltpu.VMEM((2, r, 1), F32), pltpu.VMEM((r, vw), F32)]


def _attn_call(flag, q, k, v, n_batch, seq, n_ctx, n_heads, n_maps, extra, lam_init, tq, with_ctx_queries, name):
    rows = q.shape[0]
    vw = v.shape[1] // n_heads
    kw = k.shape[1] // n_heads
    qw = q.shape[1] // n_heads
    keys = seq + n_ctx
    tk = KV_CHUNK
    nq = seq // tq
    assert seq % tq == 0
    chunks = (tk, keys // tk - 1, keys % tk)
    ctx_blk0 = n_batch * seq // n_ctx
    blocks_per_batch = keys // n_ctx
    extra_specs = [pl.BlockSpec(e.shape, lambda *_: (0, 0)) for e in extra]
    out_shape = jax.ShapeDtypeStruct((rows, n_heads * LANES), BF16)

    kern = functools.partial(_attn_kernel, n_maps=n_maps, tq=tq, tk=tk, chunks=chunks, lam_init=lam_init,
                             aliased_out=False)
    out = pl.pallas_call(
        kern,
        grid_spec=pltpu.PrefetchScalarGridSpec(
            num_scalar_prefetch=1,
            grid=(n_batch, n_heads, nq),
            in_specs=[pl.BlockSpec((tq, qw), lambda b, h, qi, f: (b * nq + qi, h)),
                      pl.BlockSpec((keys, kw), lambda b, h, qi, f: (b, h)),
                      pl.BlockSpec((keys, vw), lambda b, h, qi, f: (b, h))] + extra_specs,
            out_specs=pl.BlockSpec((tq, LANES), lambda b, h, qi, f: (b * nq + qi, h)),
            scratch_shapes=_attn_scratch(n_maps, tq, tk, kw, vw)),
        out_shape=out_shape,
        compiler_params=pltpu.CompilerParams(
            dimension_semantics=("parallel", "parallel", "arbitrary"), vmem_limit_bytes=VMEM_LIMIT),
        name=name,
    )(flag, q, k, v, *extra)
    if not with_ctx_queries:
        return out

    kern_c = functools.partial(_attn_kernel, n_maps=n_maps, tq=n_ctx, tk=n_ctx, chunks=(n_ctx, 0, 0),
                               lam_init=lam_init, aliased_out=True)
    return pl.pallas_call(
        kern_c,
        grid_spec=pltpu.PrefetchScalarGridSpec(
            num_scalar_prefetch=1,
            grid=(n_batch, n_heads),
            in_specs=[pl.BlockSpec((n_ctx, qw), lambda b, h, f: (ctx_blk0 + b, h)),
                      pl.BlockSpec((n_ctx, kw), lambda b, h, f: (b * blocks_per_batch, h)),
                      pl.BlockSpec((n_ctx, vw), lambda b, h, f: (b * blocks_per_batch, h))] + extra_specs
            + [pl.BlockSpec(memory_space=pl.ANY)],
            out_specs=pl.BlockSpec((n_ctx, LANES), lambda b, h, f: (ctx_blk0 + b, h)),
            scratch_shapes=_attn_scratch(n_maps, n_ctx, n_ctx, kw, vw)),
        out_shape=out_shape,
        input_output_aliases={4 + len(extra): 0},
        compiler_params=pltpu.CompilerParams(
            dimension_semantics=("parallel", "parallel"), vmem_limit_bytes=VMEM_LIMIT),
        name=name + "_ctx",
    )(flag, q, k, v, *extra, out)


def _route(logits, bias):
    lane = lax.broadcasted_iota(jnp.int32, logits.shape, 1)
    is_exp = (lane >= 1) & (lane <= N_EXPERTS)
    grp = (lane - 1) >> 2
    neg = jnp.float32(-jnp.inf)
    big = jnp.int32(1 << 20)
    aff = _sigmoid(logits)
    sel = jnp.where(is_exp, aff + bias, neg)

    def first_argmax(v):
        mx = jnp.max(v, axis=-1, keepdims=True)
        ix = jnp.min(jnp.where(v == mx, lane, big), axis=-1, keepdims=True)
        return mx, ix

    best = None
    for g in range(N_GROUPS):
        vg = jnp.where(grp == g, sel, neg)
        m1, i1 = first_argmax(vg)
        m2, i2 = first_argmax(jnp.where(lane == i1, neg, vg))
        score = m1 + m2
        if best is None:
            best = (score, i1, i2)
        else:
            upd = score > best[0]
            best = (jnp.where(upd, score, best[0]), jnp.where(upd, i1, best[1]), jnp.where(upd, i2, best[2]))
    _, i1, i2 = best
    a1 = jnp.sum(jnp.where(lane == i1, aff, 0.0), axis=-1, keepdims=True)
    a2 = jnp.sum(jnp.where(lane == i2, aff, 0.0), axis=-1, keepdims=True)
    tot = a1 + a2
    e1 = i1 - 1
    e2 = i2 - 1
    swap = e2 < e1
    lo = jnp.where(swap, e2, e1)
    hi = jnp.where(swap, e1, e2)
    w_lo = jnp.where(swap, a2, a1) / tot
    w_hi = jnp.where(swap, a1, a2) / tot
    l = lo & (EXPERTS_PER_GROUP - 1)
    h = hi & (EXPERTS_PER_GROUP - 1)
    cls = (lo >> 2) * N_PAIRS + ((l * (7 - l)) >> 1) + (h - l - 1)
    return cls, w_lo, w_hi


def _merge_kernel(x_ref, mod_ref, ga_ref, gf_ref, om_ref, od_ref, wgate_ref, wom_ref, wod_ref, wout_ref,
                  rw_ref, rb_ref, tri_ref, xo_ref, f_ref, pos_ref, cnt_ref, *, cap):
    @pl.when(pl.program_id(0) == 0)
    def _():
        cnt_ref[...] = jnp.zeros_like(cnt_ref)

    x = x_ref[...]
    h = _rms(x, ga_ref[...]) * (1.0 + mod_ref[1:2, :]) + mod_ref[0:1, :]
    gates = _sigmoid(_dot(h.astype(BF16), wgate_ref[...]))
    y = (gates[:, :D_MODEL] * _dot(om_ref[...], wom_ref[...])
         + gates[:, D_MODEL:] * _dot(od_ref[...], wod_ref[...]))
    xn = x + mod_ref[2:3, :] * _dot(y.astype(BF16), wout_ref[...])
    xo_ref[...] = xn
    f = _rms(xn, gf_ref[...]) * (1.0 + mod_ref[4:5, :]) + mod_ref[3:4, :]
    cls, w_lo, w_hi = _route(_dot3(f, rw_ref[...]), rb_ref[...])
    lane = lax.broadcasted_iota(jnp.int32, (x.shape[0], LANES), 1)
    f_ref[:, :D_MODEL] = f
    f_ref[:, D_MODEL:] = jnp.where(lane == 0, w_lo, jnp.where(lane == 1, w_hi, 0.0))
    onehot = (lane == cls).astype(F32)
    before = _dot(tri_ref[...], onehot.astype(BF16)) + cnt_ref[...]
    rank = jnp.sum(onehot * before, axis=-1, keepdims=True)
    cnt_ref[...] += jnp.sum(onehot, axis=0, keepdims=True)
    pos = cls.astype(F32) * float(cap) + rank
    pos_ref[...] = jnp.where(lane == 0, pos, 0.0)


def _merge_call(x_flat, mods, o_mla, o_diff, wl, router_w, router_b, n_batch, seq, rows):
    tm = ROW_TILE
    mod_map, _ = _row_class_maps(n_batch * seq // tm, seq // tm, n_batch)
    row_spec = lambda w: pl.BlockSpec((tm, w), lambda i: (i, 0))
    tri = (jnp.arange(tm)[:, None] > jnp.arange(tm)[None, :]).astype(BF16)
    consts = [wl["w_gate"], wl["w_o_mla"], wl["w_o_diff"], wl["w_out"], router_w, router_b, tri]
    return pl.pallas_call(
        functools.partial(_merge_kernel, cap=rows),
        grid=(rows // tm,),
        in_specs=[row_spec(D_MODEL), pl.BlockSpec((None, N_MOD, D_MODEL), mod_map),
                  _const_spec(wl["g_attn"].shape), _const_spec(wl["g_ffn"].shape),
                  row_spec(MLA_W), row_spec(DIFF_HEADS * LANES)]
        + [_const_spec(c.shape) for c in consts],
        out_specs=[row_spec(D_MODEL), row_spec(F_EXT_W), row_spec(LANES), _const_spec((1, LANES))],
        out_shape=[jax.ShapeDtypeStruct((rows, D_MODEL), F32), jax.ShapeDtypeStruct((rows, F_EXT_W), F32),
                   jax.ShapeDtypeStruct((rows, LANES), F32), jax.ShapeDtypeStruct((1, LANES), F32)],
        compiler_params=pltpu.CompilerParams(
            dimension_semantics=("arbitrary",), vmem_limit_bytes=VMEM_LIMIT),
        name="merge_router",
    )(x_flat, mods, wl["g_attn"], wl["g_ffn"], o_mla, o_diff, *consts)


def _moe_plan(pos_rec, counts_rec, n_rows):
    t = SORT_TILE
    tiles_per_bucket = n_rows // t
    counts = counts_rec[0, :N_CLASSES].astype(jnp.int32)
    pos = pos_rec[:, 0].astype(jnp.int32)
    src = jnp.zeros((N_CLASSES * n_rows,), jnp.int32).at[pos].set(jnp.arange(n_rows, dtype=jnp.int32))
    class_tiles = (counts + t - 1) // t
    tile_end = jnp.cumsum(class_tiles)
    n_tiles = tiles_per_bucket + N_CLASSES
    tile = jnp.arange(n_tiles, dtype=jnp.int32)
    tile_cls = jnp.minimum(jnp.searchsorted(tile_end, tile, side="right", method="compare_all"),
                           N_CLASSES - 1).astype(jnp.int32)
    in_class = tile - (tile_end - class_tiles)[tile_cls]
    n_used = tile_end[-1].reshape(1)
    used = tile < n_used[0]
    blk = jnp.where(used, tile_cls * tiles_per_bucket + in_class, 0)
    n_valid = jnp.where(used, jnp.clip(counts[tile_cls] - in_class * t, 0, t), 0)
    base = (tile_cls // N_PAIRS) * EXPERTS_PER_GROUP
    pair = tile_cls % N_PAIRS
    ea = base + jnp.asarray(PAIR_LO, jnp.int32)[pair]
    eb = base + jnp.asarray(PAIR_HI, jnp.int32)[pair]
    return ea, eb, n_used, blk, n_valid, src.reshape(N_CLASSES * tiles_per_bucket, 1, t)


def _moe_sparse_kernel(ea_ref, eb_ref, nu_ref, blk_ref, nv_ref, src0_ref, srcn_ref, srcc_ref, f_hbm,
                       wgu_s, wd_s, wgu_a, wd_a, wgu_b, wd_b, y_hbm, xbuf, ybuf, gsem, ssem, *, n_rows):
    del ea_ref, eb_ref, blk_ref
    t = SORT_TILE
    i = pl.program_id(0)
    n_used = nu_ref[0]
    n_valid = nv_ref[i]
    slot = i % 2

    def start_gather(idx_ref, s):
        for r in range(t):
            pltpu.make_async_copy(f_hbm.at[pl.ds(idx_ref[0, r], 1), :], xbuf.at[s, pl.ds(r, 1), :],
                                  gsem.at[s]).start(priority=r % 2)

    def gather_done(s):
        return pltpu.make_async_copy(f_hbm.at[pl.ds(0, t), :], xbuf.at[s], gsem.at[s])

    def scatter_done(s):
        return pltpu.make_async_copy(ybuf.at[s], y_hbm.at[pl.ds(0, t), :], ssem.at[s])

    @pl.when(i == 0)
    def _():
        start_gather(src0_ref, 0)

    @pl.when(i < n_used)
    def _():
        gather_done(slot).wait()
        start_gather(srcn_ref, 1 - slot)
        x = xbuf[slot, :, :D_MODEL].astype(BF16)
        w = xbuf[slot, :, D_MODEL:]

        def ffn(wgu_ref, wd_ref, scale):
            gu = _dot(x, wgu_ref[...])
            g = gu[:, :D_EXPERT]
            he = g * _sigmoid(g) * gu[:, D_EXPERT:]
            if scale is not None:
                he = he * scale
            return _dot(he.astype(BF16), wd_ref[...])

        ybuf[slot] = ffn(wgu_s, wd_s, None) + ffn(wgu_a, wd_a, w[:, 0:1]) + ffn(wgu_b, wd_b, w[:, 1:2])
        for r in range(t):
            dst = jnp.where(r < n_valid, srcc_ref[0, r], n_rows + slot * t + r)
            pltpu.make_async_copy(ybuf.at[slot, pl.ds(r, 1), :], y_hbm.at[pl.ds(dst, 1), :],
                                  ssem.at[slot]).start(priority=r % 2)

        @pl.when(i >= 1)
        def _():
            scatter_done(1 - slot).wait()

        @pl.when(i == n_used - 1)
        def _():
            scatter_done(slot).wait()
            gather_done(1 - slot).wait()


def _moe_sparse_call(f_ext, plan, w_gu, w_down, n_rows):
    ea, eb, n_used, blk, n_valid, src = plan
    t = SORT_TILE
    n_tiles = blk.shape[0]
    smem_spec = lambda imap: pl.BlockSpec((None, 1, t), imap, memory_space=pltpu.SMEM)
    wspec = lambda shape, imap: pl.BlockSpec((None,) + shape, imap)
    gu_shape = (D_MODEL, 2 * D_EXPERT)
    dn_shape = (D_EXPERT, D_MODEL)
    grid_spec = pltpu.PrefetchScalarGridSpec(
        num_scalar_prefetch=5,
        grid=(n_tiles,),
        in_specs=[
            smem_spec(lambda i, ea, eb, nu, blk, nv: (blk[0], 0, 0)),
            smem_spec(lambda i, ea, eb, nu, blk, nv: (blk[jnp.minimum(i + 1, n_tiles - 1)], 0, 0)),
            smem_spec(lambda i, ea, eb, nu, blk, nv: (blk[i], 0, 0)),
            pl.BlockSpec(memory_space=pl.ANY),
            wspec(gu_shape, lambda i, ea, eb, nu, blk, nv: (0, 0, 0)),
            wspec(dn_shape, lambda i, ea, eb, nu, blk, nv: (0, 0, 0)),
            wspec(gu_shape, lambda i, ea, eb, nu, blk, nv: (ea[i] + 1, 0, 0)),
            wspec(dn_shape, lambda i, ea, eb, nu, blk, nv: (ea[i] + 1, 0, 0)),
            wspec(gu_shape, lambda i, ea, eb, nu, blk, nv: (eb[i] + 1, 0, 0)),
            wspec(dn_shape, lambda i, ea, eb, nu, blk, nv: (eb[i] + 1, 0, 0)),
        ],
        out_specs=pl.BlockSpec(memory_space=pl.ANY),
        scratch_shapes=[pltpu.VMEM((2, t, F_EXT_W), F32), pltpu.VMEM((2, t, D_MODEL), F32),
                        pltpu.SemaphoreType.DMA((2,)), pltpu.SemaphoreType.DMA((2,))],
    )
    return pl.pallas_call(
        functools.partial(_moe_sparse_kernel, n_rows=n_rows),
        grid_spec=grid_spec,
        out_shape=jax.ShapeDtypeStruct((n_rows + 2 * t, D_MODEL), F32),
        compiler_params=pltpu.CompilerParams(
            dimension_semantics=("arbitrary",), vmem_limit_bytes=VMEM_LIMIT),
        name="moe_sorted",
    )(ea, eb, n_used, blk, n_valid, src, src, src, f_ext, w_gu, w_down, w_gu, w_down, w_gu, w_down)


def _residual_kernel(x_ref, y_ref, mod_ref, o_ref):
    o_ref[...] = x_ref[...] + mod_ref[5:6, :] * y_ref[...]


def _residual_call(x_flat, y, mods, n_batch, seq, out_rows):
    tm = MOE_TILE
    mod_map, _ = _row_class_maps(n_batch * seq // tm, seq // tm, n_batch)
    row_spec = pl.BlockSpec((tm, D_MODEL), lambda i: (i, 0))
    return pl.pallas_call(
        _residual_kernel,
        grid=(out_rows // tm,),
        in_specs=[row_spec, row_spec, pl.BlockSpec((None, N_MOD, D_MODEL), mod_map)],
        out_specs=row_spec,
        out_shape=jax.ShapeDtypeStruct((out_rows, D_MODEL), F32),
        compiler_params=pltpu.CompilerParams(
            dimension_semantics=("parallel",), vmem_limit_bytes=VMEM_LIMIT),
        name="ffn_residual",
    )(x_flat, y, mods)


def _axial_angles(seq, d_rope):
    d_axis = d_rope // 2
    inv = ROPE_BASE ** (-jnp.arange(0, d_axis, 2, dtype=F32) / d_axis)
    t = jnp.arange(seq, dtype=jnp.int32)
    r = (t // GRID_W).astype(F32)
    col = (t % GRID_W).astype(F32)
    ang_r = r[:, None] * inv
    ang_c = col[:, None] * inv
    cos = jnp.concatenate([jnp.cos(ang_r)] * 2 + [jnp.cos(ang_c)] * 2, axis=1)
    sin = jnp.concatenate([jnp.sin(ang_r)] * 2 + [jnp.sin(ang_c)] * 2, axis=1)
    nf = d_rope // 4
    first = jnp.concatenate([jnp.ones((nf,), F32), jnp.zeros((nf,), F32)] * 2)
    return cos, -sin * first, sin * (1.0 - first)


def _rope_table(seq, tile):
    cm, sma, smb = _axial_angles(seq, MLA_ROPE)
    pad_l = jnp.zeros((seq, MLA_NOPE), F32)
    pad_r = jnp.zeros((seq, HEAD_PAD - MLA_NOPE - MLA_ROPE), F32)
    cos_m = jnp.concatenate([pad_l + 1.0, cm, pad_r + 1.0], axis=1)
    sin_ma = jnp.concatenate([pad_l, sma, pad_r], axis=1)
    sin_mb = jnp.concatenate([pad_l, smb, pad_r], axis=1)
    cd, sda, sdb = _axial_angles(seq, DIFF_DH)
    tab = jnp.concatenate([cos_m, sin_ma, sin_mb] + [jnp.concatenate([a, a], axis=1) for a in (cd, sda, sdb)],
                          axis=1)
    ident = jnp.concatenate([jnp.ones((tile, LANES), F32), jnp.zeros((tile, 2 * LANES), F32)] * 2, axis=1)
    return jnp.concatenate([tab, ident], axis=0)


def _block_diag_mean(seg_sizes):
    m = jnp.zeros((MXU_DIM, MXU_DIM), F32)
    off = 0
    for size, active in seg_sizes:
        if active:
            m = m.at[off:off + size, off:off + size].set(1.0 / size)
        off += size
    assert off == MXU_DIM
    return m.astype(BF16)


def _prep_weights(attn_norm_g, ffn_norm_g, w_in, mla_q_norm_g, mla_w_uq, mla_kv_norm_g, mla_w_ukv, mla_q_g,
                  mla_k_g, diff_q_g, diff_k_g, diff_sub_g, w_o_mla, w_o_diff, w_out, moe_w_gate, moe_w_up,
                  moe_w_down, sh_w_gate, sh_w_up, sh_w_down):
    depth = w_in.shape[0]
    d = D_MODEL
    z = lambda w: jnp.zeros((depth, d, w), F32)
    off_kr = MLA_KV_RANK
    off_kd = off_kr + MLA_ROPE
    off_vd = off_kd + DIFF_QK_W
    off_cq = off_vd + DIFF_HEADS * DIFF_V
    off_qd = off_cq + MLA_Q_RANK
    off_gate = off_qd + DIFF_QK_W
    w_in_a = jnp.concatenate([
        w_in[:, :, :off_kr], z(MLA_NOPE), w_in[:, :, off_kr:off_kd], z(HEAD_PAD - MLA_NOPE - MLA_ROPE),
        w_in[:, :, off_kd:off_gate]], axis=2).astype(BF16)
    w_gate = w_in[:, :, off_gate:].astype(BF16)

    ukv = mla_w_ukv.reshape(depth, MLA_KV_RANK, MLA_HEADS, MLA_NOPE + MLA_V)
    pad_k = jnp.zeros((depth, MLA_KV_RANK, MLA_HEADS, HEAD_PAD - MLA_NOPE), F32)
    pad_v = jnp.zeros((depth, MLA_KV_RANK, MLA_HEADS, HEAD_PAD - MLA_V), F32)
    w_ukv = jnp.concatenate([
        jnp.concatenate([ukv[..., :MLA_NOPE], pad_k], axis=-1).reshape(depth, MLA_KV_RANK, MLA_W),
        jnp.concatenate([ukv[..., MLA_NOPE:], pad_v], axis=-1).reshape(depth, MLA_KV_RANK, MLA_W)],
        axis=-1).astype(BF16)
    uq = mla_w_uq.reshape(depth, MLA_Q_RANK, MLA_HEADS, MLA_NOPE + MLA_ROPE)
    w_uq = jnp.pad(uq, ((0, 0), (0, 0), (0, 0), (0, HEAD_PAD - MLA_NOPE - MLA_ROPE))).reshape(
        depth, MLA_Q_RANK, MLA_W).astype(BF16)
    wom = w_o_mla.reshape(depth, MLA_HEADS, MLA_V, d)
    w_o_mla_p = jnp.pad(wom, ((0, 0), (0, 0), (0, HEAD_PAD - MLA_V), (0, 0))).reshape(depth, MLA_W, d).astype(BF16)

    def per_head(g96, scale):
        g = jnp.pad(g96 * scale, ((0, 0), (0, HEAD_PAD - MLA_NOPE - MLA_ROPE)))
        return jnp.tile(g, (1, MLA_HEADS)).reshape(depth, 1, MLA_W)

    k_nope_only = jnp.concatenate([mla_k_g[:, :MLA_NOPE], jnp.zeros((depth, MLA_ROPE), F32)], axis=1)
    g_k = per_head(k_nope_only, 1.0)
    g_kr = jnp.concatenate([jnp.zeros((depth, MLA_NOPE), F32), mla_k_g[:, MLA_NOPE:],
                            jnp.zeros((depth, HEAD_PAD - MLA_NOPE - MLA_ROPE), F32)], axis=1).reshape(depth, 1, LANES)
    g_q = per_head(mla_q_g, MLA_SCALE * LOG2E)
    g_kd = jnp.tile(diff_k_g.reshape(depth, 2 * DIFF_DH), (1, DIFF_HEADS)).reshape(depth, 1, DIFF_QK_W)
    g_qd = jnp.tile(diff_q_g.reshape(depth, 2 * DIFF_DH) * (DIFF_SCALE * LOG2E),
                    (1, DIFF_HEADS)).reshape(depth, 1, DIFF_QK_W)
    lam_inits = [0.8 - 0.6 * math.exp(-0.3 * l) for l in range(depth)]
    g_sub = diff_sub_g * (1.0 - jnp.asarray(lam_inits, F32))[:, None]

    def seg_norm2(g, n):
        return n * jnp.max(g * g, axis=-1)

    q_n2 = seg_norm2(mla_q_g[:, :MLA_NOPE], MLA_NOPE) + seg_norm2(mla_q_g[:, MLA_NOPE:], MLA_ROPE)
    k_n2 = seg_norm2(mla_k_g[:, :MLA_NOPE], MLA_NOPE) + seg_norm2(mla_k_g[:, MLA_NOPE:], MLA_ROPE)
    shift_m = (jnp.sqrt(q_n2 * k_n2) * (MLA_SCALE * LOG2E * SHIFT_MARGIN)).astype(BF16).astype(F32)
    shift_d = (jnp.sqrt(seg_norm2(diff_q_g, DIFF_DH) * seg_norm2(diff_k_g, DIFF_DH))
               * (DIFF_SCALE * LOG2E * SHIFT_MARGIN)).astype(BF16).astype(F32)
    flag_m = shift_m <= SAFE_SHIFT
    flag_d = jnp.max(shift_d, axis=-1) <= SAFE_SHIFT
    aug_lane = (jnp.arange(HEAD_PAD) == MLA_NOPE + MLA_ROPE).astype(F32)
    q_aug = jnp.tile(aug_lane[None, :] * jnp.where(flag_m, -shift_m, 0.0)[:, None], (1, MLA_HEADS))
    qd_aug = jnp.pad(jnp.where(flag_d[:, None], -shift_d, 0.0), ((0, 0), (0, LANES - 2)))

    v_one = jnp.tile((jnp.arange(HEAD_PAD) == MLA_V).astype(F32), MLA_HEADS).reshape(1, MLA_W)
    b_k = _block_diag_mean([(MLA_NOPE, True), (HEAD_PAD - MLA_NOPE, False)] * 2)
    b_q = _block_diag_mean([(MLA_NOPE, True), (MLA_ROPE, True), (HEAD_PAD - MLA_NOPE - MLA_ROPE, False)] * 2)
    b_d = _block_diag_mean([(DIFF_DH, True)] * 4)

    w_gu = jnp.concatenate([
        jnp.concatenate([sh_w_gate, sh_w_up], axis=-1)[:, None].astype(BF16),
        jnp.concatenate([moe_w_gate.astype(BF16), moe_w_up.astype(BF16)], axis=-1)], axis=1)
    w_dn = jnp.concatenate([sh_w_down[:, None].astype(BF16), moe_w_down.astype(BF16)], axis=1)

    layers = []
    for l in range(depth):
        layers.append({
            "g_attn": attn_norm_g[l].reshape(1, d), "g_ffn": ffn_norm_g[l].reshape(1, d),
            "w_in_a": w_in_a[l], "w_gate": w_gate[l], "w_ukv": w_ukv[l], "w_uq": w_uq[l],
            "g_kv": mla_kv_norm_g[l].reshape(1, MLA_KV_RANK), "g_cq": mla_q_norm_g[l].reshape(1, MLA_Q_RANK),
            "g_k": g_k[l], "g_kr": g_kr[l], "g_q": g_q[l], "g_kd": g_kd[l], "g_qd": g_qd[l],
            "g_sub": g_sub[l].reshape(1, DIFF_V), "b_k": b_k, "b_q": b_q, "b_d": b_d, "v_one": v_one,
            "w_o_mla": w_o_mla_p[l], "w_o_diff": w_o_diff[l].astype(BF16), "w_out": w_out[l].astype(BF16),
            "w_gu": w_gu[l], "w_dn": w_dn[l], "lam_init": lam_inits[l],
            "q_aug": q_aug[l].reshape(1, MLA_W), "qd_aug": qd_aug[l].reshape(1, LANES),
            "flag_m": flag_m[l].astype(jnp.int32).reshape(1), "flag_d": flag_d[l].astype(jnp.int32).reshape(1),
        })
    return layers


def kernel(x, c, ctx, c_ctx, ada_w, ada_b, attn_norm_g, ffn_norm_g, w_in, mla_q_norm_g, mla_w_uq, mla_kv_norm_g, mla_w_ukv, mla_q_g, mla_k_g, diff_q_g, diff_k_g, diff_lam, diff_sub_g, w_o_mla, w_o_diff, w_out, router_w, router_b, moe_w_gate, moe_w_up, moe_w_down, sh_w_gate, sh_w_up, sh_w_down):
    n_batch, seq, d = x.shape
    n_ctx = ctx.shape[1]
    depth = w_in.shape[0]
    assert d == D_MODEL and n_ctx == ROW_TILE and seq % GRID_W == 0
    assert seq % MLA_TQ == 0 and seq % DIFF_TQ == 0 and ((seq + n_ctx) % KV_CHUNK) % 16 == 0
    assert (n_batch * seq) % MOE_TILE == 0 and (n_batch * n_ctx) % MOE_TILE == 0
    lat_rows = n_batch * seq
    all_rows = lat_rows + n_batch * n_ctx

    layers = _prep_weights(attn_norm_g, ffn_norm_g, w_in, mla_q_norm_g, mla_w_uq, mla_kv_norm_g, mla_w_ukv,
                           mla_q_g, mla_k_g, diff_q_g, diff_k_g, diff_sub_g, w_o_mla, w_o_diff, w_out,
                           moe_w_gate, moe_w_up, moe_w_down, sh_w_gate, sh_w_up, sh_w_down)
    rope_tab = _rope_table(seq, ROW_TILE)
    rw = jnp.pad(router_w, ((0, 0), (1, LANES - 1 - N_EXPERTS)))
    rb = jnp.pad(router_b, (1, LANES - 1 - N_EXPERTS)).reshape(1, LANES)

    pad_rows = (-(n_batch + 1)) % 8
    c_all = jnp.concatenate([c, c_ctx[None, :], jnp.zeros((pad_rows, d), F32)], axis=0)
    mods_all = _mods_call(c_all, ada_w, ada_b).reshape(depth, c_all.shape[0], N_MOD, d)

    xf = jnp.concatenate([x.reshape(lat_rows, d), ctx.reshape(n_batch * n_ctx, d)], axis=0)
    pending = None
    for l in range(depth):
        wl = layers[l]
        mods = mods_all[l]
        if pending is None:
            k_m, v_m, q_m, k_d, v_d, q_d = _proj_call(xf, mods, wl, rope_tab, n_batch, seq)
        else:
            xf, k_m, v_m, q_m, k_d, v_d, q_d = _proj_call(xf, mods, wl, rope_tab, n_batch, seq, prev=pending)
        last = l == depth - 1
        o_m = _attn_call(wl["flag_m"], q_m, k_m, v_m, n_batch, seq, n_ctx, MLA_HEADS, 1, [], 0.0, MLA_TQ,
                         not last, "attn_mla")
        o_d = _attn_call(wl["flag_d"], q_d, k_d, v_d, n_batch, seq, n_ctx, DIFF_HEADS, 2,
                         [diff_lam[l], wl["g_sub"]], wl["lam_init"], DIFF_TQ, not last, "attn_diff")
        out_rows = lat_rows if last else all_rows
        xf, f_ext, pos, counts = _merge_call(xf, mods, o_m, o_d, wl, rw, rb, n_batch, seq, out_rows)
        y = _moe_sparse_call(f_ext, _moe_plan(pos, counts, out_rows), wl["w_gu"], wl["w_dn"], out_rows)
        pending = (y, mods)
    xf = _residual_call(xf, pending[0], pending[1], n_batch, seq, lat_rows)
    return xf.reshape(n_batch, seq, d)
```
